```python
import math
import jax, jax.numpy as jnp
from jax import lax
import numpy as np

D_MODEL = 1024
BATCH = 4
SEQ = 4096
DEPTH = 1
DEC_BATCH = 128
DEC_SEQ = 1
PAST_LEN = 2048
PAGE_SIZE = 128

N_META = 16
M_HEADS = 4
M_DV = D_MODEL // M_HEADS
M_DK = M_DV // 2
M_CHUNK = 64
GATE_CAP = 15.0
F_HEADS = 16
F_DH = D_MODEL // F_HEADS
Q_BLOCK = 128
ATTN_SCALE = F_DH ** -0.5
N_EXPERTS = 32
TOP_K = 4
D_FF = D_MODEL
SWIGLU_LIMIT = 7.0
SWIGLU_ALPHA = 1.702
EPS = 1e-6

IN_SPLITS = (M_HEADS * M_DK, M_HEADS * M_DK, M_HEADS * M_DV, M_HEADS * M_DV, M_HEADS, M_HEADS,
             F_HEADS * F_DH, F_HEADS * F_DH, F_HEADS * F_DH, F_HEADS, 2 * D_MODEL)
N_IN = sum(IN_SPLITS)
IN_OFFSETS = tuple(int(v) for v in np.cumsum(IN_SPLITS)[:-1])

kernel_name = "hybrid_mlstm_fox_moe_step"


def rmsnorm(x, g):
    xf = x.astype(jnp.float32)
    y = xf * lax.rsqrt(jnp.mean(xf * xf, axis=-1, keepdims=True) + EPS)
    return (y * g.astype(jnp.float32)).astype(x.dtype)


def softcap(x):
    return GATE_CAP * jnp.tanh(x / GATE_CAP)


def project(h, w_in, b_i, b_f, b_ff, q_norm, k_norm):
    B, L, _ = h.shape
    z = h @ w_in
    mq, mk, mv, mo, mi, mf, fq, fk, fv, ff, gates = jnp.split(z, IN_OFFSETS, axis=-1)
    mq = mq.reshape(B, L, M_HEADS, M_DK) * (M_DK ** -0.5)
    mk = mk.reshape(B, L, M_HEADS, M_DK)
    mv = mv.reshape(B, L, M_HEADS, M_DV)
    ig = softcap(mi.astype(jnp.float32) + b_i.astype(jnp.float32))
    lfm = jax.nn.log_sigmoid(softcap(mf.astype(jnp.float32) + b_f.astype(jnp.float32)))
    fq = rmsnorm(fq.reshape(B, L, F_HEADS, F_DH), q_norm)
    fk = rmsnorm(fk.reshape(B, L, F_HEADS, F_DH), k_norm)
    fv = fv.reshape(B, L, F_HEADS, F_DH)
    lff = jax.nn.log_sigmoid(ff.astype(jnp.float32) + b_ff.astype(jnp.float32))
    return mq, mk, mv, mo, ig, lfm, fq, fk, fv, lff, gates


def mlstm_chunk(q, k, v, ig, lf, C, n, m):
    f32 = jnp.float32
    C = C.astype(f32); n = n.astype(f32); m = m.astype(f32)
    qf, kf, vf = q.astype(f32), k.astype(f32), v.astype(f32)
    L = q.shape[1]
    b = jnp.cumsum(lf, axis=1).transpose(0, 2, 1)
    igT = ig.transpose(0, 2, 1)
    causal = jnp.tril(jnp.ones((L, L), dtype=bool))
    logD = jnp.where(causal, b[..., :, None] - b[..., None, :] + igT[..., None, :], -jnp.inf)
    inter = b + m[..., None]
    m_row = jnp.maximum(inter, jnp.max(logD, axis=-1))
    s = jnp.einsum('blhk,bshk->bhls', qf, kf) * jnp.exp(logD - m_row[..., None])
    w_int = jnp.exp(inter - m_row)
    num = (jnp.einsum('bhls,bshv->bhlv', s, vf)
           + w_int[..., None] * jnp.einsum('blhk,bhvk->bhlv', qf, C))
    den = jnp.sum(s, axis=-1) + w_int * jnp.einsum('blhk,bhk->bhl', qf, n)
    h = num / jnp.maximum(jnp.abs(den), jnp.exp(-m_row))[..., None]
    h = h.transpose(0, 2, 1, 3).astype(q.dtype)
    b_last = b[..., -1]
    log_w = b_last[..., None] - b + igT
    m_new = jnp.maximum(b_last + m, jnp.max(log_w, axis=-1))
    decay = jnp.exp(b_last + m - m_new)
    ws = jnp.exp(log_w - m_new[..., None])
    C_new = decay[..., None, None] * C + jnp.einsum('bhs,bshv,bshk->bhvk', ws, vf, kf)
    n_new = decay[..., None] * n + jnp.einsum('bhs,bshk->bhk', ws, kf)
    return h, C_new, n_new, m_new


def mlstm_prompt(q, k, v, ig, lf):
    B = q.shape[0]
    f32 = jnp.float32
    C0 = jnp.zeros((B, M_HEADS, M_DV, M_DK), f32)
    n0 = jnp.zeros((B, M_HEADS, M_DK), f32)
    m0 = jnp.zeros((B, M_HEADS), f32)
    h_meta, C, n, m = mlstm_chunk(q[:, :N_META], k[:, :N_META], v[:, :N_META],
                                  ig[:, :N_META], lf[:, :N_META], C0, n0, m0)
    nc = (q.shape[1] - N_META) // M_CHUNK

    def to_chunks(a):
        r = a[:, N_META:]
        return r.reshape((B, nc, M_CHUNK) + r.shape[2:]).swapaxes(0, 1)

    def body(carry, xs):
        h, Cn, nn, mn = mlstm_chunk(*xs, *carry)
        return (Cn, nn, mn), h

    (C, n, m), hs = lax.scan(body, (C, n, m), tuple(to_chunks(a) for a in (q, k, v, ig, lf)))
    h_real = hs.swapaxes(0, 1).reshape((B, nc * M_CHUNK, M_HEADS, M_DV))
    return jnp.concatenate([h_meta, h_real], axis=1), C, n, m


def fox_block(qb, cq, pq, k, v, ck, pk):
    s = jnp.einsum('bqhd,bkhd->bhqk', qb, k).astype(jnp.float32) * ATTN_SCALE
    s = s + cq.transpose(0, 2, 1)[..., :, None] - ck.transpose(0, 2, 1)[..., None, :]
    s = jnp.where(pk[None, :] <= pq[:, None], s, -jnp.inf)
    p = jax.nn.softmax(s, axis=-1)
    return jnp.einsum('bhqk,bkhd->bqhd', p.astype(v.dtype), v)


def fox_prompt(q, k, v, lf):
    B, L = q.shape[:2]
    c = jnp.cumsum(lf.astype(jnp.float32), axis=1)
    pos = jnp.arange(L)
    y_meta = fox_block(q[:, :N_META], c[:, :N_META], pos[:N_META],
                       k[:, :N_META], v[:, :N_META], c[:, :N_META], pos[:N_META])
    nb = (L - N_META) // Q_BLOCK

    def one(i):
        start = N_META + i * Q_BLOCK
        qb = lax.dynamic_slice_in_dim(q, start, Q_BLOCK, axis=1)
        cb = lax.dynamic_slice_in_dim(c, start, Q_BLOCK, axis=1)
        return fox_block(qb, cb, start + jnp.arange(Q_BLOCK), k, v, c, pos)

    ys = lax.map(one, jnp.arange(nb))
    y_real = ys.swapaxes(0, 1).reshape(B, nb * Q_BLOCK, F_HEADS, F_DH)
    return jnp.concatenate([y_meta, y_real], axis=1)


def fox_sample(q, k, v, lf, k_past, v_past, lf_past):
    S = q.shape[1]
    P = k_past.shape[1]
    c_past = jnp.cumsum(lf_past.astype(jnp.float32), axis=1)
    c_new = c_past[:, -1:] + jnp.cumsum(lf.astype(jnp.float32), axis=1)
    cq = c_new.transpose(0, 2, 1)
    s_past = (jnp.einsum('bqhd,bkhd->bhqk', q, k_past).astype(jnp.float32) * ATTN_SCALE
              + cq[..., :, None] - c_past.transpose(0, 2, 1)[..., None, :])
    causal = jnp.tril(jnp.ones((S, S), dtype=bool))
    s_new = jnp.where(causal,
                      jnp.einsum('bqhd,bkhd->bhqk', q, k).astype(jnp.float32) * ATTN_SCALE
                      + cq[..., :, None] - cq[..., None, :], -jnp.inf)
    p = jax.nn.softmax(jnp.concatenate([s_past, s_new], axis=-1), axis=-1).astype(v.dtype)
    return (jnp.einsum('bhqk,bkhd->bqhd', p[..., :P], v_past)
            + jnp.einsum('bhqk,bkhd->bqhd', p[..., P:], v))


def merge(h_m, mo, h_f, gates, head_norm, w_bm, w_bf, w_o):
    B, L = h_m.shape[:2]
    hm = rmsnorm(h_m, head_norm.reshape(M_HEADS, M_DV)).reshape(B, L, M_HEADS * M_DV) * jax.nn.sigmoid(mo)
    ya = hm @ w_bm
    yb = h_f.reshape(B, L, F_HEADS * F_DH) @ w_bf
    ga, gb = jnp.split(jax.nn.sigmoid(gates), 2, axis=-1)
    return (ga * ya + gb * yb) @ w_o


def moe(x, w_r, b_r, w1, b1, w2, b2):
    logits = (x @ w_r + b_r).astype(jnp.float32)
    top_v, top_i = lax.top_k(logits, TOP_K)
    probs = jax.nn.softmax(top_v, axis=-1)
    comb = jnp.sum(jax.nn.one_hot(top_i, N_EXPERTS, dtype=jnp.float32) * probs[..., None], axis=-2)
    out = jnp.zeros(x.shape, jnp.float32)
    for e in range(N_EXPERTS):
        g, u = jnp.split(x @ w1[e] + b1[e], 2, axis=-1)
        g = jnp.minimum(g, SWIGLU_LIMIT)
        u = jnp.clip(u, -SWIGLU_LIMIT, SWIGLU_LIMIT)
        a = g * jax.nn.sigmoid(SWIGLU_ALPHA * g) * (u + 1.0)
        out = out + comb[:, e:e + 1] * (a @ w2[e] + b2[e])
    return out.astype(x.dtype)


def setup_inputs(seed: int = 0) -> dict:
    key = jax.random.key(seed)
    ks = jax.random.split(key, 32)
    f32 = jnp.float32
    n_pages = PAST_LEN // PAGE_SIZE
    n_phys = (DEC_BATCH * n_pages * 5) // 4

    def nrm(k, shape, s):
        return jax.random.normal(k, shape, f32) * s

    return {
        "x_prompt": nrm(ks[0], (BATCH, SEQ, D_MODEL), 1.0),
        "x_sample": nrm(ks[1], (DEC_BATCH, DEC_SEQ, D_MODEL), 1.0),
        "cache_k": nrm(ks[2], (DEPTH, n_phys, PAGE_SIZE, F_HEADS, F_DH), 1.0),
        "cache_v": nrm(ks[3], (DEPTH, n_phys, PAGE_SIZE, F_HEADS, F_DH), 1.0),
        "cache_logf": jax.nn.log_sigmoid(3.0 + nrm(ks[4], (DEPTH, n_phys, PAGE_SIZE, F_HEADS), 1.0)),
        "state_C": nrm(ks[5], (DEPTH, DEC_BATCH, M_HEADS, M_DV, M_DK), 0.1),
        "state_n": nrm(ks[6], (DEPTH, DEC_BATCH, M_HEADS, M_DK), 0.1),
        "state_m": nrm(ks[7], (DEPTH, DEC_BATCH, M_HEADS), 1.0),
        "page_table": jax.random.permutation(ks[8], n_phys)[:DEC_BATCH * n_pages]
                          .reshape(DEC_BATCH, n_pages).astype(jnp.int32),
        "meta_tokens": nrm(ks[9], (N_META, D_MODEL), 1.0),
        "ln1": 1.0 + nrm(ks[10], (DEPTH, D_MODEL), 0.05),
        "w_in": nrm(ks[11], (DEPTH, D_MODEL, N_IN), D_MODEL ** -0.5),
        "mlstm_b_i": nrm(ks[12], (DEPTH, M_HEADS), 0.1),
        "mlstm_b_f": 3.0 + nrm(ks[13], (DEPTH, M_HEADS), 0.1),
        "mlstm_head_norm": 1.0 + nrm(ks[14], (DEPTH, M_HEADS * M_DV), 0.05),
        "fox_b_f": 3.0 + nrm(ks[15], (DEPTH, F_HEADS), 0.1),
        "fox_q_norm": 1.0 + nrm(ks[16], (DEPTH, F_DH), 0.05),
        "fox_k_norm": 1.0 + nrm(ks[17], (DEPTH, F_DH), 0.05),
        "w_branch_mlstm": nrm(ks[18], (DEPTH, M_HEADS * M_DV, D_MODEL), (M_HEADS * M_DV) ** -0.5),
        "w_branch_fox": nrm(ks[19], (DEPTH, F_HEADS * F_DH, D_MODEL), (F_HEADS * F_DH) ** -0.5),
        "w_out": nrm(ks[20], (DEPTH, D_MODEL, D_MODEL), D_MODEL ** -0.5),
        "ln2": 1.0 + nrm(ks[21], (DEPTH, D_MODEL), 0.05),
        "w_router": nrm(ks[22], (DEPTH, D_MODEL, N_EXPERTS), D_MODEL ** -0.5),
        "b_router": nrm(ks[23], (DEPTH, N_EXPERTS), 0.01),
        "w_exp_in": nrm(ks[24], (DEPTH, N_EXPERTS, D_MODEL, 2 * D_FF), D_MODEL ** -0.5),
        "b_exp_in": nrm(ks[25], (DEPTH, N_EXPERTS, 2 * D_FF), 0.02),
        "w_exp_out": nrm(ks[26], (DEPTH, N_EXPERTS, D_FF, D_MODEL), D_FF ** -0.5),
        "b_exp_out": nrm(ks[27], (DEPTH, N_EXPERTS, D_MODEL), 0.02),
        "ln_final": 1.0 + nrm(ks[28], (D_MODEL,), 0.05),
    }


def reference(x_prompt, x_sample, cache_k, cache_v, cache_logf, state_C, state_n, state_m, page_table,
              meta_tokens, ln1, w_in, mlstm_b_i, mlstm_b_f, mlstm_head_norm, fox_b_f, fox_q_norm, fox_k_norm,
              w_branch_mlstm, w_branch_fox, w_out, ln2, w_router, b_router, w_exp_in, b_exp_in,
              w_exp_out, b_exp_out, ln_final):
    B = x_prompt.shape[0]
    DB, DS = x_sample.shape[:2]
    xp = jnp.concatenate([jnp.broadcast_to(meta_tokens[None].astype(x_prompt.dtype), (B, N_META, D_MODEL)),
                          x_prompt], axis=1)
    xs = x_sample
    L = xp.shape[1]
    P = page_table.shape[1] * cache_k.shape[2]
    kp_l, vp_l, lfp_l, ks_l, vs_l, lfs_l = [], [], [], [], [], []
    Cp_l, np_l, mp_l, Cs_l, ns_l, ms_l = [], [], [], [], [], []
    for l in range(DEPTH):
        mq, mk, mv, mo, ig, lfm, fq, fk, fv, lff, gates = project(
            rmsnorm(xp, ln1[l]), w_in[l], mlstm_b_i[l], mlstm_b_f[l], fox_b_f[l], fox_q_norm[l], fox_k_norm[l])
        hm, Cp, npr, mp = mlstm_prompt(mq, mk, mv, ig, lfm)
        hf = fox_prompt(fq, fk, fv, lff)
        xp = xp + merge(hm, mo, hf, gates, mlstm_head_norm[l], w_branch_mlstm[l], w_branch_fox[l], w_out[l])
        kp_l.append(fk); vp_l.append(fv); lfp_l.append(lff)
        Cp_l.append(Cp); np_l.append(npr); mp_l.append(mp)
        mq, mk, mv, mo, ig, lfm, fq, fk, fv, lff, gates = project(
            rmsnorm(xs, ln1[l]), w_in[l], mlstm_b_i[l], mlstm_b_f[l], fox_b_f[l], fox_q_norm[l], fox_k_norm[l])
        hm, Cs, ns, ms = mlstm_chunk(mq, mk, mv, ig, lfm, state_C[l], state_n[l], state_m[l])
        k_past = cache_k[l][page_table].reshape(DB, P, F_HEADS, F_DH)
        v_past = cache_v[l][page_table].reshape(DB, P, F_HEADS, F_DH)
        lf_past = cache_logf[l][page_table].reshape(DB, P, F_HEADS)
        hf = fox_sample(fq, fk, fv, lff, k_past, v_past, lf_past)
        xs = xs + merge(hm, mo, hf, gates, mlstm_head_norm[l], w_branch_mlstm[l], w_branch_fox[l], w_out[l])
        ks_l.append(fk); vs_l.append(fv); lfs_l.append(lff)
        Cs_l.append(Cs); ns_l.append(ns); ms_l.append(ms)
        t = jnp.concatenate([rmsnorm(xp, ln2[l]).reshape(-1, D_MODEL),
                             rmsnorm(xs, ln2[l]).reshape(-1, D_MODEL)], axis=0)
        ff = moe(t, w_router[l], b_router[l], w_exp_in[l], b_exp_in[l], w_exp_out[l], b_exp_out[l])
        xp = xp + ff[:B * L].reshape(xp.shape)
        xs = xs + ff[B * L:].reshape(xs.shape)
    y_prompt = rmsnorm(xp, ln_final)[:, N_META:]
    y_sample = rmsnorm(xs, ln_final)
    return (y_prompt, y_sample,
            jnp.stack(kp_l), jnp.stack(vp_l), jnp.stack(lfp_l),
            jnp.stack(ks_l), jnp.stack(vs_l), jnp.stack(lfs_l),
            jnp.stack(Cp_l), jnp.stack(np_l), jnp.stack(mp_l),
            jnp.stack(Cs_l), jnp.stack(ns_l), jnp.stack(ms_l))
```

```python
import functools

import numpy as np
import jax
import jax.numpy as jnp
from jax import lax
from jax.experimental import pallas as pl
from jax.experimental.pallas import tpu as pltpu

F32 = jnp.float32
BF16 = jnp.bfloat16
I32 = jnp.int32

D_MODEL = 1024
N_META = 16
M_HEADS, M_DK, M_DV = 4, 128, 256
F_HEADS, F_DH = 16, 64
N_EXPERTS, TOP_K, D_FF = 32, 4, 1024
GATE_CAP = 15.0
SWIGLU_LIMIT = 7.0
SWIGLU_ALPHA = 1.702
EPS = 1e-6
ATTN_SCALE = F_DH ** -0.5

LANES = 128
ATT_BLOCK = 256
PAD_FRONT = ATT_BLOCK - N_META
M_CHUNK = 128
ROW_TILE = 512
EXP_TILE = 256
GATHER_TILE = 256
COMB_TILE = 128
VMEM_LIMIT = 56 * 1024 * 1024

G_IG = 0
G_LF = M_HEADS
G_FF = 2 * M_HEADS
G_END = G_FF + F_HEADS

ZB_MQK, ZB_MV, ZB_MO, ZB_FQ, ZB_FK, ZB_FV, ZB_GA, ZB_GB = range(8)
N_ZB = 8


def _dot(a, b):
    return jnp.dot(a, b, preferred_element_type=F32)


def _dot_nt(a, b):
    return lax.dot_general(a, b, (((1,), (1,)), ((), ())), preferred_element_type=F32)


def _dot_tn(a, b):
    return lax.dot_general(a, b, (((0,), (0,)), ((), ())), preferred_element_type=F32)


def _split2(x):
    hi = x.astype(BF16)
    lo = (x - hi.astype(F32)).astype(BF16)
    return hi, lo


def _split3(x):
    a = x.astype(BF16)
    r = x - a.astype(F32)
    b = r.astype(BF16)
    c = (r - b.astype(F32)).astype(BF16)
    return a, b, c


def _dot_exact_rhs01(parts, m01):
    acc = _dot(parts[0], m01)
    for p in parts[1:]:
        acc = acc + _dot(p, m01)
    return acc


def _log_sigmoid(x):
    return jnp.minimum(x, 0.0) - jnp.log1p(jnp.exp(-jnp.abs(x)))


def _cparams(sem):
    return pltpu.CompilerParams(dimension_semantics=sem, vmem_limit_bytes=VMEM_LIMIT)


def _head_norm(y, g, r_ref, e_ref):
    ms = _dot((y * y).astype(BF16), r_ref[...])
    rs = lax.rsqrt(ms + EPS)
    rsx = _dot_exact_rhs01(_split2(rs), e_ref[...])
    return y * rsx * g


def _proj_kernel(x_ref, ln_ref, w_ref, wsh_ref, wsl_ref, bias_ref, qn_ref, kn_ref, r_ref, e_ref,
                 z_ref, k32_ref, v32_ref, gs_ref, h_scr):
    j = pl.program_id(1)

    @pl.when(j == 0)
    def _():
        x = x_ref[...]
        ms = jnp.mean(x * x, axis=-1, keepdims=True)
        h = x * lax.rsqrt(ms + EPS) * ln_ref[...]
        hh, hl = _split2(h)
        h_scr[...] = hh
        wsh = wsh_ref[...]
        g = _dot(hh, wsh) + _dot(hl, wsh) + _dot(hh, wsl_ref[...]) + bias_ref[...]
        lane = lax.broadcasted_iota(I32, g.shape, 1)
        cap = GATE_CAP * jnp.tanh(g / GATE_CAP)
        ls = _log_sigmoid(jnp.where(lane < G_FF, cap, g))
        gs_ref[...] = jnp.where(lane < G_LF, cap, jnp.where(lane < G_END, ls, 0.0))

    y = _dot(h_scr[...], w_ref[...])

    @pl.when(j == ZB_MQK)
    def _():
        col = lax.broadcasted_iota(I32, y.shape, 1)
        z_ref[...] = jnp.where(col < M_HEADS * M_DK, y * (M_DK ** -0.5), y).astype(BF16)

    @pl.when(j == ZB_MV)
    def _():
        z_ref[...] = y.astype(BF16)

    @pl.when((j == ZB_MO) | (j == ZB_GA) | (j == ZB_GB))
    def _():
        z_ref[...] = jax.nn.sigmoid(y).astype(BF16)

    @pl.when(j == ZB_FQ)
    def _():
        z_ref[...] = (_head_norm(y, qn_ref[...], r_ref, e_ref) * ATTN_SCALE).astype(BF16)

    @pl.when(j == ZB_FK)
    def _():
        kn = _head_norm(y, kn_ref[...], r_ref, e_ref)
        k32_ref[...] = kn
        z_ref[...] = kn.astype(BF16)

    @pl.when(j == ZB_FV)
    def _():
        v32_ref[...] = y
        z_ref[...] = y.astype(BF16)


def _proj(x, ln, wmain, wsh, wsl, bias, qn, kn, r64, e64, tm):
    t = x.shape[0]
    row = lambda i, j: (i, 0)
    fix = lambda i, j: (0, 0)
    return pl.pallas_call(
        _proj_kernel,
        grid=(t // tm, N_ZB),
        in_specs=[
            pl.BlockSpec((tm, D_MODEL), row),
            pl.BlockSpec((1, D_MODEL), fix),
            pl.BlockSpec((D_MODEL, D_MODEL), lambda i, j: (0, j)),
            pl.BlockSpec((D_MODEL, LANES), fix),
            pl.BlockSpec((D_MODEL, LANES), fix),
            pl.BlockSpec((1, LANES), fix),
            pl.BlockSpec((1, D_MODEL), fix),
            pl.BlockSpec((1, D_MODEL), fix),
            pl.BlockSpec((D_MODEL, LANES), fix),
            pl.BlockSpec((LANES, D_MODEL), fix),
        ],
        out_specs=[
            pl.BlockSpec((tm, D_MODEL), lambda i, j: (i, j)),
            pl.BlockSpec((tm, D_MODEL), row),
            pl.BlockSpec((tm, D_MODEL), row),
            pl.BlockSpec((tm, LANES), row),
        ],
        out_shape=[
            jax.ShapeDtypeStruct((t, N_ZB * D_MODEL), BF16),
            jax.ShapeDtypeStruct((t, D_MODEL), F32),
            jax.ShapeDtypeStruct((t, D_MODEL), F32),
            jax.ShapeDtypeStruct((t, LANES), F32),
        ],
        scratch_shapes=[pltpu.VMEM((tm, D_MODEL), BF16)],
        compiler_params=_cparams(("parallel", "arbitrary")),
        name="proj",
    )(x, ln, wmain, wsh, wsl, bias, qn, kn, r64, e64)


def _mlstm_kernel(qk_ref, v_ref, gs_ref, hnw_ref, tril_ref,
                  hn_ref, ccol_ref, crow_ref, c_out, n_out, m_out,
                  c_scr, n_scr, m_scr, crow_scr, ccol_scr):
    c = pl.program_id(1)
    lc = M_CHUNK

    @pl.when(c == 0)
    def _():
        c_scr[...] = jnp.zeros_like(c_scr)
        n_scr[...] = jnp.zeros_like(n_scr)
        m_scr[...] = jnp.zeros_like(m_scr)
        crow_scr[...] = jnp.zeros_like(crow_scr)
        ccol_scr[...] = jnp.zeros_like(ccol_scr)

    g = gs_ref[0]
    lane = lax.broadcasted_iota(I32, (lc, LANES), 1)
    pos = c * lc + lax.broadcasted_iota(I32, (lc, LANES), 0)
    valid = pos >= PAD_FRONT
    is_ig = lane < G_LF
    gsum = jnp.where(valid & jnp.logical_not(is_ig), g, 0.0)
    bcum = _dot_exact_rhs01_lhs(tril_ref[...], _split3(gsum))
    comb = jnp.where(is_ig, jnp.where(valid, g, -jnp.inf), bcum)
    comb_t = comb.T

    row_t = lax.broadcasted_iota(I32, (lc, lc), 0)
    col_s = lax.broadcasted_iota(I32, (lc, lc), 1)
    causal = col_s <= row_t

    qk = qk_ref[0]
    vv = v_ref[0]
    for h in range(M_HEADS):
        q = qk[:, h * M_DK:(h + 1) * M_DK]
        k = qk[:, (M_HEADS + h) * M_DK:(M_HEADS + h + 1) * M_DK]
        v = vv[:, h * M_DV:(h + 1) * M_DV]
        m_prev = m_scr[h, 0:1, 0:1]
        b_col = bcum[:, G_LF + h:G_LF + h + 1]
        ig_col = comb[:, G_IG + h:G_IG + h + 1]
        b_row = comb_t[G_LF + h:G_LF + h + 1, :]
        ig_row = comb_t[G_IG + h:G_IG + h + 1, :]

        log_d = jnp.where(causal, b_col - b_row + ig_row, -jnp.inf)
        inter = b_col + m_prev
        m_row = jnp.maximum(inter, jnp.max(log_d, axis=1, keepdims=True))
        s = _dot_nt(q, k) * jnp.exp(log_d - m_row)
        w_int = jnp.exp(inter - m_row)
        cmat = c_scr[h]
        nvec = n_scr[h, 0:1, :]
        num = _dot(s.astype(BF16), v) + w_int * _dot_nt(q, cmat.astype(BF16))
        den = jnp.sum(s, axis=1, keepdims=True) + w_int * jnp.sum(q.astype(F32) * nvec, axis=1, keepdims=True)
        hh = num / jnp.maximum(jnp.abs(den), jnp.exp(-m_row))
        hn = hh * lax.rsqrt(jnp.mean(hh * hh, axis=1, keepdims=True) + EPS) * hnw_ref[:, h * M_DV:(h + 1) * M_DV]
        hn_ref[0, :, h * M_DV:(h + 1) * M_DV] = hn.astype(BF16)

        b_last = b_col[lc - 1:lc, :]
        m_new = jnp.maximum(b_last + m_prev, jnp.max(b_last - b_row + ig_row, axis=1, keepdims=True))
        decay = jnp.exp(b_last + m_prev - m_new)
        ws_col = jnp.exp(b_last - b_col + ig_col - m_new)
        kw = k.astype(F32) * ws_col
        c_scr[h] = decay * cmat + _dot_tn(v, kw.astype(BF16))
        n_scr[h] = jnp.broadcast_to(decay * nvec + jnp.sum(kw, axis=0, keepdims=True), n_scr.shape[1:])
        m_scr[h] = jnp.broadcast_to(m_new, m_scr.shape[1:])

    ccol_ref[0] = bcum + crow_scr[0:1, :]
    sub = lax.broadcasted_iota(I32, (LANES, lc), 0)
    crow_ref[0] = jnp.where(sub >= G_LF, comb_t + ccol_scr[:, 0:1], 0.0)
    crow_scr[...] = jnp.broadcast_to(crow_scr[0:1, :] + bcum[lc - 1:lc, :], crow_scr.shape)
    ccol_scr[...] = jnp.broadcast_to(
        jnp.where(sub[:, 0:1] >= G_LF, ccol_scr[:, 0:1] + comb_t[:, lc - 1:lc], 0.0), ccol_scr.shape)

    @pl.when(c == pl.num_programs(1) - 1)
    def _():
        c_out[0] = c_scr[...]
        n_out[0] = n_scr[...]
        m_out[0] = m_scr[...]


def _dot_exact_rhs01_lhs(m01, parts):
    acc = _dot(m01, parts[0])
    for p in parts[1:]:
        acc = acc + _dot(m01, p)
    return acc


def _mlstm_prompt(z3, gs3, hnw, tril):
    b, lp, _ = z3.shape
    lc = M_CHUNK
    nc = lp // lc
    return pl.pallas_call(
        _mlstm_kernel,
        grid=(b, nc),
        in_specs=[
            pl.BlockSpec((1, lc, D_MODEL), lambda i, c: (i, c, ZB_MQK)),
            pl.BlockSpec((1, lc, D_MODEL), lambda i, c: (i, c, ZB_MV)),
            pl.BlockSpec((1, lc, LANES), lambda i, c: (i, c, 0)),
            pl.BlockSpec((1, D_MODEL), lambda i, c: (0, 0)),
            pl.BlockSpec((lc, lc), lambda i, c: (0, 0)),
        ],
        out_specs=[
            pl.BlockSpec((1, lc, D_MODEL), lambda i, c: (i, c, 0)),
            pl.BlockSpec((1, lc, LANES), lambda i, c: (i, c, 0)),
            pl.BlockSpec((1, LANES, lc), lambda i, c: (i, 0, c)),
            pl.BlockSpec((1, M_HEADS, M_DV, M_DK), lambda i, c: (i, 0, 0, 0)),
            pl.BlockSpec((1, M_HEADS, 8, M_DK), lambda i, c: (i, 0, 0, 0)),
            pl.BlockSpec((1, M_HEADS, 8, LANES), lambda i, c: (i, 0, 0, 0)),
        ],
        out_shape=[
            jax.ShapeDtypeStruct((b, lp, D_MODEL), BF16),
            jax.ShapeDtypeStruct((b, lp, LANES), F32),
            jax.ShapeDtypeStruct((b, LANES, lp), F32),
            jax.ShapeDtypeStruct((b, M_HEADS, M_DV, M_DK), F32),
            jax.ShapeDtypeStruct((b, M_HEADS, 8, M_DK), F32),
            jax.ShapeDtypeStruct((b, M_HEADS, 8, LANES), F32),
        ],
        scratch_shapes=[
            pltpu.VMEM((M_HEADS, M_DV, M_DK), F32),
            pltpu.VMEM((M_HEADS, 8, M_DK), F32),
            pltpu.VMEM((M_HEADS, 8, LANES), F32),
            pltpu.VMEM((8, LANES), F32),
            pltpu.VMEM((LANES, LANES), F32),
        ],
        compiler_params=_cparams(("parallel", "arbitrary")),
        name="mlstm",
    )(z3, z3, gs3, hnw, tril)


MSTEP_SEQS = 8


def _mstep_kernel(qk_ref, v_ref, gs_ref, m_ref, n_ref, c_ref, hnw_ref,
                  hn_ref, c_out, n_out, m_out):
    sb = MSTEP_SEQS
    qk = qk_ref[...]
    vv = v_ref[...]
    g = gs_ref[...]
    row = lax.broadcasted_iota(I32, (sb, 1), 0)
    m_new_all = jnp.zeros((sb, LANES), F32)
    lane = lax.broadcasted_iota(I32, (sb, LANES), 1)
    for h in range(M_HEADS):
        q = qk[:, h * M_DK:(h + 1) * M_DK]
        k = qk[:, (M_HEADS + h) * M_DK:(M_HEADS + h + 1) * M_DK]
        v = vv[:, h * M_DV:(h + 1) * M_DV]
        ig = g[:, G_IG + h:G_IG + h + 1]
        lf = g[:, G_LF + h:G_LF + h + 1]
        m_prev = m_ref[:, h:h + 1]
        inter = lf + m_prev
        m_row = jnp.maximum(inter, ig)
        d = jnp.exp(ig - m_row)
        w_int = jnp.exp(inter - m_row)
        qf = q.astype(F32)
        kf = k.astype(F32)
        s = jnp.sum(qf * kf, axis=1, keepdims=True) * d
        nvec = n_ref[:, h * M_DK:(h + 1) * M_DK]
        den = s + w_int * jnp.sum(qf * nvec, axis=1, keepdims=True)
        kw = (kf * d).astype(BF16)
        hrows = []
        for i in range(sb):
            cmat = c_ref[i, h]
            sel = row == i
            qi = jnp.where(sel, q, jnp.zeros_like(q))
            qc = _dot_nt(qi, cmat.astype(BF16))
            hrows.append(jnp.where(sel, qc, 0.0))
            vi = jnp.where(sel, v, jnp.zeros_like(v))
            c_out[i, h] = w_int[i:i + 1, :] * cmat + _dot_tn(vi, kw)
        qc_all = functools.reduce(lambda a, b: a + b, hrows)
        num = s * v.astype(F32) + w_int * qc_all
        hh = num / jnp.maximum(jnp.abs(den), jnp.exp(-m_row))
        hn = hh * lax.rsqrt(jnp.mean(hh * hh, axis=1, keepdims=True) + EPS) * hnw_ref[:, h * M_DV:(h + 1) * M_DV]
        hn_ref[:, h * M_DV:(h + 1) * M_DV] = hn.astype(BF16)
        n_out[:, h * M_DK:(h + 1) * M_DK] = w_int * nvec + kf * d
        m_new_all = jnp.where(lane == h, m_row, m_new_all)
    m_out[...] = m_new_all


def _mlstm_step(zs, gss, m_in, n_in, c_in, hnw):
    db = zs.shape[0]
    sb = MSTEP_SEQS
    return pl.pallas_call(
        _mstep_kernel,
        grid=(db // sb,),
        in_specs=[
            pl.BlockSpec((sb, D_MODEL), lambda i: (i, ZB_MQK)),
            pl.BlockSpec((sb, D_MODEL), lambda i: (i, ZB_MV)),
            pl.BlockSpec((sb, LANES), lambda i: (i, 0)),
            pl.BlockSpec((sb, LANES), lambda i: (i, 0)),
            pl.BlockSpec((sb, M_HEADS * M_DK), lambda i: (i, 0)),
            pl.BlockSpec((sb, M_HEADS, M_DV, M_DK), lambda i: (i, 0, 0, 0)),
            pl.BlockSpec((1, D_MODEL), lambda i: (0, 0)),
        ],
        out_specs=[
            pl.BlockSpec((sb, D_MODEL), lambda i: (i, 0)),
            pl.BlockSpec((sb, M_HEADS, M_DV, M_DK), lambda i: (i, 0, 0, 0)),
            pl.BlockSpec((sb, M_HEADS * M_DK), lambda i: (i, 0)),
            pl.BlockSpec((sb, LANES), lambda i: (i, 0)),
        ],
        out_shape=[
            jax.ShapeDtypeStruct((db, D_MODEL), BF16),
            jax.ShapeDtypeStruct((db, M_HEADS, M_DV, M_DK), F32),
            jax.ShapeDtypeStruct((db, M_HEADS * M_DK), F32),
            jax.ShapeDtypeStruct((db, LANES), F32),
        ],
        compiler_params=_cparams(("parallel",)),
        name="mstep",
    )(zs, zs, gss, m_in, n_in, c_in, hnw)


def _fox_kernel(qi_tab, ki_tab, q_ref, k_ref, v_ref, cq_ref, ck_ref, o_ref, acc_scr, m_scr, l_scr):
    step = pl.program_id(1)
    qi = qi_tab[step]
    ki = ki_tab[step]
    tq = tk = ATT_BLOCK

    @pl.when(ki == 0)
    def _():
        acc_scr[...] = jnp.zeros_like(acc_scr)
        m_scr[...] = jnp.full_like(m_scr, -jnp.inf)
        l_scr[...] = jnp.zeros_like(l_scr)

    row = qi * tq + lax.broadcasted_iota(I32, (tq, tk), 0)
    col = ki * tk + lax.broadcasted_iota(I32, (tq, tk), 1)
    visible = (col <= row) & ((col >= PAD_FRONT) | (row < PAD_FRONT))
    lane = lax.broadcasted_iota(I32, (tq, LANES), 1)
    first = lane < F_DH

    cq = cq_ref[0]
    ck = ck_ref[0]
    for p in range(F_HEADS // 2):
        sl = slice(p * LANES, (p + 1) * LANES)
        q2 = q_ref[0, :, sl]
        k2 = k_ref[0, :, sl]
        v2 = v_ref[0, :, sl]
        alphas, pvs = [], []
        for hh in range(2):
            h = 2 * p + hh
            qm = jnp.where(first if hh == 0 else jnp.logical_not(first), q2, jnp.zeros_like(q2))
            s = _dot_nt(qm, k2)
            s = s + cq[:, G_FF + h:G_FF + h + 1] - ck[G_FF + h:G_FF + h + 1, :]
            s = jnp.where(visible, s, -jnp.inf)
            m_prev = m_scr[h]
            m_next = jnp.maximum(m_prev, jnp.max(s, axis=1, keepdims=True))
            alpha = jnp.exp(m_prev - m_next)
            pr = jnp.exp(s - m_next[:, 0:1])
            l_scr[h] = alpha * l_scr[h] + jnp.sum(pr, axis=1, keepdims=True)
            m_scr[h] = m_next
            alphas.append(alpha)
            pvs.append(_dot(pr.astype(BF16), v2))
        acc_scr[:, sl] = jnp.where(first, alphas[0], alphas[1]) * acc_scr[:, sl] + jnp.where(first, pvs[0], pvs[1])

    @pl.when(ki == qi)
    def _():
        for p in range(F_HEADS // 2):
            sl = slice(p * LANES, (p + 1) * LANES)
            l2 = jnp.where(first, l_scr[2 * p], l_scr[2 * p + 1])
            o_ref[0, :, sl] = (acc_scr[:, sl] / l2).astype(BF16)


def _fox_prompt(z3, ccol, crow):
    b, lp, _ = z3.shape
    blk = ATT_BLOCK
    nb = lp // blk
    qi_tab = np.concatenate([np.full((i + 1,), i, np.int32) for i in range(nb)])
    ki_tab = np.concatenate([np.arange(i + 1, dtype=np.int32) for i in range(nb)])
    grid_spec = pltpu.PrefetchScalarGridSpec(
        num_scalar_prefetch=2,
        grid=(b, len(qi_tab)),
        in_specs=[
            pl.BlockSpec((1, blk, D_MODEL), lambda i, s, qt, kt: (i, qt[s], ZB_FQ)),
            pl.BlockSpec((1, blk, D_MODEL), lambda i, s, qt, kt: (i, kt[s], ZB_FK)),
            pl.BlockSpec((1, blk, D_MODEL), lambda i, s, qt, kt: (i, kt[s], ZB_FV)),
            pl.BlockSpec((1, blk, LANES), lambda i, s, qt, kt: (i, qt[s], 0)),
            pl.BlockSpec((1, LANES, blk), lambda i, s, qt, kt: (i, 0, kt[s])),
        ],
        out_specs=pl.BlockSpec((1, blk, D_MODEL), lambda i, s, qt, kt: (i, qt[s], 0)),
        scratch_shapes=[
            pltpu.VMEM((blk, D_MODEL), F32),
            pltpu.VMEM((F_HEADS, blk, LANES), F32),
            pltpu.VMEM((F_HEADS, blk, LANES), F32),
        ],
    )
    return pl.pallas_call(
        _fox_kernel,
        grid_spec=grid_spec,
        out_shape=jax.ShapeDtypeStruct((b, lp, D_MODEL), BF16),
        compiler_params=_cparams(("parallel", "arbitrary")),
        name="fox",
    )(jnp.asarray(qi_tab), jnp.asarray(ki_tab), z3, z3, z3, ccol, crow)


def _expand_heads(x, e_ref):
    return _dot_exact_rhs01(_split3(x), e_ref[...])


def _dec_kernel(pt_ref, q_ref, kn_ref, vn_ref, gs_ref, kp_ref, vp_ref, lf_ref, tril_ref, e_ref, et_ref,
                o_ref, qrows_scr, acc_scr, m_scr, l_scr, c_scr):
    p = pl.program_id(1)
    pg = kp_ref.shape[1]

    @pl.when(p == 0)
    def _():
        qrow = q_ref[0]
        qrows_scr[...] = (qrow * et_ref[...]).astype(BF16)
        acc_scr[...] = jnp.zeros_like(acc_scr)
        m_scr[...] = jnp.full_like(m_scr, -jnp.inf)
        l_scr[...] = jnp.zeros_like(l_scr)
        c_scr[...] = jnp.zeros_like(c_scr)

    kb = kp_ref[0].astype(BF16)
    s = _dot_nt(kb, qrows_scr[...])
    lf = lf_ref[0]
    lf128 = jnp.concatenate([lf, jnp.zeros((pg, LANES - F_HEADS), F32)], axis=1)
    cum = _dot_exact_rhs01_lhs(tril_ref[...], _split3(lf128)) + c_scr[0:1, :]
    c_scr[...] = jnp.broadcast_to(cum[pg - 1:pg, :], c_scr.shape)
    u = s - cum
    m_prev = m_scr[0:1, :]
    m_next = jnp.maximum(m_prev, jnp.max(u, axis=0, keepdims=True))
    alpha = jnp.exp(m_prev - m_next)
    pr = jnp.exp(u - m_next)
    l_scr[...] = jnp.broadcast_to(alpha * l_scr[0:1, :] + jnp.sum(pr, axis=0, keepdims=True), l_scr.shape)
    m_scr[...] = jnp.broadcast_to(m_next, m_scr.shape)
    pexp = _dot(pr.astype(BF16), e_ref[...])
    contrib = (pexp * vp_ref[0]).reshape(pg // 8, 8, D_MODEL).sum(axis=0)
    a8 = _expand_heads(jnp.broadcast_to(alpha, (8, LANES)), e_ref)
    acc_scr[...] = a8 * acc_scr[...] + contrib

    @pl.when(p == pl.num_programs(1) - 1)
    def _():
        g = gs_ref[0]
        lane = lax.broadcasted_iota(I32, (1, LANES), 1)
        lff = jnp.zeros((1, LANES), F32)
        for h in range(F_HEADS):
            lff = jnp.where(lane == h, g[:, G_FF + h:G_FF + h + 1], lff)
        knew = jnp.broadcast_to(kn_ref[0], (8, D_MODEL)).astype(BF16)
        s_new = _dot_nt(knew, qrows_scr[...])[0:1, :]
        u_new = s_new - (c_scr[0:1, :] + lff)
        m_prev = m_scr[0:1, :]
        m_fin = jnp.maximum(m_prev, u_new)
        alpha = jnp.exp(m_prev - m_fin)
        pn = jnp.exp(u_new - m_fin)
        l_fin = alpha * l_scr[0:1, :] + pn
        acc = jnp.sum(acc_scr[...], axis=0, keepdims=True)
        a1 = _expand_heads(jnp.broadcast_to(alpha / l_fin, (8, LANES)), e_ref)[0:1, :]
        p1 = _expand_heads(jnp.broadcast_to(pn / l_fin, (8, LANES)), e_ref)[0:1, :]
        o_ref[0] = a1 * acc + p1 * vn_ref[0]


def _fox_decode(page_table, q3, kn3, vn3, gs3, cache_k, cache_v, cache_lf, tril, e64, e64t):
    db, npg = page_table.shape
    pg = cache_k.shape[1]
    grid_spec = pltpu.PrefetchScalarGridSpec(
        num_scalar_prefetch=1,
        grid=(db, npg),
        in_specs=[
            pl.BlockSpec((1, 1, D_MODEL), lambda s, p, pt: (s, 0, 0)),
            pl.BlockSpec((1, 1, D_MODEL), lambda s, p, pt: (s, 0, 0)),
            pl.BlockSpec((1, 1, D_MODEL), lambda s, p, pt: (s, 0, 0)),
            pl.BlockSpec((1, 1, LANES), lambda s, p, pt: (s, 0, 0)),
            pl.BlockSpec((1, pg, D_MODEL), lambda s, p, pt: (pt[s, p], 0, 0)),
            pl.BlockSpec((1, pg, D_MODEL), lambda s, p, pt: (pt[s, p], 0, 0)),
            pl.BlockSpec((1, pg, F_HEADS), lambda s, p, pt: (pt[s, p], 0, 0)),
            pl.BlockSpec((pg, pg), lambda s, p, pt: (0, 0)),
            pl.BlockSpec((LANES, D_MODEL), lambda s, p, pt: (0, 0)),
            pl.BlockSpec((LANES, D_MODEL), lambda s, p, pt: (0, 0)),
        ],
        out_specs=pl.BlockSpec((1, 1, D_MODEL), lambda s, p, pt: (s, 0, 0)),
        scratch_shapes=[
            pltpu.VMEM((LANES, D_MODEL), BF16),
            pltpu.VMEM((8, D_MODEL), F32),
            pltpu.VMEM((8, LANES), F32),
            pltpu.VMEM((8, LANES), F32),
            pltpu.VMEM((8, LANES), F32),
        ],
    )
    return pl.pallas_call(
        _dec_kernel,
        grid_spec=grid_spec,
        out_shape=jax.ShapeDtypeStruct((db, 1, D_MODEL), F32),
        compiler_params=_cparams(("parallel", "arbitrary")),
        name="dec",
    )(page_table, q3, kn3, vn3, gs3, cache_k, cache_v, cache_lf, tril, e64, e64t)


def _merge_kernel(x_ref, hn_ref, so_ref, hf_ref, ga_ref, gb_ref, wbm_ref, wbf_ref, wo_ref, ln2_ref,
                  wrh_ref, wrl_ref, br_ref, x1_ref, t_ref, lg_ref):
    hm = hn_ref[...] * so_ref[...]
    ya = _dot(hm, wbm_ref[...])
    yb = _dot(hf_ref[...], wbf_ref[...])
    u = ga_ref[...].astype(F32) * ya + gb_ref[...].astype(F32) * yb
    x1 = x_ref[...] + _dot(u.astype(BF16), wo_ref[...])
    x1_ref[...] = x1
    t = x1 * lax.rsqrt(jnp.mean(x1 * x1, axis=-1, keepdims=True) + EPS) * ln2_ref[...]
    t_ref[...] = t
    th, tl = _split2(t)
    wrh = wrh_ref[...]
    lg_ref[...] = _dot(th, wrh) + _dot(tl, wrh) + _dot(th, wrl_ref[...]) + br_ref[...]


def _merge(x, hn, z, hf, wbm, wbf, wo, ln2, wrh, wrl, br, tm):
    t = x.shape[0]
    row = lambda i: (i, 0)
    fix = lambda i: (0, 0)
    full = pl.BlockSpec((D_MODEL, D_MODEL), fix)
    return pl.pallas_call(
        _merge_kernel,
        grid=(t // tm,),
        in_specs=[
            pl.BlockSpec((tm, D_MODEL), row),
            pl.BlockSpec((tm, D_MODEL), row),
            pl.BlockSpec((tm, D_MODEL), lambda i: (i, ZB_MO)),
            pl.BlockSpec((tm, D_MODEL), row),
            pl.BlockSpec((tm, D_MODEL), lambda i: (i, ZB_GA)),
            pl.BlockSpec((tm, D_MODEL), lambda i: (i, ZB_GB)),
            full, full, full,
            pl.BlockSpec((1, D_MODEL), fix),
            pl.BlockSpec((D_MODEL, LANES), fix),
            pl.BlockSpec((D_MODEL, LANES), fix),
            pl.BlockSpec((1, LANES), fix),
        ],
        out_specs=[
            pl.BlockSpec((tm, D_MODEL), row),
            pl.BlockSpec((tm, D_MODEL), row),
            pl.BlockSpec((tm, LANES), row),
        ],
        out_shape=[
            jax.ShapeDtypeStruct((t, D_MODEL), F32),
            jax.ShapeDtypeStruct((t, D_MODEL), F32),
            jax.ShapeDtypeStruct((t, LANES), F32),
        ],
        compiler_params=_cparams(("parallel",)),
        name="merge",
    )(x, hn, z, hf, z, z, wbm, wbf, wo, ln2, wrh, wrl, br)


def _route_kernel(lg_ref, tril_ref, ids_ref, wts_ref, rank_ref, cnt_ref, carry_scr):
    i = pl.program_id(0)

    @pl.when(i == 0)
    def _():
        carry_scr[...] = jnp.zeros_like(carry_scr)

    tm = lg_ref.shape[0]
    lane = lax.broadcasted_iota(I32, (tm, LANES), 1)
    lanef = lane.astype(F32)
    lg = jnp.where(lane < N_EXPERTS, lg_ref[...], -jnp.inf)
    vals, idxs, hots = [], [], []
    for _ in range(TOP_K):
        mx = jnp.max(lg, axis=1, keepdims=True)
        idx = jnp.min(jnp.where(lg == mx, lanef, float(LANES)), axis=1, keepdims=True)
        hot = lanef == idx
        lg = jnp.where(hot, -jnp.inf, lg)
        vals.append(mx)
        idxs.append(idx)
        hots.append(hot)
    es = [jnp.exp(v - vals[0]) for v in vals]
    tot = functools.reduce(lambda a, b: a + b, es)
    sel = functools.reduce(lambda a, b: a | b, hots)
    a01 = jnp.where(sel, 1.0, 0.0)
    before = _dot(tril_ref[...], a01.astype(BF16)) + carry_scr[0:1, :]
    carry_scr[...] = jnp.broadcast_to(carry_scr[0:1, :] + jnp.sum(a01, axis=0, keepdims=True), carry_scr.shape)
    ids = jnp.zeros((tm, LANES), I32)
    wts = jnp.zeros((tm, LANES), F32)
    rank = jnp.zeros((tm, LANES), I32)
    for kk in range(TOP_K):
        r = jnp.sum(jnp.where(hots[kk], before, 0.0), axis=1, keepdims=True)
        ids = jnp.where(lane == kk, idxs[kk].astype(I32), ids)
        wts = jnp.where(lane == kk, es[kk] / tot, wts)
        rank = jnp.where(lane == kk, r.astype(I32), rank)
    ids_ref[...] = ids
    wts_ref[...] = wts
    rank_ref[...] = rank
    cnt_ref[...] = carry_scr[...].astype(I32)


def _route(logits, tril_strict, tm):
    t = logits.shape[0]
    row = lambda i: (i, 0)
    return pl.pallas_call(
        _route_kernel,
        grid=(t // tm,),
        in_specs=[pl.BlockSpec((tm, LANES), row), pl.BlockSpec((tm, tm), lambda i: (0, 0))],
        out_specs=[pl.BlockSpec((tm, LANES), row)] * 3 + [pl.BlockSpec((8, LANES), lambda i: (0, 0))],
        out_shape=[
            jax.ShapeDtypeStruct((t, LANES), I32),
            jax.ShapeDtypeStruct((t, LANES), F32),
            jax.ShapeDtypeStruct((t, LANES), I32),
            jax.ShapeDtypeStruct((8, LANES), I32),
        ],
        scratch_shapes=[pltpu.VMEM((8, LANES), F32)],
        compiler_params=_cparams(("arbitrary",)),
        name="route",
    )(logits, tril_strict)


def _gather_kernel(idx_ref, src_ref, out_ref, sem):
    n = out_ref.shape[0]

    def issue(r, carry):
        pltpu.make_async_copy(src_ref.at[pl.ds(idx_ref[0, 0, r], 1), :], out_ref.at[pl.ds(r, 1), :], sem).start()
        return carry

    lax.fori_loop(0, n, issue, 0, unroll=8)
    pltpu.make_async_copy(src_ref.at[pl.ds(0, n), :], out_ref, sem).wait()


def _gather_rows(src, idx, tile):
    n = idx.shape[0]
    w = src.shape[1]
    return pl.pallas_call(
        _gather_kernel,
        grid=(n // tile,),
        in_specs=[
            pl.BlockSpec((1, 1, tile), lambda i: (i, 0, 0), memory_space=pltpu.SMEM),
            pl.BlockSpec(memory_space=pl.ANY),
        ],
        out_specs=pl.BlockSpec((tile, w), lambda i: (i, 0)),
        out_shape=jax.ShapeDtypeStruct((n, w), src.dtype),
        scratch_shapes=[pltpu.SemaphoreType.DMA],
        compiler_params=_cparams(("arbitrary",)),
        name="gather",
    )(idx.reshape(n // tile, 1, tile), src)


def _experts_kernel(te_ref, nu_ref, xs_ref, w1_ref, b1_ref, w2_ref, b2_ref, ys_ref, w1b_scr, w2b_scr):
    j = pl.program_id(0)
    prev = te_ref[jnp.maximum(j - 1, 0)]

    @pl.when((j == 0) | (te_ref[j] != prev))
    def _():
        w1b_scr[...] = w1_ref[0].astype(BF16)
        w2b_scr[...] = w2_ref[0].astype(BF16)

    @pl.when(j < nu_ref[0])
    def _():
        hcat = _dot(xs_ref[...].astype(BF16), w1b_scr[...]) + b1_ref[0]
        g = jnp.minimum(hcat[:, :D_FF], SWIGLU_LIMIT)
        u = jnp.clip(hcat[:, D_FF:], -SWIGLU_LIMIT, SWIGLU_LIMIT)
        a = g * jax.nn.sigmoid(SWIGLU_ALPHA * g) * (u + 1.0)
        ys_ref[...] = _dot(a.astype(BF16), w2b_scr[...]) + b2_ref[0]

    @pl.when(j >= nu_ref[0])
    def _():
        ys_ref[...] = jnp.zeros_like(ys_ref)


def _experts(tile_expert, n_used, xs, w1, b1, w2, b2):
    r = xs.shape[0]
    tm = EXP_TILE
    grid_spec = pltpu.PrefetchScalarGridSpec(
        num_scalar_prefetch=2,
        grid=(r // tm,),
        in_specs=[
            pl.BlockSpec((tm, D_MODEL), lambda j, te, nu: (j, 0)),
            pl.BlockSpec((1, D_MODEL, 2 * D_FF), lambda j, te, nu: (te[j], 0, 0)),
            pl.BlockSpec((1, 1, 2 * D_FF), lambda j, te, nu: (te[j], 0, 0)),
            pl.BlockSpec((1, D_FF, D_MODEL), lambda j, te, nu: (te[j], 0, 0)),
            pl.BlockSpec((1, 1, D_MODEL), lambda j, te, nu: (te[j], 0, 0)),
        ],
        out_specs=pl.BlockSpec((tm, D_MODEL), lambda j, te, nu: (j, 0)),
        scratch_shapes=[
            pltpu.VMEM((D_MODEL, 2 * D_FF), BF16),
            pltpu.VMEM((D_FF, D_MODEL), BF16),
        ],
    )
    return pl.pallas_call(
        _experts_kernel,
        grid_spec=grid_spec,
        out_shape=jax.ShapeDtypeStruct((r, D_MODEL), F32),
        compiler_params=_cparams(("arbitrary",)),
        name="experts",
    )(tile_expert, n_used, xs, w1, b1.reshape(N_EXPERTS, 1, 2 * D_FF), w2, b2.reshape(N_EXPERTS, 1, D_MODEL))


def _combine_kernel(pos_ref, ys_ref, x1_ref, wts_ref, lnf_ref, y_ref, rows_scr, sem):
    tc = x1_ref.shape[0]

    def issue(r, carry):
        for kk in range(TOP_K):
            pltpu.make_async_copy(ys_ref.at[pl.ds(pos_ref[0, kk, r], 1), :],
                                  rows_scr.at[kk, pl.ds(r, 1), :], sem).start()
        return carry

    lax.fori_loop(0, tc, issue, 0, unroll=2)
    for kk in range(TOP_K):
        pltpu.make_async_copy(ys_ref.at[pl.ds(0, tc), :], rows_scr.at[kk], sem).wait()
    w = wts_ref[...]
    acc = x1_ref[...]
    for kk in range(TOP_K):
        acc = acc + w[:, kk:kk + 1] * rows_scr[kk]
    y_ref[...] = acc * lax.rsqrt(jnp.mean(acc * acc, axis=-1, keepdims=True) + EPS) * lnf_ref[...]


def _combine(pos, ys, x1, wts, lnf):
    t = x1.shape[0]
    tc = COMB_TILE
    row = lambda i: (i, 0)
    pos3 = pos.reshape(t // tc, tc, TOP_K).transpose(0, 2, 1)
    return pl.pallas_call(
        _combine_kernel,
        grid=(t // tc,),
        in_specs=[
            pl.BlockSpec((1, TOP_K, tc), lambda i: (i, 0, 0), memory_space=pltpu.SMEM),
            pl.BlockSpec(memory_space=pl.ANY),
            pl.BlockSpec((tc, D_MODEL), row),
            pl.BlockSpec((tc, LANES), row),
            pl.BlockSpec((1, D_MODEL), lambda i: (0, 0)),
        ],
        out_specs=pl.BlockSpec((tc, D_MODEL), row),
        out_shape=jax.ShapeDtypeStruct((t, D_MODEL), F32),
        scratch_shapes=[pltpu.VMEM((TOP_K, tc, D_MODEL), F32), pltpu.SemaphoreType.DMA],
        compiler_params=_cparams(("arbitrary",)),
        name="combine",
    )(pos3, ys, x1, wts, lnf)


def _consts():
    r64 = np.zeros((D_MODEL, LANES), np.float32)
    e64 = np.zeros((LANES, D_MODEL), np.float32)
    for h in range(F_HEADS):
        r64[h * F_DH:(h + 1) * F_DH, h] = 1.0 / F_DH
        e64[h, h * F_DH:(h + 1) * F_DH] = 1.0
    return jnp.asarray(r64, BF16), jnp.asarray(e64, BF16), jnp.asarray(e64, F32)


def _tril(n, strict=False):
    return jnp.asarray(np.tril(np.ones((n, n), np.float32), -1 if strict else 0), BF16)


def _round_up(a, b):
    return (a + b - 1) // b * b


def kernel(x_prompt, x_sample, cache_k, cache_v, cache_logf, state_C, state_n, state_m, page_table, meta_tokens, ln1, w_in, mlstm_b_i, mlstm_b_f, mlstm_head_norm, fox_b_f, fox_q_norm, fox_k_norm, w_branch_mlstm, w_branch_fox, w_out, ln2, w_router, b_router, w_exp_in, b_exp_in, w_exp_out, b_exp_out, ln_final):
    depth = w_in.shape[0]
    b, seq, _ = x_prompt.shape
    db, ds, _ = x_sample.shape
    assert ds == 1 and seq % ATT_BLOCK == 0 and db % MSTEP_SEQS == 0
    l_true = seq + N_META
    lp = seq + ATT_BLOCK
    tp = b * lp
    n_phys, pg = cache_k.shape[1], cache_k.shape[2]
    r64, e64, e64f = _consts()

    xp = jnp.concatenate([jnp.zeros((b, PAD_FRONT, D_MODEL), F32),
                          jnp.broadcast_to(meta_tokens[None].astype(F32), (b, N_META, D_MODEL)),
                          x_prompt], axis=1).reshape(tp, D_MODEL)
    xs = x_sample.reshape(db, D_MODEL)

    outs = {k: [] for k in ("kp", "vp", "lfp", "ks", "vs", "lfs", "cp", "np", "mp", "cs", "ns", "ms")}
    offs = np.cumsum((0,) + (M_HEADS * M_DK, M_HEADS * M_DK, M_HEADS * M_DV, M_HEADS * M_DV, M_HEADS, M_HEADS,
                             F_HEADS * F_DH, F_HEADS * F_DH, F_HEADS * F_DH, F_HEADS, 2 * D_MODEL))
    seg = lambda w, i: w[:, offs[i]:offs[i + 1]]
    for l in range(depth):
        w = w_in[l]
        wmain = jnp.concatenate([seg(w, 0), seg(w, 1), seg(w, 2), seg(w, 3), seg(w, 6), seg(w, 7), seg(w, 8),
                                 seg(w, 10)], axis=1).astype(BF16)
        wsm = jnp.concatenate([seg(w, 4), seg(w, 5), seg(w, 9),
                               jnp.zeros((D_MODEL, LANES - G_END), F32)], axis=1)
        wsh = wsm.astype(BF16)
        wsl = (wsm - wsh.astype(F32)).astype(BF16)
        bias = jnp.concatenate([mlstm_b_i[l], mlstm_b_f[l], fox_b_f[l], jnp.zeros((LANES - G_END,), F32)])[None]
        qn = jnp.tile(fox_q_norm[l], F_HEADS)[None]
        kn = jnp.tile(fox_k_norm[l], F_HEADS)[None]
        hnw = mlstm_head_norm[l][None]
        ln1l = ln1[l][None]
        wbm = w_branch_mlstm[l].astype(BF16)
        wbf = w_branch_fox[l].astype(BF16)
        wo = w_out[l].astype(BF16)
        wr = jnp.concatenate([w_router[l], jnp.zeros((D_MODEL, LANES - N_EXPERTS), F32)], axis=1)
        wrh = wr.astype(BF16)
        wrl = (wr - wrh.astype(F32)).astype(BF16)
        br = jnp.concatenate([b_router[l], jnp.zeros((LANES - N_EXPERTS,), F32)])[None]

        zp, kp32, vp32, gsp = _proj(xp, ln1l, wmain, wsh, wsl, bias, qn, kn, r64, e64, ROW_TILE)
        zp3 = zp.reshape(b, lp, N_ZB * D_MODEL)
        hnp, ccol, crow, c_p, n_p, m_p = _mlstm_prompt(zp3, gsp.reshape(b, lp, LANES), hnw, _tril(M_CHUNK))
        hfp = _fox_prompt(zp3, ccol, crow)
        x1p, tpn, lgp = _merge(xp, hnp.reshape(tp, D_MODEL), zp, hfp.reshape(tp, D_MODEL), wbm, wbf, wo,
                               ln2[l][None], wrh, wrl, br, ROW_TILE)
        outs["kp"].append(kp32.reshape(b, lp, F_HEADS, F_DH)[:, PAD_FRONT:])
        outs["vp"].append(vp32.reshape(b, lp, F_HEADS, F_DH)[:, PAD_FRONT:])
        outs["lfp"].append(gsp.reshape(b, lp, LANES)[:, PAD_FRONT:, G_FF:G_END])
        outs["cp"].append(c_p)
        outs["np"].append(n_p[:, :, 0, :])
        outs["mp"].append(m_p[:, :, 0, 0])

        zs, ks32, vs32, gss = _proj(xs, ln1l, wmain, wsh, wsl, bias, qn, kn, r64, e64, db)
        m_in = jnp.concatenate([state_m[l], jnp.zeros((db, LANES - M_HEADS), F32)], axis=1)
        hns, c_s, n_s, m_s = _mlstm_step(zs, gss, m_in, state_n[l].reshape(db, M_HEADS * M_DK), state_C[l], hnw)
        n_s = n_s.reshape(db, M_HEADS, M_DK)
        q3 = zs[:, ZB_FQ * D_MODEL:(ZB_FQ + 1) * D_MODEL].astype(F32).reshape(db, 1, D_MODEL)
        kn3 = zs[:, ZB_FK * D_MODEL:(ZB_FK + 1) * D_MODEL].astype(F32).reshape(db, 1, D_MODEL)
        hfs = _fox_decode(page_table, q3, kn3, vs32.reshape(db, 1, D_MODEL), gss.reshape(db, 1, LANES),
                          cache_k[l].reshape(n_phys, pg, D_MODEL), cache_v[l].reshape(n_phys, pg, D_MODEL),
                          cache_logf[l], _tril(pg), e64, e64f)
        x1s, tsn, lgs = _merge(xs, hns, zs, hfs.reshape(db, D_MODEL).astype(BF16), wbm, wbf, wo,
                               ln2[l][None], wrh, wrl, br, db)
        outs["ks"].append(ks32.reshape(db, 1, F_HEADS, F_DH))
        outs["vs"].append(vs32.reshape(db, 1, F_HEADS, F_DH))
        outs["lfs"].append(gss[:, G_FF:G_END].reshape(db, 1, F_HEADS))
        outs["cs"].append(c_s)
        outs["ns"].append(n_s)
        outs["ms"].append(m_s[:, :M_HEADS])

        tt = tp + db
        tmoe = _round_up(tt, ROW_TILE)
        zpad = jnp.zeros((tmoe - tt, D_MODEL), F32)
        t_all = jnp.concatenate([tpn, tsn, zpad], axis=0)
        x1_all = jnp.concatenate([x1p, x1s, zpad], axis=0)
        lg_all = jnp.concatenate([lgp, lgs, jnp.zeros((tmoe - tt, LANES), F32)], axis=0)
        ids, wts, rank, cnt = _route(lg_all, _tril(ROW_TILE, strict=True), ROW_TILE)
        counts = cnt[0, :N_EXPERTS]
        pcounts = (counts + EXP_TILE - 1) // EXP_TILE * EXP_TILE
        ends = jnp.cumsum(pcounts)
        starts = ends - pcounts
        ids4 = ids[:, :TOP_K]
        pos = starts[ids4] + rank[:, :TOP_K]
        r_pad = _round_up(TOP_K * tmoe + N_EXPERTS * (EXP_TILE - 1), EXP_TILE)
        src = jnp.zeros((r_pad,), I32).at[pos.reshape(-1)].set(
            jnp.repeat(jnp.arange(tmoe, dtype=I32), TOP_K))
        n_tiles = r_pad // EXP_TILE
        tile_expert = jnp.minimum(
            jnp.searchsorted(ends, jnp.arange(n_tiles, dtype=I32) * EXP_TILE, side="right"),
            N_EXPERTS - 1).astype(I32)
        n_used = (ends[-1] // EXP_TILE).astype(I32)[None]
        xsrt = _gather_rows(t_all, src, GATHER_TILE)
        ysrt = _experts(tile_expert, n_used, xsrt, w_exp_in[l], b_exp_in[l], w_exp_out[l], b_exp_out[l])
        lnf = ln_final[None] if l == depth - 1 else jnp.ones((1, D_MODEL), F32)
        y_all = _combine(pos, ysrt, x1_all, wts, lnf)
        if l < depth - 1:
            raise NotImplementedError("deeper stacks need the un-normalised residual stream as well")

    y_prompt = y_all[:tp].reshape(b, lp, D_MODEL)[:, ATT_BLOCK:]
    y_sample = y_all[tp:tp + db].reshape(db, 1, D_MODEL)
    st = lambda k: jnp.stack(outs[k])
    return (y_prompt, y_sample, st("kp"), st("vp"), st("lfp"), st("ks"), st("vs"), st("lfs"),
            st("cp"), st("np"), st("mp"), st("cs"), st("ns"), st("ms"))
```

```python
import functools

import numpy as np
import jax
import jax.numpy as jnp
from jax import lax
from jax.experimental import pallas as pl
from jax.experimental.pallas import tpu as pltpu

F32 = jnp.float32
BF16 = jnp.bfloat16
I32 = jnp.int32

D_MODEL = 1024
N_META = 16
M_HEADS, M_DK, M_DV = 4, 128, 256
F_HEADS, F_DH = 16, 64
N_EXPERTS, TOP_K, D_FF = 32, 4, 1024
GATE_CAP = 15.0
SWIGLU_LIMIT = 7.0
SWIGLU_ALPHA = 1.702
EPS = 1e-6
ATTN_SCALE = F_DH ** -0.5

LANES = 128
ATT_BLOCK = 256
PAD_FRONT = ATT_BLOCK - N_META
M_CHUNK = 128
ROW_TILE = 512
EXP_TILE = 256
GATHER_TILE = 256
COMB_TILE = 128
VMEM_LIMIT = 56 * 1024 * 1024

G_IG = 0
G_LF = M_HEADS
G_FF = 2 * M_HEADS
G_END = G_FF + F_HEADS

ZB_MQK, ZB_MV, ZB_MO, ZB_FQ, ZB_FK, ZB_FV, ZB_GA, ZB_GB = range(8)
N_ZB = 8


def _dot(a, b):
    return jnp.dot(a, b, preferred_element_type=F32)


def _dot_nt(a, b):
    return lax.dot_general(a, b, (((1,), (1,)), ((), ())), preferred_element_type=F32)


def _dot_tn(a, b):
    return lax.dot_general(a, b, (((0,), (0,)), ((), ())), preferred_element_type=F32)


def _split2(x):
    hi = x.astype(BF16)
    lo = (x - hi.astype(F32)).astype(BF16)
    return hi, lo


def _split3(x):
    a = x.astype(BF16)
    r = x - a.astype(F32)
    b = r.astype(BF16)
    c = (r - b.astype(F32)).astype(BF16)
    return a, b, c


def _dot_exact_rhs01(parts, m01):
    acc = _dot(parts[0], m01)
    for p in parts[1:]:
        acc = acc + _dot(p, m01)
    return acc


def _log_sigmoid(x):
    return jnp.minimum(x, 0.0) - jnp.log1p(jnp.exp(-jnp.abs(x)))


def _cparams(sem):
    return pltpu.CompilerParams(dimension_semantics=sem, vmem_limit_bytes=VMEM_LIMIT)


def _head_norm(y, g, r_ref, e_ref):
    ms = _dot((y * y).astype(BF16), r_ref[...])
    rs = lax.rsqrt(ms + EPS)
    rsx = _dot_exact_rhs01(_split2(rs), e_ref[...])
    return y * rsx * g


def _proj_kernel(x_ref, ln_ref, w_ref, wsh_ref, wsl_ref, bias_ref, qn_ref, kn_ref, r_ref, e_ref,
                 z_ref, k32_ref, v32_ref, gs_ref, h_scr):
    j = pl.program_id(1)

    @pl.when(j == 0)
    def _():
        x = x_ref[...]
        ms = jnp.mean(x * x, axis=-1, keepdims=True)
        h = x * lax.rsqrt(ms + EPS) * ln_ref[...]
        hh, hl = _split2(h)
        h_scr[...] = hh
        wsh = wsh_ref[...]
        g = _dot(hh, wsh) + _dot(hl, wsh) + _dot(hh, wsl_ref[...]) + bias_ref[...]
        lane = lax.broadcasted_iota(I32, g.shape, 1)
        cap = GATE_CAP * jnp.tanh(g / GATE_CAP)
        ls = _log_sigmoid(jnp.where(lane < G_FF, cap, g))
        gs_ref[...] = jnp.where(lane < G_LF, cap, jnp.where(lane < G_END, ls, 0.0))

    y = _dot(h_scr[...], w_ref[...])

    @pl.when(j == ZB_MQK)
    def _():
        col = lax.broadcasted_iota(I32, y.shape, 1)
        z_ref[...] = jnp.where(col < M_HEADS * M_DK, y * (M_DK ** -0.5), y).astype(BF16)

    @pl.when(j == ZB_MV)
    def _():
        z_ref[...] = y.astype(BF16)

    @pl.when((j == ZB_MO) | (j == ZB_GA) | (j == ZB_GB))
    def _():
        z_ref[...] = jax.nn.sigmoid(y).astype(BF16)

    @pl.when(j == ZB_FQ)
    def _():
        z_ref[...] = (_head_norm(y, qn_ref[...], r_ref, e_ref) * ATTN_SCALE).astype(BF16)

    @pl.when(j == ZB_FK)
    def _():
        kn = _head_norm(y, kn_ref[...], r_ref, e_ref)
        k32_ref[...] = kn
        z_ref[...] = kn.astype(BF16)

    @pl.when(j == ZB_FV)
    def _():
        v32_ref[...] = y
        z_ref[...] = y.astype(BF16)


def _proj(x, ln, wmain, wsh, wsl, bias, qn, kn, r64, e64, tm):
    t = x.shape[0]
    row = lambda i, j: (i, 0)
    fix = lambda i, j: (0, 0)
    return pl.pallas_call(
        _proj_kernel,
        grid=(t // tm, N_ZB),
        in_specs=[
            pl.BlockSpec((tm, D_MODEL), row),
            pl.BlockSpec((1, D_MODEL), fix),
            pl.BlockSpec((D_MODEL, D_MODEL), lambda i, j: (0, j)),
            pl.BlockSpec((D_MODEL, LANES), fix),
            pl.BlockSpec((D_MODEL, LANES), fix),
            pl.BlockSpec((1, LANES), fix),
            pl.BlockSpec((1, D_MODEL), fix),
            pl.BlockSpec((1, D_MODEL), fix),
            pl.BlockSpec((D_MODEL, LANES), fix),
            pl.BlockSpec((LANES, D_MODEL), fix),
        ],
        out_specs=[
            pl.BlockSpec((tm, D_MODEL), lambda i, j: (i, j)),
            pl.BlockSpec((tm, D_MODEL), row),
            pl.BlockSpec((tm, D_MODEL), row),
            pl.BlockSpec((tm, LANES), row),
        ],
        out_shape=[
            jax.ShapeDtypeStruct((t, N_ZB * D_MODEL), BF16),
            jax.ShapeDtypeStruct((t, D_MODEL), F32),
            jax.ShapeDtypeStruct((t, D_MODEL), F32),
            jax.ShapeDtypeStruct((t, LANES), F32),
        ],
        scratch_shapes=[pltpu.VMEM((tm, D_MODEL), BF16)],
        compiler_params=_cparams(("parallel", "arbitrary")),
        name="proj",
    )(x, ln, wmain, wsh, wsl, bias, qn, kn, r64, e64)


def _mlstm_kernel(qk_ref, v_ref, gs_ref, hnw_ref, tril_ref,
                  hn_ref, ccol_ref, crow_ref, c_out, n_out, m_out,
                  c_scr, n_scr, m_scr, crow_scr, ccol_scr):
    c = pl.program_id(1)
    lc = M_CHUNK

    @pl.when(c == 0)
    def _():
        c_scr[...] = jnp.zeros_like(c_scr)
        n_scr[...] = jnp.zeros_like(n_scr)
        m_scr[...] = jnp.zeros_like(m_scr)
        crow_scr[...] = jnp.zeros_like(crow_scr)
        ccol_scr[...] = jnp.zeros_like(ccol_scr)

    g = gs_ref[0]
    lane = lax.broadcasted_iota(I32, (lc, LANES), 1)
    pos = c * lc + lax.broadcasted_iota(I32, (lc, LANES), 0)
    valid = pos >= PAD_FRONT
    is_ig = lane < G_LF
    gsum = jnp.where(valid & jnp.logical_not(is_ig), g, 0.0)
    bcum = _dot_exact_rhs01_lhs(tril_ref[...], _split3(gsum))
    comb = jnp.where(is_ig, jnp.where(valid, g, -jnp.inf), bcum)
    comb_t = comb.T

    row_t = lax.broadcasted_iota(I32, (lc, lc), 0)
    col_s = lax.broadcasted_iota(I32, (lc, lc), 1)
    causal = col_s <= row_t

    qk = qk_ref[0]
    vv = v_ref[0]
    for h in range(M_HEADS):
        q = qk[:, h * M_DK:(h + 1) * M_DK]
        k = qk[:, (M_HEADS + h) * M_DK:(M_HEADS + h + 1) * M_DK]
        v = vv[:, h * M_DV:(h + 1) * M_DV]
        m_prev = m_scr[h, 0:1, 0:1]
        b_col = bcum[:, G_LF + h:G_LF + h + 1]
        ig_col = comb[:, G_IG + h:G_IG + h + 1]
        b_row = comb_t[G_LF + h:G_LF + h + 1, :]
        ig_row = comb_t[G_IG + h:G_IG + h + 1, :]

        log_d = jnp.where(causal, b_col - b_row + ig_row, -jnp.inf)
        inter = b_col + m_prev
        m_row = jnp.maximum(inter, jnp.max(log_d, axis=1, keepdims=True))
        s = _dot_nt(q, k) * jnp.exp(log_d - m_row)
        w_int = jnp.exp(inter - m_row)
        cmat = c_scr[h]
        nvec = n_scr[h, 0:1, :]
        num = _dot(s.astype(BF16), v) + w_int * _dot_nt(q, cmat.astype(BF16))
        den = jnp.sum(s, axis=1, keepdims=True) + w_int * jnp.sum(q.astype(F32) * nvec, axis=1, keepdims=True)
        hh = num / jnp.maximum(jnp.abs(den), jnp.exp(-m_row))
        hn = hh * lax.rsqrt(jnp.mean(hh * hh, axis=1, keepdims=True) + EPS) * hnw_ref[:, h * M_DV:(h + 1) * M_DV]
        hn_ref[0, :, h * M_DV:(h + 1) * M_DV] = hn.astype(BF16)

        b_last = b_col[lc - 1:lc, :]
        m_new = jnp.maximum(b_last + m_prev, jnp.max(b_last - b_row + ig_row, axis=1, keepdims=True))
        decay = jnp.exp(b_last + m_prev - m_new)
        ws_col = jnp.exp(b_last - b_col + ig_col - m_new)
        kw = k.astype(F32) * ws_col
        c_scr[h] = decay * cmat + _dot_tn(v, kw.astype(BF16))
        n_scr[h] = jnp.broadcast_to(decay * nvec + jnp.sum(kw, axis=0, keepdims=True), n_scr.shape[1:])
        m_scr[h] = jnp.broadcast_to(m_new, m_scr.shape[1:])

    ccol_ref[0] = bcum + crow_scr[0:1, :]
    sub = lax.broadcasted_iota(I32, (LANES, lc), 0)
    crow_ref[0] = jnp.where(sub >= G_LF, comb_t + ccol_scr[:, 0:1], 0.0)
    crow_scr[...] = jnp.broadcast_to(crow_scr[0:1, :] + bcum[lc - 1:lc, :], crow_scr.shape)
    ccol_scr[...] = jnp.broadcast_to(
        jnp.where(sub[:, 0:1] >= G_LF, ccol_scr[:, 0:1] + comb_t[:, lc - 1:lc], 0.0), ccol_scr.shape)

    @pl.when(c == pl.num_programs(1) - 1)
    def _():
        c_out[0] = c_scr[...]
        n_out[0] = n_scr[...]
        m_out[0] = m_scr[...]


def _dot_exact_rhs01_lhs(m01, parts):
    acc = _dot(m01, parts[0])
    for p in parts[1:]:
        acc = acc + _dot(m01, p)
    return acc


def _mlstm_prompt(z3, gs3, hnw, tril):
    b, lp, _ = z3.shape
    lc = M_CHUNK
    nc = lp // lc
    return pl.pallas_call(
        _mlstm_kernel,
        grid=(b, nc),
        in_specs=[
            pl.BlockSpec((1, lc, D_MODEL), lambda i, c: (i, c, ZB_MQK)),
            pl.BlockSpec((1, lc, D_MODEL), lambda i, c: (i, c, ZB_MV)),
            pl.BlockSpec((1, lc, LANES), lambda i, c: (i, c, 0)),
            pl.BlockSpec((1, D_MODEL), lambda i, c: (0, 0)),
            pl.BlockSpec((lc, lc), lambda i, c: (0, 0)),
        ],
        out_specs=[
            pl.BlockSpec((1, lc, D_MODEL), lambda i, c: (i, c, 0)),
            pl.BlockSpec((1, lc, LANES), lambda i, c: (i, c, 0)),
            pl.BlockSpec((1, LANES, lc), lambda i, c: (i, 0, c)),
            pl.BlockSpec((1, M_HEADS, M_DV, M_DK), lambda i, c: (i, 0, 0, 0)),
            pl.BlockSpec((1, M_HEADS, 8, M_DK), lambda i, c: (i, 0, 0, 0)),
            pl.BlockSpec((1, M_HEADS, 8, LANES), lambda i, c: (i, 0, 0, 0)),
        ],
        out_shape=[
            jax.ShapeDtypeStruct((b, lp, D_MODEL), BF16),
            jax.ShapeDtypeStruct((b, lp, LANES), F32),
            jax.ShapeDtypeStruct((b, LANES, lp), F32),
            jax.ShapeDtypeStruct((b, M_HEADS, M_DV, M_DK), F32),
            jax.ShapeDtypeStruct((b, M_HEADS, 8, M_DK), F32),
            jax.ShapeDtypeStruct((b, M_HEADS, 8, LANES), F32),
        ],
        scratch_shapes=[
            pltpu.VMEM((M_HEADS, M_DV, M_DK), F32),
            pltpu.VMEM((M_HEADS, 8, M_DK), F32),
            pltpu.VMEM((M_HEADS, 8, LANES), F32),
            pltpu.VMEM((8, LANES), F32),
            pltpu.VMEM((LANES, LANES), F32),
        ],
        compiler_params=_cparams(("parallel", "arbitrary")),
        name="mlstm",
    )(z3, z3, gs3, hnw, tril)


MSTEP_SEQS = 8


def _mstep_kernel(qk_ref, v_ref, gs_ref, m_ref, n_ref, c_ref, hnw_ref,
                  hn_ref, c_out, n_out, m_out):
    sb = MSTEP_SEQS
    qk = qk_ref[...]
    vv = v_ref[...]
    g = gs_ref[...]
    row = lax.broadcasted_iota(I32, (sb, 1), 0)
    m_new_all = jnp.zeros((sb, LANES), F32)
    lane = lax.broadcasted_iota(I32, (sb, LANES), 1)
    for h in range(M_HEADS):
        q = qk[:, h * M_DK:(h + 1) * M_DK]
        k = qk[:, (M_HEADS + h) * M_DK:(M_HEADS + h + 1) * M_DK]
        v = vv[:, h * M_DV:(h + 1) * M_DV]
        ig = g[:, G_IG + h:G_IG + h + 1]
        lf = g[:, G_LF + h:G_LF + h + 1]
        m_prev = m_ref[:, h:h + 1]
        inter = lf + m_prev
        m_row = jnp.maximum(inter, ig)
        d = jnp.exp(ig - m_row)
        w_int = jnp.exp(inter - m_row)
        qf = q.astype(F32)
        kf = k.astype(F32)
        s = jnp.sum(qf * kf, axis=1, keepdims=True) * d
        nvec = n_ref[:, h * M_DK:(h + 1) * M_DK]
        den = s + w_int * jnp.sum(qf * nvec, axis=1, keepdims=True)
        kw = (kf * d).astype(BF16)
        hrows = []
        for i in range(sb):
            cmat = c_ref[i, h]
            sel = row == i
            qi = jnp.where(sel, q, jnp.zeros_like(q))
            qc = _dot_nt(qi, cmat.astype(BF16))
            hrows.append(jnp.where(sel, qc, 0.0))
            vi = jnp.where(sel, v, jnp.zeros_like(v))
            c_out[i, h] = w_int[i:i + 1, :] * cmat + _dot_tn(vi, kw)
        qc_all = functools.reduce(lambda a, b: a + b, hrows)
        num = s * v.astype(F32) + w_int * qc_all
        hh = num / jnp.maximum(jnp.abs(den), jnp.exp(-m_row))
        hn = hh * lax.rsqrt(jnp.mean(hh * hh, axis=1, keepdims=True) + EPS) * hnw_ref[:, h * M_DV:(h + 1) * M_DV]
        hn_ref[:, h * M_DV:(h + 1) * M_DV] = hn.astype(BF16)
        n_out[:, h * M_DK:(h + 1) * M_DK] = w_int * nvec + kf * d
        m_new_all = jnp.where(lane == h, m_row, m_new_all)
    m_out[...] = m_new_all


def _mlstm_step(zs, gss, m_in, n_in, c_in, hnw):
    db = zs.shape[0]
    sb = MSTEP_SEQS
    return pl.pallas_call(
        _mstep_kernel,
        grid=(db // sb,),
        in_specs=[
            pl.BlockSpec((sb, D_MODEL), lambda i: (i, ZB_MQK)),
            pl.BlockSpec((sb, D_MODEL), lambda i: (i, ZB_MV)),
            pl.BlockSpec((sb, LANES), lambda i: (i, 0)),
            pl.BlockSpec((sb, LANES), lambda i: (i, 0)),
            pl.BlockSpec((sb, M_HEADS * M_DK), lambda i: (i, 0)),
            pl.BlockSpec((sb, M_HEADS, M_DV, M_DK), lambda i: (i, 0, 0, 0)),
            pl.BlockSpec((1, D_MODEL), lambda i: (0, 0)),
        ],
        out_specs=[
            pl.BlockSpec((sb, D_MODEL), lambda i: (i, 0)),
            pl.BlockSpec((sb, M_HEADS, M_DV, M_DK), lambda i: (i, 0, 0, 0)),
            pl.BlockSpec((sb, M_HEADS * M_DK), lambda i: (i, 0)),
            pl.BlockSpec((sb, LANES), lambda i: (i, 0)),
        ],
        out_shape=[
            jax.ShapeDtypeStruct((db, D_MODEL), BF16),
            jax.ShapeDtypeStruct((db, M_HEADS, M_DV, M_DK), F32),
            jax.ShapeDtypeStruct((db, M_HEADS * M_DK), F32),
            jax.ShapeDtypeStruct((db, LANES), F32),
        ],
        compiler_params=_cparams(("parallel",)),
        name="mstep",
    )(zs, zs, gss, m_in, n_in, c_in, hnw)


def _fox_kernel(qi_tab, ki_tab, q_ref, k_ref, v_ref, cq_ref, ck_ref, o_ref, acc_scr, m_scr, l_scr):
    step = pl.program_id(1)
    qi = qi_tab[step]
    ki = ki_tab[step]
    tq = tk = ATT_BLOCK

    @pl.when(ki == 0)
    def _():
        acc_scr[...] = jnp.zeros_like(acc_scr)
        m_scr[...] = jnp.full_like(m_scr, -jnp.inf)
        l_scr[...] = jnp.zeros_like(l_scr)

    row = qi * tq + lax.broadcasted_iota(I32, (tq, tk), 0)
    col = ki * tk + lax.broadcasted_iota(I32, (tq, tk), 1)
    visible = (col <= row) & ((col >= PAD_FRONT) | (row < PAD_FRONT))
    lane = lax.broadcasted_iota(I32, (tq, LANES), 1)
    first = lane < F_DH

    cq = cq_ref[0]
    ck = ck_ref[0]
    for p in range(F_HEADS // 2):
        sl = slice(p * LANES, (p + 1) * LANES)
        q2 = q_ref[0, :, sl]
        k2 = k_ref[0, :, sl]
        v2 = v_ref[0, :, sl]
        alphas, pvs = [], []
        for hh in range(2):
            h = 2 * p + hh
            qm = jnp.where(first if hh == 0 else jnp.logical_not(first), q2, jnp.zeros_like(q2))
            s = _dot_nt(qm, k2)
            s = s + cq[:, G_FF + h:G_FF + h + 1] - ck[G_FF + h:G_FF + h + 1, :]
            s = jnp.where(visible, s, -jnp.inf)
            m_prev = m_scr[h]
            m_next = jnp.maximum(m_prev, jnp.max(s, axis=1, keepdims=True))
            alpha = jnp.exp(m_prev - m_next)
            pr = jnp.exp(s - m_next[:, 0:1])
            l_scr[h] = alpha * l_scr[h] + jnp.sum(pr, axis=1, keepdims=True)
            m_scr[h] = m_next
            alphas.append(alpha)
            pvs.append(_dot(pr.astype(BF16), v2))
        acc_scr[:, sl] = jnp.where(first, alphas[0], alphas[1]) * acc_scr[:, sl] + jnp.where(first, pvs[0], pvs[1])

    @pl.when(ki == qi)
    def _():
        for p in range(F_HEADS // 2):
            sl = slice(p * LANES, (p + 1) * LANES)
            l2 = jnp.where(first, l_scr[2 * p], l_scr[2 * p + 1])
            o_ref[0, :, sl] = (acc_scr[:, sl] / l2).astype(BF16)


def _fox_prompt(z3, ccol, crow):
    b, lp, _ = z3.shape
    blk = ATT_BLOCK
    nb = lp // blk
    qi_tab = np.concatenate([np.full((i + 1,), i, np.int32) for i in range(nb)])
    ki_tab = np.concatenate([np.arange(i + 1, dtype=np.int32) for i in range(nb)])
    grid_spec = pltpu.PrefetchScalarGridSpec(
        num_scalar_prefetch=2,
        grid=(b, len(qi_tab)),
        in_specs=[
            pl.BlockSpec((1, blk, D_MODEL), lambda i, s, qt, kt: (i, qt[s], ZB_FQ)),
            pl.BlockSpec((1, blk, D_MODEL), lambda i, s, qt, kt: (i, kt[s], ZB_FK)),
            pl.BlockSpec((1, blk, D_MODEL), lambda i, s, qt, kt: (i, kt[s], ZB_FV)),
            pl.BlockSpec((1, blk, LANES), lambda i, s, qt, kt: (i, qt[s], 0)),
            pl.BlockSpec((1, LANES, blk), lambda i, s, qt, kt: (i, 0, kt[s])),
        ],
        out_specs=pl.BlockSpec((1, blk, D_MODEL), lambda i, s, qt, kt: (i, qt[s], 0)),
        scratch_shapes=[
            pltpu.VMEM((blk, D_MODEL), F32),
            pltpu.VMEM((F_HEADS, blk, LANES), F32),
            pltpu.VMEM((F_HEADS, blk, LANES), F32),
        ],
    )
    return pl.pallas_call(
        _fox_kernel,
        grid_spec=grid_spec,
        out_shape=jax.ShapeDtypeStruct((b, lp, D_MODEL), BF16),
        compiler_params=_cparams(("parallel", "arbitrary")),
        name="fox",
    )(jnp.asarray(qi_tab), jnp.asarray(ki_tab), z3, z3, z3, ccol, crow)


DEC_PAGES = 8


def _lane_bcast_cols(row):
    full = jnp.broadcast_to(row, (LANES, D_MODEL)).T
    return full.reshape(F_HEADS, F_DH, LANES)


def _per_head(x):
    return x.reshape(F_HEADS, 1, LANES)


def _dec_kernel(pt_ref, q_ref, kn_ref, vn_ref, lfn_ref, triu_ref, ones_ref, *rest):
    np_ = DEC_PAGES
    k_refs, v_refs, lf_refs = rest[:np_], rest[np_:2 * np_], rest[2 * np_:3 * np_]
    o_ref, qb_scr, acc_scr, m_scr, l_scr, c_scr = rest[3 * np_:]
    p = pl.program_id(1)

    @pl.when(p == 0)
    def _():
        qb_scr[...] = _lane_bcast_cols(q_ref[0])
        acc_scr[...] = jnp.zeros_like(acc_scr)
        m_scr[...] = jnp.full_like(m_scr, -jnp.inf)
        l_scr[...] = jnp.zeros_like(l_scr)
        c_scr[...] = jnp.zeros_like(c_scr)

    qb = qb_scr[...]
    carry = c_scr[...]
    us = []
    for i in range(np_):
        cum = _dot_exact_rhs01(_split3(lf_refs[i][0]), triu_ref[...]) + carry
        carry = jnp.broadcast_to(cum[:, LANES - 1:LANES], carry.shape)
        s = jnp.sum(qb * k_refs[i][0], axis=1)
        us.append(s - cum)
    c_scr[...] = carry
    m_prev = m_scr[...]
    m_cur = functools.reduce(jnp.maximum, [jnp.max(u, axis=1, keepdims=True) for u in us])
    m_next = jnp.maximum(m_prev, m_cur)
    alpha = jnp.exp(m_prev - m_next)
    acc = _per_head(alpha) * acc_scr[...]
    l_new = alpha * l_scr[...]
    for i in range(np_):
        pr = jnp.exp(us[i] - m_next)
        l_new = l_new + jnp.sum(pr, axis=1, keepdims=True)
        acc = acc + _per_head(pr) * v_refs[i][0]
    acc_scr[...] = acc
    l_scr[...] = l_new
    m_scr[...] = m_next

    @pl.when(p == pl.num_programs(1) - 1)
    def _():
        s_new = jnp.sum(qb * _lane_bcast_cols(kn_ref[0]), axis=1)
        u_new = s_new - (carry + lfn_ref[0])
        m_fin = jnp.maximum(m_next, u_new)
        a_fin = jnp.exp(m_next - m_fin)
        pn = jnp.exp(u_new - m_fin)
        l_fin = a_fin * l_new + pn
        tot = (_per_head(a_fin) * acc + _per_head(pn * (1.0 / LANES)) * _lane_bcast_cols(vn_ref[0])) / _per_head(l_fin)
        out8 = _dot_nt_exact_lhs01(ones_ref[...], _split3(tot.reshape(D_MODEL, LANES)))
        o_ref[0] = out8[0:1, :]


def _dot_nt_exact_lhs01(m01, parts):
    acc = _dot_nt(m01, parts[0])
    for part in parts[1:]:
        acc = acc + _dot_nt(m01, part)
    return acc


def _fox_decode(page_table, q3, kn3, vn3, lfn3, cache_kt, cache_vt, cache_lft):
    db, npg = page_table.shape
    pg = cache_kt.shape[3]
    assert pg == LANES and npg % DEC_PAGES == 0
    row3 = lambda s, p, pt: (s, 0, 0)
    fix = lambda s, p, pt: (0, 0)

    def page(i, nd):
        return lambda s, p, pt: (pt[s, p * DEC_PAGES + i],) + (0,) * nd

    in_specs = [
        pl.BlockSpec((1, 1, D_MODEL), row3),
        pl.BlockSpec((1, 1, D_MODEL), row3),
        pl.BlockSpec((1, 1, D_MODEL), row3),
        pl.BlockSpec((1, F_HEADS, LANES), row3),
        pl.BlockSpec((LANES, LANES), fix),
        pl.BlockSpec((8, LANES), fix),
    ]
    in_specs += [pl.BlockSpec((1, F_HEADS, F_DH, LANES), page(i, 3)) for i in range(DEC_PAGES)]
    in_specs += [pl.BlockSpec((1, F_HEADS, F_DH, LANES), page(i, 3)) for i in range(DEC_PAGES)]
    in_specs += [pl.BlockSpec((1, F_HEADS, LANES), page(i, 2)) for i in range(DEC_PAGES)]
    grid_spec = pltpu.PrefetchScalarGridSpec(
        num_scalar_prefetch=1,
        grid=(db, npg // DEC_PAGES),
        in_specs=in_specs,
        out_specs=pl.BlockSpec((1, 1, D_MODEL), row3),
        scratch_shapes=[
            pltpu.VMEM((F_HEADS, F_DH, LANES), F32),
            pltpu.VMEM((F_HEADS, F_DH, LANES), F32),
            pltpu.VMEM((F_HEADS, LANES), F32),
            pltpu.VMEM((F_HEADS, LANES), F32),
            pltpu.VMEM((F_HEADS, LANES), F32),
        ],
    )
    triu = jnp.asarray(np.triu(np.ones((LANES, LANES), np.float32)), BF16)
    ones = jnp.ones((8, LANES), BF16)
    return pl.pallas_call(
        _dec_kernel,
        grid_spec=grid_spec,
        out_shape=jax.ShapeDtypeStruct((db, 1, D_MODEL), F32),
        compiler_params=_cparams(("parallel", "arbitrary")),
        name="dec",
    )(page_table, q3, kn3, vn3, lfn3, triu, ones,
      *([cache_kt] * DEC_PAGES), *([cache_vt] * DEC_PAGES), *([cache_lft] * DEC_PAGES))


def _merge_kernel(x_ref, hn_ref, so_ref, hf_ref, ga_ref, gb_ref, wbm_ref, wbf_ref, wo_ref, ln2_ref,
                  wrh_ref, wrl_ref, br_ref, x1_ref, t_ref, lg_ref):
    hm = hn_ref[...] * so_ref[...]
    ya = _dot(hm, wbm_ref[...])
    yb = _dot(hf_ref[...], wbf_ref[...])
    u = ga_ref[...].astype(F32) * ya + gb_ref[...].astype(F32) * yb
    x1 = x_ref[...] + _dot(u.astype(BF16), wo_ref[...])
    x1_ref[...] = x1
    t = x1 * lax.rsqrt(jnp.mean(x1 * x1, axis=-1, keepdims=True) + EPS) * ln2_ref[...]
    t_ref[...] = t
    th, tl = _split2(t)
    wrh = wrh_ref[...]
    lg_ref[...] = _dot(th, wrh) + _dot(tl, wrh) + _dot(th, wrl_ref[...]) + br_ref[...]


def _merge(x, hn, z, hf, wbm, wbf, wo, ln2, wrh, wrl, br, tm):
    t = x.shape[0]
    row = lambda i: (i, 0)
    fix = lambda i: (0, 0)
    full = pl.BlockSpec((D_MODEL, D_MODEL), fix)
    return pl.pallas_call(
        _merge_kernel,
        grid=(t // tm,),
        in_specs=[
            pl.BlockSpec((tm, D_MODEL), row),
            pl.BlockSpec((tm, D_MODEL), row),
            pl.BlockSpec((tm, D_MODEL), lambda i: (i, ZB_MO)),
            pl.BlockSpec((tm, D_MODEL), row),
            pl.BlockSpec((tm, D_MODEL), lambda i: (i, ZB_GA)),
            pl.BlockSpec((tm, D_MODEL), lambda i: (i, ZB_GB)),
            full, full, full,
            pl.BlockSpec((1, D_MODEL), fix),
            pl.BlockSpec((D_MODEL, LANES), fix),
            pl.BlockSpec((D_MODEL, LANES), fix),
            pl.BlockSpec((1, LANES), fix),
        ],
        out_specs=[
            pl.BlockSpec((tm, D_MODEL), row),
            pl.BlockSpec((tm, D_MODEL), row),
            pl.BlockSpec((tm, LANES), row),
        ],
        out_shape=[
            jax.ShapeDtypeStruct((t, D_MODEL), F32),
            jax.ShapeDtypeStruct((t, D_MODEL), F32),
            jax.ShapeDtypeStruct((t, LANES), F32),
        ],
        compiler_params=_cparams(("parallel",)),
        name="merge",
    )(x, hn, z, hf, z, z, wbm, wbf, wo, ln2, wrh, wrl, br)


def _route_kernel(lg_ref, tril_ref, ids_ref, wts_ref, rank_ref, cnt_ref, carry_scr):
    i = pl.program_id(0)

    @pl.when(i == 0)
    def _():
        carry_scr[...] = jnp.zeros_like(carry_scr)

    tm = lg_ref.shape[0]
    lane = lax.broadcasted_iota(I32, (tm, LANES), 1)
    lanef = lane.astype(F32)
    lg = jnp.where(lane < N_EXPERTS, lg_ref[...], -jnp.inf)
    vals, idxs, hots = [], [], []
    for _ in range(TOP_K):
        mx = jnp.max(lg, axis=1, keepdims=True)
        idx = jnp.min(jnp.where(lg == mx, lanef, float(LANES)), axis=1, keepdims=True)
        hot = lanef == idx
        lg = jnp.where(hot, -jnp.inf, lg)
        vals.append(mx)
        idxs.append(idx)
        hots.append(hot)
    es = [jnp.exp(v - vals[0]) for v in vals]
    tot = functools.reduce(lambda a, b: a + b, es)
    sel = functools.reduce(lambda a, b: a | b, hots)
    a01 = jnp.where(sel, 1.0, 0.0)
    before = _dot(tril_ref[...], a01.astype(BF16)) + carry_scr[0:1, :]
    carry_scr[...] = jnp.broadcast_to(carry_scr[0:1, :] + jnp.sum(a01, axis=0, keepdims=True), carry_scr.shape)
    ids = jnp.zeros((tm, LANES), I32)
    wts = jnp.zeros((tm, LANES), F32)
    rank = jnp.zeros((tm, LANES), I32)
    for kk in range(TOP_K):
        r = jnp.sum(jnp.where(hots[kk], before, 0.0), axis=1, keepdims=True)
        ids = jnp.where(lane == kk, idxs[kk].astype(I32), ids)
        wts = jnp.where(lane == kk, es[kk] / tot, wts)
        rank = jnp.where(lane == kk, r.astype(I32), rank)
    ids_ref[...] = ids
    wts_ref[...] = wts
    rank_ref[...] = rank
    cnt_ref[...] = carry_scr[...].astype(I32)


def _route(logits, tril_strict, tm):
    t = logits.shape[0]
    row = lambda i: (i, 0)
    return pl.pallas_call(
        _route_kernel,
        grid=(t // tm,),
        in_specs=[pl.BlockSpec((tm, LANES), row), pl.BlockSpec((tm, tm), lambda i: (0, 0))],
        out_specs=[pl.BlockSpec((tm, LANES), row)] * 3 + [pl.BlockSpec((8, LANES), lambda i: (0, 0))],
        out_shape=[
            jax.ShapeDtypeStruct((t, LANES), I32),
            jax.ShapeDtypeStruct((t, LANES), F32),
            jax.ShapeDtypeStruct((t, LANES), I32),
            jax.ShapeDtypeStruct((8, LANES), I32),
        ],
        scratch_shapes=[pltpu.VMEM((8, LANES), F32)],
        compiler_params=_cparams(("arbitrary",)),
        name="route",
    )(logits, tril_strict)


def _gather_kernel(idx_ref, src_ref, out_ref, sem):
    n = out_ref.shape[0]

    def issue(r, carry):
        pltpu.make_async_copy(src_ref.at[pl.ds(idx_ref[0, 0, r], 1), :], out_ref.at[pl.ds(r, 1), :], sem).start()
        return carry

    lax.fori_loop(0, n, issue, 0, unroll=8)
    pltpu.make_async_copy(src_ref.at[pl.ds(0, n), :], out_ref, sem).wait()


def _gather_rows(src, idx, tile):
    n = idx.shape[0]
    w = src.shape[1]
    return pl.pallas_call(
        _gather_kernel,
        grid=(n // tile,),
        in_specs=[
            pl.BlockSpec((1, 1, tile), lambda i: (i, 0, 0), memory_space=pltpu.SMEM),
            pl.BlockSpec(memory_space=pl.ANY),
        ],
        out_specs=pl.BlockSpec((tile, w), lambda i: (i, 0)),
        out_shape=jax.ShapeDtypeStruct((n, w), src.dtype),
        scratch_shapes=[pltpu.SemaphoreType.DMA],
        compiler_params=_cparams(("arbitrary",)),
        name="gather",
    )(idx.reshape(n // tile, 1, tile), src)


def _experts_kernel(te_ref, nu_ref, xs_ref, w1_ref, b1_ref, w2_ref, b2_ref, ys_ref, w1b_scr, w2b_scr):
    j = pl.program_id(0)
    prev = te_ref[jnp.maximum(j - 1, 0)]

    @pl.when((j == 0) | (te_ref[j] != prev))
    def _():
        w1b_scr[...] = w1_ref[0].astype(BF16)
        w2b_scr[...] = w2_ref[0].astype(BF16)

    @pl.when(j < nu_ref[0])
    def _():
        hcat = _dot(xs_ref[...].astype(BF16), w1b_scr[...]) + b1_ref[0]
        g = jnp.minimum(hcat[:, :D_FF], SWIGLU_LIMIT)
        u = jnp.clip(hcat[:, D_FF:], -SWIGLU_LIMIT, SWIGLU_LIMIT)
        a = g * jax.nn.sigmoid(SWIGLU_ALPHA * g) * (u + 1.0)
        ys_ref[...] = _dot(a.astype(BF16), w2b_scr[...]) + b2_ref[0]

    @pl.when(j >= nu_ref[0])
    def _():
        ys_ref[...] = jnp.zeros_like(ys_ref)


def _experts(tile_expert, n_used, xs, w1, b1, w2, b2):
    r = xs.shape[0]
    tm = EXP_TILE
    grid_spec = pltpu.PrefetchScalarGridSpec(
        num_scalar_prefetch=2,
        grid=(r // tm,),
        in_specs=[
            pl.BlockSpec((tm, D_MODEL), lambda j, te, nu: (j, 0)),
            pl.BlockSpec((1, D_MODEL, 2 * D_FF), lambda j, te, nu: (te[j], 0, 0)),
            pl.BlockSpec((1, 1, 2 * D_FF), lambda j, te, nu: (te[j], 0, 0)),
            pl.BlockSpec((1, D_FF, D_MODEL), lambda j, te, nu: (te[j], 0, 0)),
            pl.BlockSpec((1, 1, D_MODEL), lambda j, te, nu: (te[j], 0, 0)),
        ],
        out_specs=pl.BlockSpec((tm, D_MODEL), lambda j, te, nu: (j, 0)),
        scratch_shapes=[
            pltpu.VMEM((D_MODEL, 2 * D_FF), BF16),
            pltpu.VMEM((D_FF, D_MODEL), BF16),
        ],
    )
    return pl.pallas_call(
        _experts_kernel,
        grid_spec=grid_spec,
        out_shape=jax.ShapeDtypeStruct((r, D_MODEL), F32),
        compiler_params=_cparams(("arbitrary",)),
        name="experts",
    )(tile_expert, n_used, xs, w1, b1.reshape(N_EXPERTS, 1, 2 * D_FF), w2, b2.reshape(N_EXPERTS, 1, D_MODEL))


def _combine_kernel(pos_ref, ys_ref, x1_ref, wts_ref, lnf_ref, y_ref, rows_scr, sem):
    tc = x1_ref.shape[0]

    def issue(r, carry):
        for kk in range(TOP_K):
            pltpu.make_async_copy(ys_ref.at[pl.ds(pos_ref[0, kk, r], 1), :],
                                  rows_scr.at[kk, pl.ds(r, 1), :], sem).start()
        return carry

    lax.fori_loop(0, tc, issue, 0, unroll=2)
    for kk in range(TOP_K):
        pltpu.make_async_copy(ys_ref.at[pl.ds(0, tc), :], rows_scr.at[kk], sem).wait()
    w = wts_ref[...]
    acc = x1_ref[...]
    for kk in range(TOP_K):
        acc = acc + w[:, kk:kk + 1] * rows_scr[kk]
    y_ref[...] = acc * lax.rsqrt(jnp.mean(acc * acc, axis=-1, keepdims=True) + EPS) * lnf_ref[...]


def _combine(pos, ys, x1, wts, lnf):
    t = x1.shape[0]
    tc = COMB_TILE
    row = lambda i: (i, 0)
    pos3 = pos.reshape(t // tc, tc, TOP_K).transpose(0, 2, 1)
    return pl.pallas_call(
        _combine_kernel,
        grid=(t // tc,),
        in_specs=[
            pl.BlockSpec((1, TOP_K, tc), lambda i: (i, 0, 0), memory_space=pltpu.SMEM),
            pl.BlockSpec(memory_space=pl.ANY),
            pl.BlockSpec((tc, D_MODEL), row),
            pl.BlockSpec((tc, LANES), row),
            pl.BlockSpec((1, D_MODEL), lambda i: (0, 0)),
        ],
        out_specs=pl.BlockSpec((tc, D_MODEL), row),
        out_shape=jax.ShapeDtypeStruct((t, D_MODEL), F32),
        scratch_shapes=[pltpu.VMEM((TOP_K, tc, D_MODEL), F32), pltpu.SemaphoreType.DMA],
        compiler_params=_cparams(("arbitrary",)),
        name="combine",
    )(pos3, ys, x1, wts, lnf)


def _consts():
    r64 = np.zeros((D_MODEL, LANES), np.float32)
    e64 = np.zeros((LANES, D_MODEL), np.float32)
    for h in range(F_HEADS):
        r64[h * F_DH:(h + 1) * F_DH, h] = 1.0 / F_DH
        e64[h, h * F_DH:(h + 1) * F_DH] = 1.0
    return jnp.asarray(r64, BF16), jnp.asarray(e64, BF16), jnp.asarray(e64, F32)


def _tril(n, strict=False):
    return jnp.asarray(np.tril(np.ones((n, n), np.float32), -1 if strict else 0), BF16)


def _round_up(a, b):
    return (a + b - 1) // b * b


def kernel(x_prompt, x_sample, cache_k, cache_v, cache_logf, state_C, state_n, state_m, page_table, meta_tokens, ln1, w_in, mlstm_b_i, mlstm_b_f, mlstm_head_norm, fox_b_f, fox_q_norm, fox_k_norm, w_branch_mlstm, w_branch_fox, w_out, ln2, w_router, b_router, w_exp_in, b_exp_in, w_exp_out, b_exp_out, ln_final):
    depth = w_in.shape[0]
    b, seq, _ = x_prompt.shape
    db, ds, _ = x_sample.shape
    assert ds == 1 and seq % ATT_BLOCK == 0 and db % MSTEP_SEQS == 0
    l_true = seq + N_META
    lp = seq + ATT_BLOCK
    tp = b * lp
    n_phys, pg = cache_k.shape[1], cache_k.shape[2]
    r64, e64, e64f = _consts()

    xp = jnp.concatenate([jnp.zeros((b, PAD_FRONT, D_MODEL), F32),
                          jnp.broadcast_to(meta_tokens[None].astype(F32), (b, N_META, D_MODEL)),
                          x_prompt], axis=1).reshape(tp, D_MODEL)
    xs = x_sample.reshape(db, D_MODEL)

    outs = {k: [] for k in ("kp", "vp", "lfp", "ks", "vs", "lfs", "cp", "np", "mp", "cs", "ns", "ms")}
    offs = np.cumsum((0,) + (M_HEADS * M_DK, M_HEADS * M_DK, M_HEADS * M_DV, M_HEADS * M_DV, M_HEADS, M_HEADS,
                             F_HEADS * F_DH, F_HEADS * F_DH, F_HEADS * F_DH, F_HEADS, 2 * D_MODEL))
    seg = lambda w, i: w[:, offs[i]:offs[i + 1]]
    for l in range(depth):
        w = w_in[l]
        wmain = jnp.concatenate([seg(w, 0), seg(w, 1), seg(w, 2), seg(w, 3), seg(w, 6), seg(w, 7), seg(w, 8),
                                 seg(w, 10)], axis=1).astype(BF16)
        wsm = jnp.concatenate([seg(w, 4), seg(w, 5), seg(w, 9),
                               jnp.zeros((D_MODEL, LANES - G_END), F32)], axis=1)
        wsh = wsm.astype(BF16)
        wsl = (wsm - wsh.astype(F32)).astype(BF16)
        bias = jnp.concatenate([mlstm_b_i[l], mlstm_b_f[l], fox_b_f[l], jnp.zeros((LANES - G_END,), F32)])[None]
        qn = jnp.tile(fox_q_norm[l], F_HEADS)[None]
        kn = jnp.tile(fox_k_norm[l], F_HEADS)[None]
        hnw = mlstm_head_norm[l][None]
        ln1l = ln1[l][None]
        wbm = w_branch_mlstm[l].astype(BF16)
        wbf = w_branch_fox[l].astype(BF16)
        wo = w_out[l].astype(BF16)
        wr = jnp.concatenate([w_router[l], jnp.zeros((D_MODEL, LANES - N_EXPERTS), F32)], axis=1)
        wrh = wr.astype(BF16)
        wrl = (wr - wrh.astype(F32)).astype(BF16)
        br = jnp.concatenate([b_router[l], jnp.zeros((LANES - N_EXPERTS,), F32)])[None]

        zp, kp32, vp32, gsp = _proj(xp, ln1l, wmain, wsh, wsl, bias, qn, kn, r64, e64, ROW_TILE)
        zp3 = zp.reshape(b, lp, N_ZB * D_MODEL)
        hnp, ccol, crow, c_p, n_p, m_p = _mlstm_prompt(zp3, gsp.reshape(b, lp, LANES), hnw, _tril(M_CHUNK))
        hfp = _fox_prompt(zp3, ccol, crow)
        x1p, tpn, lgp = _merge(xp, hnp.reshape(tp, D_MODEL), zp, hfp.reshape(tp, D_MODEL), wbm, wbf, wo,
                               ln2[l][None], wrh, wrl, br, ROW_TILE)
        outs["kp"].append(kp32.reshape(b, lp, F_HEADS, F_DH)[:, PAD_FRONT:])
        outs["vp"].append(vp32.reshape(b, lp, F_HEADS, F_DH)[:, PAD_FRONT:])
        outs["lfp"].append(gsp.reshape(b, lp, LANES)[:, PAD_FRONT:, G_FF:G_END])
        outs["cp"].append(c_p)
        outs["np"].append(n_p[:, :, 0, :])
        outs["mp"].append(m_p[:, :, 0, 0])

        zs, ks32, vs32, gss = _proj(xs, ln1l, wmain, wsh, wsl, bias, qn, kn, r64, e64, db)
        m_in = jnp.concatenate([state_m[l], jnp.zeros((db, LANES - M_HEADS), F32)], axis=1)
        hns, c_s, n_s, m_s = _mlstm_step(zs, gss, m_in, state_n[l].reshape(db, M_HEADS * M_DK), state_C[l], hnw)
        n_s = n_s.reshape(db, M_HEADS, M_DK)
        q3 = zs[:, ZB_FQ * D_MODEL:(ZB_FQ + 1) * D_MODEL].astype(F32).reshape(db, 1, D_MODEL)
        kn3 = zs[:, ZB_FK * D_MODEL:(ZB_FK + 1) * D_MODEL].astype(F32).reshape(db, 1, D_MODEL)
        hfs = _fox_decode(page_table, q3, kn3, vs32.reshape(db, 1, D_MODEL),
                          jnp.broadcast_to(gss[:, G_FF:G_END, None], (db, F_HEADS, LANES)),
                          jnp.transpose(cache_k[l], (0, 2, 3, 1)), jnp.transpose(cache_v[l], (0, 2, 3, 1)),
                          jnp.transpose(cache_logf[l], (0, 2, 1)))
        x1s, tsn, lgs = _merge(xs, hns, zs, hfs.reshape(db, D_MODEL).astype(BF16), wbm, wbf, wo,
                               ln2[l][None], wrh, wrl, br, db)
        outs["ks"].append(ks32.reshape(db, 1, F_HEADS, F_DH))
        outs["vs"].append(vs32.reshape(db, 1, F_HEADS, F_DH))
        outs["lfs"].append(gss[:, G_FF:G_END].reshape(db, 1, F_HEADS))
        outs["cs"].append(c_s)
        outs["ns"].append(n_s)
        outs["ms"].append(m_s[:, :M_HEADS])

        tt = tp + db
        tmoe = _round_up(tt, ROW_TILE)
        zpad = jnp.zeros((tmoe - tt, D_MODEL), F32)
        t_all = jnp.concatenate([tpn, tsn, zpad], axis=0)
        x1_all = jnp.concatenate([x1p, x1s, zpad], axis=0)
        lg_all = jnp.concatenate([lgp, lgs, jnp.zeros((tmoe - tt, LANES), F32)], axis=0)
        ids, wts, rank, cnt = _route(lg_all, _tril(ROW_TILE, strict=True), ROW_TILE)
        counts = cnt[0, :N_EXPERTS]
        pcounts = (counts + EXP_TILE - 1) // EXP_TILE * EXP_TILE
        ends = jnp.cumsum(pcounts)
        starts = ends - pcounts
        ids4 = ids[:, :TOP_K]
        pos = starts[ids4] + rank[:, :TOP_K]
        r_pad = _round_up(TOP_K * tmoe + N_EXPERTS * (EXP_TILE - 1), EXP_TILE)
        src = jnp.zeros((r_pad,), I32).at[pos.reshape(-1)].set(
            jnp.repeat(jnp.arange(tmoe, dtype=I32), TOP_K))
        n_tiles = r_pad // EXP_TILE
        tile_start = jnp.arange(n_tiles, dtype=I32) * EXP_TILE
        tile_expert = jnp.minimum(jnp.sum((ends[None, :] <= tile_start[:, None]).astype(I32), axis=1),
                                  N_EXPERTS - 1)
        n_used = (ends[-1] // EXP_TILE).astype(I32)[None]
        xsrt = _gather_rows(t_all, src, GATHER_TILE)
        ysrt = _experts(tile_expert, n_used, xsrt, w_exp_in[l], b_exp_in[l], w_exp_out[l], b_exp_out[l])
        lnf = ln_final[None] if l == depth - 1 else jnp.ones((1, D_MODEL), F32)
        y_all = _combine(pos, ysrt, x1_all, wts, lnf)
        if l < depth - 1:
            raise NotImplementedError("deeper stacks need the un-normalised residual stream as well")

    y_prompt = y_all[:tp].reshape(b, lp, D_MODEL)[:, ATT_BLOCK:]
    y_sample = y_all[tp:tp + db].reshape(db, 1, D_MODEL)
    st = lambda k: jnp.stack(outs[k])
    return (y_prompt, y_sample, st("kp"), st("vp"), st("lfp"), st("ks"), st("vs"), st("lfs"),
            st("cp"), st("np"), st("mp"), st("cs"), st("ns"), st("ms"))
```

```python
import functools

import numpy as np
import jax
import jax.numpy as jnp
from jax import lax
from jax.experimental import pallas as pl
from jax.experimental.pallas import tpu as pltpu

F32 = jnp.float32
BF16 = jnp.bfloat16
I32 = jnp.int32

D_MODEL = 1024
N_META = 16
M_HEADS, M_DK, M_DV = 4, 128, 256
F_HEADS, F_DH = 16, 64
N_EXPERTS, TOP_K, D_FF = 32, 4, 1024
GATE_CAP = 15.0
SWIGLU_LIMIT = 7.0
SWIGLU_ALPHA = 1.702
EPS = 1e-6
ATTN_SCALE = F_DH ** -0.5

LANES = 128
ATT_BLOCK = 256
ATT_SUB = 128
PAD_FRONT = ATT_BLOCK - N_META
M_CHUNK = 128
ROW_TILE = 512
EXP_TILE = 256
GATHER_TILE = 256
COMB_TILE = 128
VMEM_LIMIT = 56 * 1024 * 1024

G_IG = 0
G_LF = M_HEADS
G_FF = 2 * M_HEADS
G_END = G_FF + F_HEADS

WB_MQK, WB_MV, WB_MO, WB_FQ, WB_FK, WB_FV, WB_GA, WB_GB = range(8)
N_WB = 8
ZB_MQK, ZB_MV, ZB_MO, ZB_FQ, ZB_FK, ZB_GA, ZB_GB = range(7)
N_ZB = 7


def _dot(a, b):
    return jnp.dot(a, b, preferred_element_type=F32)


def _dot_nt(a, b):
    return lax.dot_general(a, b, (((1,), (1,)), ((), ())), preferred_element_type=F32)


def _dot_tn(a, b):
    return lax.dot_general(a, b, (((0,), (0,)), ((), ())), preferred_element_type=F32)


def _split2(x):
    hi = x.astype(BF16)
    lo = (x - hi.astype(F32)).astype(BF16)
    return hi, lo


def _split3(x):
    a = x.astype(BF16)
    r = x - a.astype(F32)
    b = r.astype(BF16)
    c = (r - b.astype(F32)).astype(BF16)
    return a, b, c


def _dot_exact_rhs01(parts, m01):
    acc = _dot(parts[0], m01)
    for p in parts[1:]:
        acc = acc + _dot(p, m01)
    return acc


def _log_sigmoid(x):
    return jnp.minimum(x, 0.0) - jnp.log1p(jnp.exp(-jnp.abs(x)))


def _cparams(sem):
    return pltpu.CompilerParams(dimension_semantics=sem, vmem_limit_bytes=VMEM_LIMIT)


def _head_norm(y, g, r_ref, e_ref):
    ms = _dot((y * y).astype(BF16), r_ref[...])
    rs = lax.rsqrt(ms + EPS)
    rsx = _dot_exact_rhs01(_split2(rs), e_ref[...])
    return y * rsx * g


def _proj_kernel(x_ref, ln_ref, w_ref, wsh_ref, wsl_ref, bias_ref, qn_ref, kn_ref, r_ref, e_ref,
                 z_ref, kt32_ref, vt32_ref, vt16_ref, gs_ref, h_scr):
    j = pl.program_id(1)

    @pl.when(j == 0)
    def _():
        x = x_ref[...]
        ms = jnp.mean(x * x, axis=-1, keepdims=True)
        h = x * lax.rsqrt(ms + EPS) * ln_ref[...]
        hh, hl = _split2(h)
        h_scr[...] = hh
        wsh = wsh_ref[...]
        g = _dot(hh, wsh) + _dot(hl, wsh) + _dot(hh, wsl_ref[...]) + bias_ref[...]
        lane = lax.broadcasted_iota(I32, g.shape, 1)
        cap = GATE_CAP * jnp.tanh(g / GATE_CAP)
        ls = _log_sigmoid(jnp.where(lane < G_FF, cap, g))
        gs_ref[...] = jnp.where(lane < G_LF, cap, jnp.where(lane < G_END, ls, 0.0))

    y = _dot(h_scr[...], w_ref[...])

    @pl.when(j == WB_MQK)
    def _():
        col = lax.broadcasted_iota(I32, y.shape, 1)
        z_ref[...] = jnp.where(col < M_HEADS * M_DK, y * (M_DK ** -0.5), y).astype(BF16)

    @pl.when(j == WB_MV)
    def _():
        z_ref[...] = y.astype(BF16)

    @pl.when((j == WB_MO) | (j == WB_GA) | (j == WB_GB))
    def _():
        z_ref[...] = jax.nn.sigmoid(y).astype(BF16)

    @pl.when(j == WB_FQ)
    def _():
        z_ref[...] = (_head_norm(y, qn_ref[...], r_ref, e_ref) * ATTN_SCALE).astype(BF16)

    @pl.when(j == WB_FK)
    def _():
        kn = _head_norm(y, kn_ref[...], r_ref, e_ref)
        kt32_ref[...] = kn.T
        z_ref[...] = kn.astype(BF16)

    @pl.when(j == WB_FV)
    def _():
        yt = y.T
        vt32_ref[...] = yt
        vt16_ref[...] = yt.astype(BF16)


def _proj(x, ln, wmain, wsh, wsl, bias, qn, kn, r64, e64, tm):
    t = x.shape[0]
    row = lambda i, j: (i, 0)
    col = lambda i, j: (0, i)
    fix = lambda i, j: (0, 0)
    return pl.pallas_call(
        _proj_kernel,
        grid=(t // tm, N_WB),
        in_specs=[
            pl.BlockSpec((tm, D_MODEL), row),
            pl.BlockSpec((1, D_MODEL), fix),
            pl.BlockSpec((D_MODEL, D_MODEL), lambda i, j: (0, j)),
            pl.BlockSpec((D_MODEL, LANES), fix),
            pl.BlockSpec((D_MODEL, LANES), fix),
            pl.BlockSpec((1, LANES), fix),
            pl.BlockSpec((1, D_MODEL), fix),
            pl.BlockSpec((1, D_MODEL), fix),
            pl.BlockSpec((D_MODEL, LANES), fix),
            pl.BlockSpec((LANES, D_MODEL), fix),
        ],
        out_specs=[
            pl.BlockSpec((tm, D_MODEL), lambda i, j: (i, jnp.where(j >= WB_FV, j - 1, j))),
            pl.BlockSpec((D_MODEL, tm), col),
            pl.BlockSpec((D_MODEL, tm), col),
            pl.BlockSpec((D_MODEL, tm), col),
            pl.BlockSpec((tm, LANES), row),
        ],
        out_shape=[
            jax.ShapeDtypeStruct((t, N_ZB * D_MODEL), BF16),
            jax.ShapeDtypeStruct((D_MODEL, t), F32),
            jax.ShapeDtypeStruct((D_MODEL, t), F32),
            jax.ShapeDtypeStruct((D_MODEL, t), BF16),
            jax.ShapeDtypeStruct((t, LANES), F32),
        ],
        scratch_shapes=[pltpu.VMEM((tm, D_MODEL), BF16)],
        compiler_params=_cparams(("parallel", "arbitrary")),
        name="proj",
    )(x, ln, wmain, wsh, wsl, bias, qn, kn, r64, e64)


def _mlstm_kernel(qk_ref, v_ref, gs_ref, hnw_ref, tril_ref, place_ref,
                  hn_ref, ka_ref, qa_ref, c_out, n_out, m_out,
                  c_scr, n_scr, m_scr, crow_scr):
    c = pl.program_id(1)
    lc = M_CHUNK

    @pl.when(c == 0)
    def _():
        c_scr[...] = jnp.zeros_like(c_scr)
        n_scr[...] = jnp.zeros_like(n_scr)
        m_scr[...] = jnp.zeros_like(m_scr)
        crow_scr[...] = jnp.zeros_like(crow_scr)

    g = gs_ref[0]
    lane = lax.broadcasted_iota(I32, (lc, LANES), 1)
    pos = c * lc + lax.broadcasted_iota(I32, (lc, LANES), 0)
    valid = pos >= PAD_FRONT
    is_ig = lane < G_LF
    gsum = jnp.where(valid & jnp.logical_not(is_ig), g, 0.0)
    bcum = _dot_exact_rhs01_lhs(tril_ref[...], _split3(gsum))
    comb = jnp.where(is_ig, jnp.where(valid, g, -jnp.inf), bcum)
    comb_t = comb.T

    row_t = lax.broadcasted_iota(I32, (lc, lc), 0)
    col_s = lax.broadcasted_iota(I32, (lc, lc), 1)
    causal = col_s <= row_t

    qk = qk_ref[0]
    vv = v_ref[0]
    for h in range(M_HEADS):
        q = qk[:, h * M_DK:(h + 1) * M_DK]
        k = qk[:, (M_HEADS + h) * M_DK:(M_HEADS + h + 1) * M_DK]
        v = vv[:, h * M_DV:(h + 1) * M_DV]
        m_prev = m_scr[h, 0:1, 0:1]
        b_col = bcum[:, G_LF + h:G_LF + h + 1]
        ig_col = comb[:, G_IG + h:G_IG + h + 1]
        b_row = comb_t[G_LF + h:G_LF + h + 1, :]
        ig_row = comb_t[G_IG + h:G_IG + h + 1, :]

        log_d = jnp.where(causal, b_col - b_row + ig_row, -jnp.inf)
        inter = b_col + m_prev
        m_row = jnp.maximum(inter, jnp.max(log_d, axis=1, keepdims=True))
        s = _dot_nt(q, k) * jnp.exp(log_d - m_row)
        w_int = jnp.exp(inter - m_row)
        cmat = c_scr[h]
        nvec = n_scr[h, 0:1, :]
        num = _dot(s.astype(BF16), v) + w_int * _dot_nt(q, cmat.astype(BF16))
        den = jnp.sum(s, axis=1, keepdims=True) + w_int * jnp.sum(q.astype(F32) * nvec, axis=1, keepdims=True)
        hh = num / jnp.maximum(jnp.abs(den), jnp.exp(-m_row))
        hn = hh * lax.rsqrt(jnp.mean(hh * hh, axis=1, keepdims=True) + EPS) * hnw_ref[:, h * M_DV:(h + 1) * M_DV]
        hn_ref[0, :, h * M_DV:(h + 1) * M_DV] = hn.astype(BF16)

        b_last = b_col[lc - 1:lc, :]
        m_new = jnp.maximum(b_last + m_prev, jnp.max(b_last - b_row + ig_row, axis=1, keepdims=True))
        decay = jnp.exp(b_last + m_prev - m_new)
        ws_col = jnp.exp(b_last - b_col + ig_col - m_new)
        kw = k.astype(F32) * ws_col
        c_scr[h] = decay * cmat + _dot_tn(v, kw.astype(BF16))
        n_scr[h] = jnp.broadcast_to(decay * nvec + jnp.sum(kw, axis=0, keepdims=True), n_scr.shape[1:])
        m_scr[h] = jnp.broadcast_to(m_new, m_scr.shape[1:])

    cg = bcum + crow_scr[0:1, :]
    crow_scr[...] = jnp.broadcast_to(cg[lc - 1:lc, :], crow_scr.shape)
    p0, p1, p2 = _split3(cg)
    p0 = jnp.where(lane == 0, jnp.ones_like(p0), p0)
    ka_ref[0] = (_dot(p0, place_ref[0]) + _dot(p1, place_ref[1]) + _dot(p2, place_ref[2])).astype(BF16)
    qa_ref[0] = (_dot(p0, place_ref[3]) + _dot(p1, place_ref[4]) + _dot(p2, place_ref[5])).astype(BF16)

    @pl.when(c == pl.num_programs(1) - 1)
    def _():
        c_out[0] = c_scr[...]
        n_out[0] = n_scr[...]
        m_out[0] = m_scr[...]


def _dot_exact_rhs01_lhs(m01, parts):
    acc = _dot(m01, parts[0])
    for p in parts[1:]:
        acc = acc + _dot(m01, p)
    return acc


BIAS_LANES = 6


def _bias_placement():
    pm = np.zeros((6, LANES, LANES), np.float32)
    for h in range(F_HEADS):
        base = BIAS_LANES * h
        for t in range(3):
            pm[t, G_FF + h, base + t] = -1.0
            pm[0, 0, base + 3 + t] = 1.0
            pm[3 + t, G_FF + h, base + 3 + t] = 1.0
            pm[3, 0, base + t] = 1.0
    return jnp.asarray(pm, BF16)


def _mlstm_prompt(z3, gs3, hnw, tril):
    b, lp, _ = z3.shape
    lc = M_CHUNK
    nc = lp // lc
    return pl.pallas_call(
        _mlstm_kernel,
        grid=(b, nc),
        in_specs=[
            pl.BlockSpec((1, lc, D_MODEL), lambda i, c: (i, c, ZB_MQK)),
            pl.BlockSpec((1, lc, D_MODEL), lambda i, c: (i, c, ZB_MV)),
            pl.BlockSpec((1, lc, LANES), lambda i, c: (i, c, 0)),
            pl.BlockSpec((1, D_MODEL), lambda i, c: (0, 0)),
            pl.BlockSpec((lc, lc), lambda i, c: (0, 0)),
            pl.BlockSpec((6, LANES, LANES), lambda i, c: (0, 0, 0)),
        ],
        out_specs=[
            pl.BlockSpec((1, lc, D_MODEL), lambda i, c: (i, c, 0)),
            pl.BlockSpec((1, lc, LANES), lambda i, c: (i, c, 0)),
            pl.BlockSpec((1, lc, LANES), lambda i, c: (i, c, 0)),
            pl.BlockSpec((1, M_HEADS, M_DV, M_DK), lambda i, c: (i, 0, 0, 0)),
            pl.BlockSpec((1, M_HEADS, 8, M_DK), lambda i, c: (i, 0, 0, 0)),
            pl.BlockSpec((1, M_HEADS, 8, LANES), lambda i, c: (i, 0, 0, 0)),
        ],
        out_shape=[
            jax.ShapeDtypeStruct((b, lp, D_MODEL), BF16),
            jax.ShapeDtypeStruct((b, lp, LANES), BF16),
            jax.ShapeDtypeStruct((b, lp, LANES), BF16),
            jax.ShapeDtypeStruct((b, M_HEADS, M_DV, M_DK), F32),
            jax.ShapeDtypeStruct((b, M_HEADS, 8, M_DK), F32),
            jax.ShapeDtypeStruct((b, M_HEADS, 8, LANES), F32),
        ],
        scratch_shapes=[
            pltpu.VMEM((M_HEADS, M_DV, M_DK), F32),
            pltpu.VMEM((M_HEADS, 8, M_DK), F32),
            pltpu.VMEM((M_HEADS, 8, LANES), F32),
            pltpu.VMEM((8, LANES), F32),
        ],
        compiler_params=_cparams(("parallel", "arbitrary")),
        name="mlstm",
    )(z3, z3, gs3, hnw, tril, _bias_placement())


MSTEP_SEQS = 8


def _mstep_kernel(qk_ref, v_ref, gs_ref, m_ref, n_ref, c_ref, hnw_ref,
                  hn_ref, c_out, n_out, m_out):
    sb = MSTEP_SEQS
    qk = qk_ref[...]
    vv = v_ref[...]
    g = gs_ref[...]
    row = lax.broadcasted_iota(I32, (sb, 1), 0)
    m_new_all = jnp.zeros((sb, LANES), F32)
    lane = lax.broadcasted_iota(I32, (sb, LANES), 1)
    for h in range(M_HEADS):
        q = qk[:, h * M_DK:(h + 1) * M_DK]
        k = qk[:, (M_HEADS + h) * M_DK:(M_HEADS + h + 1) * M_DK]
        v = vv[:, h * M_DV:(h + 1) * M_DV]
        ig = g[:, G_IG + h:G_IG + h + 1]
        lf = g[:, G_LF + h:G_LF + h + 1]
        m_prev = m_ref[:, h:h + 1]
        inter = lf + m_prev
        m_row = jnp.maximum(inter, ig)
        d = jnp.exp(ig - m_row)
        w_int = jnp.exp(inter - m_row)
        qf = q.astype(F32)
        kf = k.astype(F32)
        s = jnp.sum(qf * kf, axis=1, keepdims=True) * d
        nvec = n_ref[:, h * M_DK:(h + 1) * M_DK]
        den = s + w_int * jnp.sum(qf * nvec, axis=1, keepdims=True)
        kw = (kf * d).astype(BF16)
        hrows = []
        for i in range(sb):
            cmat = c_ref[i, h]
            sel = row == i
            qi = jnp.where(sel, q, jnp.zeros_like(q))
            qc = _dot_nt(qi, cmat.astype(BF16))
            hrows.append(jnp.where(sel, qc, 0.0))
            vi = jnp.where(sel, v, jnp.zeros_like(v))
            c_out[i, h] = w_int[i:i + 1, :] * cmat + _dot_tn(vi, kw)
        qc_all = functools.reduce(lambda a, b: a + b, hrows)
        num = s * v.astype(F32) + w_int * qc_all
        hh = num / jnp.maximum(jnp.abs(den), jnp.exp(-m_row))
        hn = hh * lax.rsqrt(jnp.mean(hh * hh, axis=1, keepdims=True) + EPS) * hnw_ref[:, h * M_DV:(h + 1) * M_DV]
        hn_ref[:, h * M_DV:(h + 1) * M_DV] = hn.astype(BF16)
        n_out[:, h * M_DK:(h + 1) * M_DK] = w_int * nvec + kf * d
        m_new_all = jnp.where(lane == h, m_row, m_new_all)
    m_out[...] = m_new_all


def _mlstm_step(zs, gss, m_in, n_in, c_in, hnw):
    db = zs.shape[0]
    sb = MSTEP_SEQS
    return pl.pallas_call(
        _mstep_kernel,
        grid=(db // sb,),
        in_specs=[
            pl.BlockSpec((sb, D_MODEL), lambda i: (i, ZB_MQK)),
            pl.BlockSpec((sb, D_MODEL), lambda i: (i, ZB_MV)),
            pl.BlockSpec((sb, LANES), lambda i: (i, 0)),
            pl.BlockSpec((sb, LANES), lambda i: (i, 0)),
            pl.BlockSpec((sb, M_HEADS * M_DK), lambda i: (i, 0)),
            pl.BlockSpec((sb, M_HEADS, M_DV, M_DK), lambda i: (i, 0, 0, 0)),
            pl.BlockSpec((1, D_MODEL), lambda i: (0, 0)),
        ],
        out_specs=[
            pl.BlockSpec((sb, D_MODEL), lambda i: (i, 0)),
            pl.BlockSpec((sb, M_HEADS, M_DV, M_DK), lambda i: (i, 0, 0, 0)),
            pl.BlockSpec((sb, M_HEADS * M_DK), lambda i: (i, 0)),
            pl.BlockSpec((sb, LANES), lambda i: (i, 0)),
        ],
        out_shape=[
            jax.ShapeDtypeStruct((db, D_MODEL), BF16),
            jax.ShapeDtypeStruct((db, M_HEADS, M_DV, M_DK), F32),
            jax.ShapeDtypeStruct((db, M_HEADS * M_DK), F32),
            jax.ShapeDtypeStruct((db, LANES), F32),
        ],
        compiler_params=_cparams(("parallel",)),
        name="mstep",
    )(zs, zs, gss, m_in, n_in, c_in, hnw)


def _fox_kernel(qi_tab, ki_tab, q_ref, qa_ref, k_ref, ka_ref, vt_ref, o_ref,
                qb_scr, acc_scr, m_scr, l_scr, a_scr, st_scr, pt_scr):
    step = pl.program_id(1)
    qi = qi_tab[step]
    ki = ki_tab[step]
    tq = tk = ATT_BLOCK

    @pl.when(ki == 0)
    def _():
        acc_scr[...] = jnp.zeros_like(acc_scr)
        m_scr[...] = jnp.full_like(m_scr, -jnp.inf)
        l_scr[...] = jnp.zeros_like(l_scr)
        lane = lax.broadcasted_iota(I32, (tq, LANES), 1)
        qa = qa_ref[0]
        for h in range(F_HEADS):
            q2 = q_ref[0, :, (h // 2) * LANES:(h // 2 + 1) * LANES]
            own = (lane >= (h % 2) * F_DH) & (lane < (h % 2 + 1) * F_DH)
            bias = (lane >= h * BIAS_LANES) & (lane < (h + 1) * BIAS_LANES)
            qb_scr[h] = jnp.concatenate([jnp.where(own, q2, jnp.zeros_like(q2)),
                                         jnp.where(bias, qa, jnp.zeros_like(qa))], axis=1)

    def sweep(masked):
        ka = ka_ref[0]
        if masked:
            key = ki * tk + lax.broadcasted_iota(I32, (tk, tq), 0)
            qry = qi * tq + lax.broadcasted_iota(I32, (tk, tq), 1)
            visible = (key <= qry) & ((key >= PAD_FRONT) | (qry < PAD_FRONT))
        for p in range(F_HEADS // 2):
            kb = jnp.concatenate([k_ref[0, :, p * LANES:(p + 1) * LANES], ka], axis=1)
            for h in (2 * p, 2 * p + 1):
                st_scr[h] = _dot_nt(kb, qb_scr[h])
        for h in range(F_HEADS):
            st = st_scr[h]
            if masked:
                st = jnp.where(visible, st, -jnp.inf)
            m_prev = m_scr[h, 0:1, :]
            m_next = jnp.maximum(m_prev, jnp.max(st, axis=0, keepdims=True))
            alpha = jnp.exp(m_prev - m_next)
            pt = jnp.exp(st - m_next)
            l_scr[h] = jnp.broadcast_to(alpha * l_scr[h, 0:1, :] + jnp.sum(pt, axis=0, keepdims=True), (8, tq))
            m_scr[h] = jnp.broadcast_to(m_next, (8, tq))
            a_scr[h] = jnp.broadcast_to(alpha, (8, tq))
            pt_scr[h] = pt.astype(BF16)
        for h in range(F_HEADS):
            acc_scr[h] = a_scr[h, 0:1, :] * acc_scr[h] + _dot(vt_ref[h * F_DH:(h + 1) * F_DH, :], pt_scr[h])

    edge = (ki == qi) | (ki == 0)

    @pl.when(edge)
    def _():
        sweep(True)

    @pl.when(jnp.logical_not(edge))
    def _():
        sweep(False)

    @pl.when(ki == qi)
    def _():
        for p in range(F_HEADS // 2):
            o2 = jnp.concatenate([acc_scr[2 * p] * (1.0 / l_scr[2 * p, 0:1, :]),
                                  acc_scr[2 * p + 1] * (1.0 / l_scr[2 * p + 1, 0:1, :])], axis=0)
            o_ref[0, :, p * LANES:(p + 1) * LANES] = o2.T.astype(BF16)


def _fox_prompt(z3, qa, ka, vt16):
    b, lp, _ = z3.shape
    blk = ATT_BLOCK
    nb = lp // blk
    qi_tab = np.concatenate([np.full((i + 1,), i, np.int32) for i in range(nb)])
    ki_tab = np.concatenate([np.arange(i + 1, dtype=np.int32) for i in range(nb)])
    grid_spec = pltpu.PrefetchScalarGridSpec(
        num_scalar_prefetch=2,
        grid=(b, len(qi_tab)),
        in_specs=[
            pl.BlockSpec((1, blk, D_MODEL), lambda i, s, qt, kt: (i, qt[s], ZB_FQ)),
            pl.BlockSpec((1, blk, LANES), lambda i, s, qt, kt: (i, qt[s], 0)),
            pl.BlockSpec((1, blk, D_MODEL), lambda i, s, qt, kt: (i, kt[s], ZB_FK)),
            pl.BlockSpec((1, blk, LANES), lambda i, s, qt, kt: (i, kt[s], 0)),
            pl.BlockSpec((D_MODEL, blk), lambda i, s, qt, kt: (0, i * nb + kt[s])),
        ],
        out_specs=pl.BlockSpec((1, blk, D_MODEL), lambda i, s, qt, kt: (i, qt[s], 0)),
        scratch_shapes=[
            pltpu.VMEM((F_HEADS, blk, 2 * LANES), BF16),
            pltpu.VMEM((F_HEADS, F_DH, blk), F32),
            pltpu.VMEM((F_HEADS, 8, blk), F32),
            pltpu.VMEM((F_HEADS, 8, blk), F32),
            pltpu.VMEM((F_HEADS, 8, blk), F32),
            pltpu.VMEM((F_HEADS, blk, blk), F32),
            pltpu.VMEM((F_HEADS, blk, blk), BF16),
        ],
    )
    return pl.pallas_call(
        _fox_kernel,
        grid_spec=grid_spec,
        out_shape=jax.ShapeDtypeStruct((b, lp, D_MODEL), BF16),
        compiler_params=_cparams(("parallel", "arbitrary")),
        name="fox",
    )(jnp.asarray(qi_tab), jnp.asarray(ki_tab), z3, qa, z3, ka, vt16)


DEC_PAGES = 8


def _lane_bcast_cols(row):
    full = jnp.broadcast_to(row, (LANES, D_MODEL)).T
    return full.reshape(F_HEADS, F_DH, LANES)


def _per_head(x):
    return x.reshape(F_HEADS, 1, LANES)


def _dec_kernel(pt_ref, q_ref, kn_ref, vn_ref, lfn_ref, triu_ref, ones_ref, *rest):
    np_ = DEC_PAGES
    k_refs, v_refs, lf_refs = rest[:np_], rest[np_:2 * np_], rest[2 * np_:3 * np_]
    o_ref, qb_scr, acc_scr, m_scr, l_scr, c_scr = rest[3 * np_:]
    p = pl.program_id(1)

    @pl.when(p == 0)
    def _():
        qb_scr[...] = _lane_bcast_cols(q_ref[0])
        acc_scr[...] = jnp.zeros_like(acc_scr)
        m_scr[...] = jnp.full_like(m_scr, -jnp.inf)
        l_scr[...] = jnp.zeros_like(l_scr)
        c_scr[...] = jnp.zeros_like(c_scr)

    qb = qb_scr[...]
    carry = c_scr[...]
    us = []
    for i in range(np_):
        cum = _dot_exact_rhs01(_split3(lf_refs[i][0]), triu_ref[...]) + carry
        carry = jnp.broadcast_to(cum[:, LANES - 1:LANES], carry.shape)
        s = jnp.sum(qb * k_refs[i][0], axis=1)
        us.append(s - cum)
    c_scr[...] = carry
    m_prev = m_scr[...]
    m_cur = functools.reduce(jnp.maximum, [jnp.max(u, axis=1, keepdims=True) for u in us])
    m_next = jnp.maximum(m_prev, m_cur)
    alpha = jnp.exp(m_prev - m_next)
    acc = _per_head(alpha) * acc_scr[...]
    l_new = alpha * l_scr[...]
    for i in range(np_):
        pr = jnp.exp(us[i] - m_next)
        l_new = l_new + jnp.sum(pr, axis=1, keepdims=True)
        acc = acc + _per_head(pr) * v_refs[i][0]
    acc_scr[...] = acc
    l_scr[...] = l_new
    m_scr[...] = m_next

    @pl.when(p == pl.num_programs(1) - 1)
    def _():
        s_new = jnp.sum(qb * _lane_bcast_cols(kn_ref[0]), axis=1)
        u_new = s_new - (carry + lfn_ref[0])
        m_fin = jnp.maximum(m_next, u_new)
        a_fin = jnp.exp(m_next - m_fin)
        pn = jnp.exp(u_new - m_fin)
        l_fin = a_fin * l_new + pn
        tot = (_per_head(a_fin) * acc + _per_head(pn * (1.0 / LANES)) * _lane_bcast_cols(vn_ref[0])) / _per_head(l_fin)
        out8 = _dot_nt_exact_lhs01(ones_ref[...], _split3(tot.reshape(D_MODEL, LANES)))
        o_ref[0] = out8[0:1, :]


def _dot_nt_exact_lhs01(m01, parts):
    acc = _dot_nt(m01, parts[0])
    for part in parts[1:]:
        acc = acc + _dot_nt(m01, part)
    return acc


def _fox_decode(page_table, q3, kn3, vn3, lfn3, cache_kt, cache_vt, cache_lft):
    db, npg = page_table.shape
    pg = cache_kt.shape[3]
    assert pg == LANES and npg % DEC_PAGES == 0
    row3 = lambda s, p, pt: (s, 0, 0)
    fix = lambda s, p, pt: (0, 0)

    def page(i, nd):
        return lambda s, p, pt: (pt[s, p * DEC_PAGES + i],) + (0,) * nd

    in_specs = [
        pl.BlockSpec((1, 1, D_MODEL), row3),
        pl.BlockSpec((1, 1, D_MODEL), row3),
        pl.BlockSpec((1, 1, D_MODEL), row3),
        pl.BlockSpec((1, F_HEADS, LANES), row3),
        pl.BlockSpec((LANES, LANES), fix),
        pl.BlockSpec((8, LANES), fix),
    ]
    in_specs += [pl.BlockSpec((1, F_HEADS, F_DH, LANES), page(i, 3)) for i in range(DEC_PAGES)]
    in_specs += [pl.BlockSpec((1, F_HEADS, F_DH, LANES), page(i, 3)) for i in range(DEC_PAGES)]
    in_specs += [pl.BlockSpec((1, F_HEADS, LANES), page(i, 2)) for i in range(DEC_PAGES)]
    grid_spec = pltpu.PrefetchScalarGridSpec(
        num_scalar_prefetch=1,
        grid=(db, npg // DEC_PAGES),
        in_specs=in_specs,
        out_specs=pl.BlockSpec((1, 1, D_MODEL), row3),
        scratch_shapes=[
            pltpu.VMEM((F_HEADS, F_DH, LANES), F32),
            pltpu.VMEM((F_HEADS, F_DH, LANES), F32),
            pltpu.VMEM((F_HEADS, LANES), F32),
            pltpu.VMEM((F_HEADS, LANES), F32),
            pltpu.VMEM((F_HEADS, LANES), F32),
        ],
    )
    triu = jnp.asarray(np.triu(np.ones((LANES, LANES), np.float32)), BF16)
    ones = jnp.ones((8, LANES), BF16)
    return pl.pallas_call(
        _dec_kernel,
        grid_spec=grid_spec,
        out_shape=jax.ShapeDtypeStruct((db, 1, D_MODEL), F32),
        compiler_params=_cparams(("parallel", "arbitrary")),
        name="dec",
    )(page_table, q3, kn3, vn3, lfn3, triu, ones,
      *([cache_kt] * DEC_PAGES), *([cache_vt] * DEC_PAGES), *([cache_lft] * DEC_PAGES))


def _merge_kernel(x_ref, hn_ref, so_ref, hf_ref, ga_ref, gb_ref, wbm_ref, wbf_ref, wo_ref, ln2_ref,
                  wrh_ref, wrl_ref, br_ref, x1_ref, t_ref, lg_ref):
    hm = hn_ref[...] * so_ref[...]
    ya = _dot(hm, wbm_ref[...])
    yb = _dot(hf_ref[...], wbf_ref[...])
    u = ga_ref[...].astype(F32) * ya + gb_ref[...].astype(F32) * yb
    x1 = x_ref[...] + _dot(u.astype(BF16), wo_ref[...])
    x1_ref[...] = x1
    t = x1 * lax.rsqrt(jnp.mean(x1 * x1, axis=-1, keepdims=True) + EPS) * ln2_ref[...]
    t_ref[...] = t
    th, tl = _split2(t)
    wrh = wrh_ref[...]
    lg_ref[...] = _dot(th, wrh) + _dot(tl, wrh) + _dot(th, wrl_ref[...]) + br_ref[...]


def _merge(x, hn, z, hf, wbm, wbf, wo, ln2, wrh, wrl, br, tm):
    t = x.shape[0]
    row = lambda i: (i, 0)
    fix = lambda i: (0, 0)
    full = pl.BlockSpec((D_MODEL, D_MODEL), fix)
    return pl.pallas_call(
        _merge_kernel,
        grid=(t // tm,),
        in_specs=[
            pl.BlockSpec((tm, D_MODEL), row),
            pl.BlockSpec((tm, D_MODEL), row),
            pl.BlockSpec((tm, D_MODEL), lambda i: (i, ZB_MO)),
            pl.BlockSpec((tm, D_MODEL), row),
            pl.BlockSpec((tm, D_MODEL), lambda i: (i, ZB_GA)),
            pl.BlockSpec((tm, D_MODEL), lambda i: (i, ZB_GB)),
            full, full, full,
            pl.BlockSpec((1, D_MODEL), fix),
            pl.BlockSpec((D_MODEL, LANES), fix),
            pl.BlockSpec((D_MODEL, LANES), fix),
            pl.BlockSpec((1, LANES), fix),
        ],
        out_specs=[
            pl.BlockSpec((tm, D_MODEL), row),
            pl.BlockSpec((tm, D_MODEL), row),
            pl.BlockSpec((tm, LANES), row),
        ],
        out_shape=[
            jax.ShapeDtypeStruct((t, D_MODEL), F32),
            jax.ShapeDtypeStruct((t, D_MODEL), F32),
            jax.ShapeDtypeStruct((t, LANES), F32),
        ],
        compiler_params=_cparams(("parallel",)),
        name="merge",
    )(x, hn, z, hf, z, z, wbm, wbf, wo, ln2, wrh, wrl, br)


def _route_kernel(lg_ref, tril_ref, ids_ref, wts_ref, rank_ref, cnt_ref, carry_scr):
    i = pl.program_id(0)

    @pl.when(i == 0)
    def _():
        carry_scr[...] = jnp.zeros_like(carry_scr)

    tm = lg_ref.shape[0]
    lane = lax.broadcasted_iota(I32, (tm, LANES), 1)
    lanef = lane.astype(F32)
    lg = jnp.where(lane < N_EXPERTS, lg_ref[...], -jnp.inf)
    vals, idxs, hots = [], [], []
    for _ in range(TOP_K):
        mx = jnp.max(lg, axis=1, keepdims=True)
        idx = jnp.min(jnp.where(lg == mx, lanef, float(LANES)), axis=1, keepdims=True)
        hot = lanef == idx
        lg = jnp.where(hot, -jnp.inf, lg)
        vals.append(mx)
        idxs.append(idx)
        hots.append(hot)
    es = [jnp.exp(v - vals[0]) for v in vals]
    tot = functools.reduce(lambda a, b: a + b, es)
    sel = functools.reduce(lambda a, b: a | b, hots)
    a01 = jnp.where(sel, 1.0, 0.0)
    before = _dot(tril_ref[...], a01.astype(BF16)) + carry_scr[0:1, :]
    carry_scr[...] = jnp.broadcast_to(carry_scr[0:1, :] + jnp.sum(a01, axis=0, keepdims=True), carry_scr.shape)
    ids = jnp.zeros((tm, LANES), I32)
    wts = jnp.zeros((tm, LANES), F32)
    rank = jnp.zeros((tm, LANES), I32)
    for kk in range(TOP_K):
        r = jnp.sum(jnp.where(hots[kk], before, 0.0), axis=1, keepdims=True)
        ids = jnp.where(lane == kk, idxs[kk].astype(I32), ids)
        wts = jnp.where(lane == kk, es[kk] / tot, wts)
        rank = jnp.where(lane == kk, r.astype(I32), rank)
    ids_ref[...] = ids
    wts_ref[...] = wts
    rank_ref[...] = rank
    cnt_ref[...] = carry_scr[...].astype(I32)


def _route(logits, tril_strict, tm):
    t = logits.shape[0]
    row = lambda i: (i, 0)
    return pl.pallas_call(
        _route_kernel,
        grid=(t // tm,),
        in_specs=[pl.BlockSpec((tm, LANES), row), pl.BlockSpec((tm, tm), lambda i: (0, 0))],
        out_specs=[pl.BlockSpec((tm, LANES), row)] * 3 + [pl.BlockSpec((8, LANES), lambda i: (0, 0))],
        out_shape=[
            jax.ShapeDtypeStruct((t, LANES), I32),
            jax.ShapeDtypeStruct((t, LANES), F32),
            jax.ShapeDtypeStruct((t, LANES), I32),
            jax.ShapeDtypeStruct((8, LANES), I32),
        ],
        scratch_shapes=[pltpu.VMEM((8, LANES), F32)],
        compiler_params=_cparams(("arbitrary",)),
        name="route",
    )(logits, tril_strict)


def _gather_kernel(idx_ref, src_ref, out_ref, sem):
    n = out_ref.shape[0]

    def issue(r, carry):
        pltpu.make_async_copy(src_ref.at[pl.ds(idx_ref[0, 0, r], 1), :], out_ref.at[pl.ds(r, 1), :], sem).start()
        return carry

    lax.fori_loop(0, n, issue, 0, unroll=8)
    pltpu.make_async_copy(src_ref.at[pl.ds(0, n), :], out_ref, sem).wait()


def _gather_rows(src, idx, tile):
    n = idx.shape[0]
    w = src.shape[1]
    return pl.pallas_call(
        _gather_kernel,
        grid=(n // tile,),
        in_specs=[
            pl.BlockSpec((1, 1, tile), lambda i: (i, 0, 0), memory_space=pltpu.SMEM),
            pl.BlockSpec(memory_space=pl.ANY),
        ],
        out_specs=pl.BlockSpec((tile, w), lambda i: (i, 0)),
        out_shape=jax.ShapeDtypeStruct((n, w), src.dtype),
        scratch_shapes=[pltpu.SemaphoreType.DMA],
        compiler_params=_cparams(("arbitrary",)),
        name="gather",
    )(idx.reshape(n // tile, 1, tile), src)


def _experts_kernel(te_ref, nu_ref, xs_ref, w1_ref, b1_ref, w2_ref, b2_ref, ys_ref, w1b_scr, w2b_scr):
    j = pl.program_id(0)
    prev = te_ref[jnp.maximum(j - 1, 0)]

    @pl.when((j == 0) | (te_ref[j] != prev))
    def _():
        w1b_scr[...] = w1_ref[0].astype(BF16)
        w2b_scr[...] = w2_ref[0].astype(BF16)

    @pl.when(j < nu_ref[0])
    def _():
        hcat = _dot(xs_ref[...].astype(BF16), w1b_scr[...]) + b1_ref[0]
        g = jnp.minimum(hcat[:, :D_FF], SWIGLU_LIMIT)
        u = jnp.clip(hcat[:, D_FF:], -SWIGLU_LIMIT, SWIGLU_LIMIT)
        a = g * jax.nn.sigmoid(SWIGLU_ALPHA * g) * (u + 1.0)
        ys_ref[...] = _dot(a.astype(BF16), w2b_scr[...]) + b2_ref[0]

    @pl.when(j >= nu_ref[0])
    def _():
        ys_ref[...] = jnp.zeros_like(ys_ref)


def _experts(tile_expert, n_used, xs, w1, b1, w2, b2):
    r = xs.shape[0]
    tm = EXP_TILE
    grid_spec = pltpu.PrefetchScalarGridSpec(
        num_scalar_prefetch=2,
        grid=(r // tm,),
        in_specs=[
            pl.BlockSpec((tm, D_MODEL), lambda j, te, nu: (j, 0)),
            pl.BlockSpec((1, D_MODEL, 2 * D_FF), lambda j, te, nu: (te[j], 0, 0)),
            pl.BlockSpec((1, 1, 2 * D_FF), lambda j, te, nu: (te[j], 0, 0)),
            pl.BlockSpec((1, D_FF, D_MODEL), lambda j, te, nu: (te[j], 0, 0)),
            pl.BlockSpec((1, 1, D_MODEL), lambda j, te, nu: (te[j], 0, 0)),
        ],
        out_specs=pl.BlockSpec((tm, D_MODEL), lambda j, te, nu: (j, 0)),
        scratch_shapes=[
            pltpu.VMEM((D_MODEL, 2 * D_FF), BF16),
            pltpu.VMEM((D_FF, D_MODEL), BF16),
        ],
    )
    return pl.pallas_call(
        _experts_kernel,
        grid_spec=grid_spec,
        out_shape=jax.ShapeDtypeStruct((r, D_MODEL), F32),
        compiler_params=_cparams(("arbitrary",)),
        name="experts",
    )(tile_expert, n_used, xs, w1, b1.reshape(N_EXPERTS, 1, 2 * D_FF), w2, b2.reshape(N_EXPERTS, 1, D_MODEL))


def _combine_kernel(pos_ref, ys_ref, x1_ref, wts_ref, lnf_ref, y_ref, rows_scr, sem):
    tc = x1_ref.shape[0]

    def issue(r, carry):
        for kk in range(TOP_K):
            pltpu.make_async_copy(ys_ref.at[pl.ds(pos_ref[0, kk, r], 1), :],
                                  rows_scr.at[kk, pl.ds(r, 1), :], sem).start()
        return carry

    lax.fori_loop(0, tc, issue, 0, unroll=2)
    for kk in range(TOP_K):
        pltpu.make_async_copy(ys_ref.at[pl.ds(0, tc), :], rows_scr.at[kk], sem).wait()
    w = wts_ref[...]
    acc = x1_ref[...]
    for kk in range(TOP_K):
        acc = acc + w[:, kk:kk + 1] * rows_scr[kk]
    y_ref[...] = acc * lax.rsqrt(jnp.mean(acc * acc, axis=-1, keepdims=True) + EPS) * lnf_ref[...]


def _combine(pos, ys, x1, wts, lnf):
    t = x1.shape[0]
    tc = COMB_TILE
    row = lambda i: (i, 0)
    pos3 = pos.reshape(t // tc, tc, TOP_K).transpose(0, 2, 1)
    return pl.pallas_call(
        _combine_kernel,
        grid=(t // tc,),
        in_specs=[
            pl.BlockSpec((1, TOP_K, tc), lambda i: (i, 0, 0), memory_space=pltpu.SMEM),
            pl.BlockSpec(memory_space=pl.ANY),
            pl.BlockSpec((tc, D_MODEL), row),
            pl.BlockSpec((tc, LANES), row),
            pl.BlockSpec((1, D_MODEL), lambda i: (0, 0)),
        ],
        out_specs=pl.BlockSpec((tc, D_MODEL), row),
        out_shape=jax.ShapeDtypeStruct((t, D_MODEL), F32),
        scratch_shapes=[pltpu.VMEM((TOP_K, tc, D_MODEL), F32), pltpu.SemaphoreType.DMA],
        compiler_params=_cparams(("arbitrary",)),
        name="combine",
    )(pos3, ys, x1, wts, lnf)


def _consts():
    r64 = np.zeros((D_MODEL, LANES), np.float32)
    e64 = np.zeros((LANES, D_MODEL), np.float32)
    for h in range(F_HEADS):
        r64[h * F_DH:(h + 1) * F_DH, h] = 1.0 / F_DH
        e64[h, h * F_DH:(h + 1) * F_DH] = 1.0
    return jnp.asarray(r64, BF16), jnp.asarray(e64, BF16), jnp.asarray(e64, F32)


def _tril(n, strict=False):
    return jnp.asarray(np.tril(np.ones((n, n), np.float32), -1 if strict else 0), BF16)


def _round_up(a, b):
    return (a + b - 1) // b * b


def kernel(x_prompt, x_sample, cache_k, cache_v, cache_logf, state_C, state_n, state_m, page_table, meta_tokens, ln1, w_in, mlstm_b_i, mlstm_b_f, mlstm_head_norm, fox_b_f, fox_q_norm, fox_k_norm, w_branch_mlstm, w_branch_fox, w_out, ln2, w_router, b_router, w_exp_in, b_exp_in, w_exp_out, b_exp_out, ln_final):
    depth = w_in.shape[0]
    b, seq, _ = x_prompt.shape
    db, ds, _ = x_sample.shape
    assert ds == 1 and seq % ATT_BLOCK == 0 and db % MSTEP_SEQS == 0
    l_true = seq + N_META
    lp = seq + ATT_BLOCK
    tp = b * lp
    n_phys, pg = cache_k.shape[1], cache_k.shape[2]
    r64, e64, e64f = _consts()

    xp = jnp.concatenate([jnp.zeros((b, PAD_FRONT, D_MODEL), F32),
                          jnp.broadcast_to(meta_tokens[None].astype(F32), (b, N_META, D_MODEL)),
                          x_prompt], axis=1).reshape(tp, D_MODEL)
    xs = x_sample.reshape(db, D_MODEL)

    outs = {k: [] for k in ("kp", "vp", "lfp", "ks", "vs", "lfs", "cp", "np", "mp", "cs", "ns", "ms")}
    offs = np.cumsum((0,) + (M_HEADS * M_DK, M_HEADS * M_DK, M_HEADS * M_DV, M_HEADS * M_DV, M_HEADS, M_HEADS,
                             F_HEADS * F_DH, F_HEADS * F_DH, F_HEADS * F_DH, F_HEADS, 2 * D_MODEL))
    seg = lambda w, i: w[:, offs[i]:offs[i + 1]]
    for l in range(depth):
        w = w_in[l]
        wmain = jnp.concatenate([seg(w, 0), seg(w, 1), seg(w, 2), seg(w, 3), seg(w, 6), seg(w, 7), seg(w, 8),
                                 seg(w, 10)], axis=1).astype(BF16)
        wsm = jnp.concatenate([seg(w, 4), seg(w, 5), seg(w, 9),
                               jnp.zeros((D_MODEL, LANES - G_END), F32)], axis=1)
        wsh = wsm.astype(BF16)
        wsl = (wsm - wsh.astype(F32)).astype(BF16)
        bias = jnp.concatenate([mlstm_b_i[l], mlstm_b_f[l], fox_b_f[l], jnp.zeros((LANES - G_END,), F32)])[None]
        qn = jnp.tile(fox_q_norm[l], F_HEADS)[None]
        kn = jnp.tile(fox_k_norm[l], F_HEADS)[None]
        hnw = mlstm_head_norm[l][None]
        ln1l = ln1[l][None]
        wbm = w_branch_mlstm[l].astype(BF16)
        wbf = w_branch_fox[l].astype(BF16)
        wo = w_out[l].astype(BF16)
        wr = jnp.concatenate([w_router[l], jnp.zeros((D_MODEL, LANES - N_EXPERTS), F32)], axis=1)
        wrh = wr.astype(BF16)
        wrl = (wr - wrh.astype(F32)).astype(BF16)
        br = jnp.concatenate([b_router[l], jnp.zeros((LANES - N_EXPERTS,), F32)])[None]

        zp, kpt32, vpt32, vpt16, gsp = _proj(xp, ln1l, wmain, wsh, wsl, bias, qn, kn, r64, e64, ROW_TILE)
        zp3 = zp.reshape(b, lp, N_ZB * D_MODEL)
        hnp, kap, qap, c_p, n_p, m_p = _mlstm_prompt(zp3, gsp.reshape(b, lp, LANES), hnw, _tril(M_CHUNK))
        hfp = _fox_prompt(zp3, qap, kap, vpt16)
        x1p, tpn, lgp = _merge(xp, hnp.reshape(tp, D_MODEL), zp, hfp.reshape(tp, D_MODEL), wbm, wbf, wo,
                               ln2[l][None], wrh, wrl, br, ROW_TILE)
        unpad = lambda a: jnp.transpose(a.reshape(F_HEADS, F_DH, b, lp)[:, :, :, PAD_FRONT:], (2, 3, 0, 1))
        outs["kp"].append(unpad(kpt32))
        outs["vp"].append(unpad(vpt32))
        outs["lfp"].append(gsp.reshape(b, lp, LANES)[:, PAD_FRONT:, G_FF:G_END])
        outs["cp"].append(c_p)
        outs["np"].append(n_p[:, :, 0, :])
        outs["mp"].append(m_p[:, :, 0, 0])

        zs, kst32, vst32, _, gss = _proj(xs, ln1l, wmain, wsh, wsl, bias, qn, kn, r64, e64, db)
        vs32 = vst32.T
        m_in = jnp.concatenate([state_m[l], jnp.zeros((db, LANES - M_HEADS), F32)], axis=1)
        hns, c_s, n_s, m_s = _mlstm_step(zs, gss, m_in, state_n[l].reshape(db, M_HEADS * M_DK), state_C[l], hnw)
        n_s = n_s.reshape(db, M_HEADS, M_DK)
        q3 = zs[:, ZB_FQ * D_MODEL:(ZB_FQ + 1) * D_MODEL].astype(F32).reshape(db, 1, D_MODEL)
        kn3 = zs[:, ZB_FK * D_MODEL:(ZB_FK + 1) * D_MODEL].astype(F32).reshape(db, 1, D_MODEL)
        hfs = _fox_decode(page_table, q3, kn3, vs32.reshape(db, 1, D_MODEL),
                          jnp.broadcast_to(gss[:, G_FF:G_END, None], (db, F_HEADS, LANES)),
                          jnp.transpose(cache_k[l], (0, 2, 3, 1)), jnp.transpose(cache_v[l], (0, 2, 3, 1)),
                          jnp.transpose(cache_logf[l], (0, 2, 1)))
        x1s, tsn, lgs = _merge(xs, hns, zs, hfs.reshape(db, D_MODEL).astype(BF16), wbm, wbf, wo,
                               ln2[l][None], wrh, wrl, br, db)
        outs["ks"].append(jnp.transpose(kst32.reshape(F_HEADS, F_DH, db), (2, 0, 1)).reshape(db, 1, F_HEADS, F_DH))
        outs["vs"].append(jnp.transpose(vst32.reshape(F_HEADS, F_DH, db), (2, 0, 1)).reshape(db, 1, F_HEADS, F_DH))
        outs["lfs"].append(gss[:, G_FF:G_END].reshape(db, 1, F_HEADS))
        outs["cs"].append(c_s)
        outs["ns"].append(n_s)
        outs["ms"].append(m_s[:, :M_HEADS])

        tt = tp + db
        tmoe = _round_up(tt, ROW_TILE)
        zpad = jnp.zeros((tmoe - tt, D_MODEL), F32)
        t_all = jnp.concatenate([tpn, tsn, zpad], axis=0)
        x1_all = jnp.concatenate([x1p, x1s, zpad], axis=0)
        lg_all = jnp.concatenate([lgp, lgs, jnp.zeros((tmoe - tt, LANES), F32)], axis=0)
        ids, wts, rank, cnt = _route(lg_all, _tril(ROW_TILE, strict=True), ROW_TILE)
        counts = cnt[0, :N_EXPERTS]
        pcounts = (counts + EXP_TILE - 1) // EXP_TILE * EXP_TILE
        ends = jnp.cumsum(pcounts)
        starts = ends - pcounts
        ids4 = ids[:, :TOP_K]
        pos = starts[ids4] + rank[:, :TOP_K]
        r_pad = _round_up(TOP_K * tmoe + N_EXPERTS * (EXP_TILE - 1), EXP_TILE)
        src = jnp.zeros((r_pad,), I32).at[pos.reshape(-1)].set(
            jnp.repeat(jnp.arange(tmoe, dtype=I32), TOP_K))
        n_tiles = r_pad // EXP_TILE
        tile_start = jnp.arange(n_tiles, dtype=I32) * EXP_TILE
        tile_expert = jnp.minimum(jnp.sum((ends[None, :] <= tile_start[:, None]).astype(I32), axis=1),
                                  N_EXPERTS - 1)
        n_used = (ends[-1] // EXP_TILE).astype(I32)[None]
        xsrt = _gather_rows(t_all, src, GATHER_TILE)
        ysrt = _experts(tile_expert, n_used, xsrt, w_exp_in[l], b_exp_in[l], w_exp_out[l], b_exp_out[l])
        lnf = ln_final[None] if l == depth - 1 else jnp.ones((1, D_MODEL), F32)
        y_all = _combine(pos, ysrt, x1_all, wts, lnf)
        if l < depth - 1:
            raise NotImplementedError("deeper stacks need the un-normalised residual stream as well")

    y_prompt = y_all[:tp].reshape(b, lp, D_MODEL)[:, ATT_BLOCK:]
    y_sample = y_all[tp:tp + db].reshape(db, 1, D_MODEL)
    st = lambda k: jnp.stack(outs[k])
    return (y_prompt, y_sample, st("kp"), st("vp"), st("lfp"), st("ks"), st("vs"), st("lfs"),
            st("cp"), st("np"), st("mp"), st("cs"), st("ns"), st("ms"))
```

```python
import functools

import numpy as np
import jax
import jax.numpy as jnp
from jax import lax
from jax.experimental import pallas as pl
from jax.experimental.pallas import tpu as pltpu

F32 = jnp.float32
BF16 = jnp.bfloat16
I32 = jnp.int32

D_MODEL = 1024
N_META = 16
M_HEADS, M_DK, M_DV = 4, 128, 256
F_HEADS, F_DH = 16, 64
N_EXPERTS, TOP_K, D_FF = 32, 4, 1024
GATE_CAP = 15.0
SWIGLU_LIMIT = 7.0
SWIGLU_ALPHA = 1.702
EPS = 1e-6
ATTN_SCALE = F_DH ** -0.5

LANES = 128
ATT_BLOCK = 256
ATT_SUB = 128
PAD_FRONT = ATT_BLOCK - N_META
M_CHUNK = 128
ROW_TILE = 512
EXP_TILE = 256
GATHER_TILE = 256
COMB_TILE = 128
VMEM_LIMIT = 56 * 1024 * 1024

G_IG = 0
G_LF = M_HEADS
G_FF = 2 * M_HEADS
G_END = G_FF + F_HEADS

WB_MQK, WB_MV, WB_MO, WB_FQ, WB_FK, WB_FV, WB_GA, WB_GB = range(8)
N_WB = 8
ZB_MQK, ZB_MV, ZB_MO, ZB_FQ, ZB_FK, ZB_GA, ZB_GB = range(7)
N_ZB = 7


def _dot(a, b):
    return jnp.dot(a, b, preferred_element_type=F32)


def _dot_nt(a, b):
    return lax.dot_general(a, b, (((1,), (1,)), ((), ())), preferred_element_type=F32)


def _dot_tn(a, b):
    return lax.dot_general(a, b, (((0,), (0,)), ((), ())), preferred_element_type=F32)


def _split2(x):
    hi = x.astype(BF16)
    lo = (x - hi.astype(F32)).astype(BF16)
    return hi, lo


def _split3(x):
    a = x.astype(BF16)
    r = x - a.astype(F32)
    b = r.astype(BF16)
    c = (r - b.astype(F32)).astype(BF16)
    return a, b, c


def _dot_exact_rhs01(parts, m01):
    acc = _dot(parts[0], m01)
    for p in parts[1:]:
        acc = acc + _dot(p, m01)
    return acc


def _log_sigmoid(x):
    return jnp.minimum(x, 0.0) - jnp.log1p(jnp.exp(-jnp.abs(x)))


def _cparams(sem):
    return pltpu.CompilerParams(dimension_semantics=sem, vmem_limit_bytes=VMEM_LIMIT)


def _head_norm(y, g, r_ref, e_ref):
    ms = _dot((y * y).astype(BF16), r_ref[...])
    rs = lax.rsqrt(ms + EPS)
    rsx = _dot_exact_rhs01(_split2(rs), e_ref[...])
    return y * rsx * g


def _proj_kernel(x_ref, ln_ref, w_ref, wsh_ref, wsl_ref, bias_ref, qn_ref, kn_ref, r_ref, e_ref,
                 z_ref, kt32_ref, vt32_ref, vt16_ref, gs_ref):
    x = x_ref[...]
    ms = jnp.mean(x * x, axis=-1, keepdims=True)
    h = x * lax.rsqrt(ms + EPS) * ln_ref[...]
    hh, hl = _split2(h)
    wsh = wsh_ref[...]
    g = _dot(hh, wsh) + _dot(hl, wsh) + _dot(hh, wsl_ref[...]) + bias_ref[...]
    lane = lax.broadcasted_iota(I32, g.shape, 1)
    cap = GATE_CAP * jnp.tanh(g / GATE_CAP)
    ls = _log_sigmoid(jnp.where(lane < G_FF, cap, g))
    gs_ref[...] = jnp.where(lane < G_LF, cap, jnp.where(lane < G_END, ls, 0.0))

    def y_of(wb):
        return _dot(hh, w_ref[wb])

    def put(zb, val):
        z_ref[:, zb * D_MODEL:(zb + 1) * D_MODEL] = val.astype(BF16)

    y = y_of(WB_MQK)
    col = lax.broadcasted_iota(I32, y.shape, 1)
    put(ZB_MQK, jnp.where(col < M_HEADS * M_DK, y * (M_DK ** -0.5), y))
    put(ZB_MV, y_of(WB_MV))
    put(ZB_MO, jax.nn.sigmoid(y_of(WB_MO)))
    put(ZB_FQ, _head_norm(y_of(WB_FQ), qn_ref[...], r_ref, e_ref) * ATTN_SCALE)
    kn = _head_norm(y_of(WB_FK), kn_ref[...], r_ref, e_ref)
    kt32_ref[...] = kn.T
    put(ZB_FK, kn)
    vt = y_of(WB_FV).T
    vt32_ref[...] = vt
    vt16_ref[...] = vt.astype(BF16)
    put(ZB_GA, jax.nn.sigmoid(y_of(WB_GA)))
    put(ZB_GB, jax.nn.sigmoid(y_of(WB_GB)))


def _proj(x, ln, wmain, wsh, wsl, bias, qn, kn, r64, e64, tm):
    t = x.shape[0]
    row = lambda i: (i, 0)
    col = lambda i: (0, i)
    fix = lambda i: (0, 0)
    return pl.pallas_call(
        _proj_kernel,
        grid=(t // tm,),
        in_specs=[
            pl.BlockSpec((tm, D_MODEL), row),
            pl.BlockSpec((1, D_MODEL), fix),
            pl.BlockSpec((N_WB, D_MODEL, D_MODEL), lambda i: (0, 0, 0), pipeline_mode=pl.Buffered(1)),
            pl.BlockSpec((D_MODEL, LANES), fix),
            pl.BlockSpec((D_MODEL, LANES), fix),
            pl.BlockSpec((1, LANES), fix),
            pl.BlockSpec((1, D_MODEL), fix),
            pl.BlockSpec((1, D_MODEL), fix),
            pl.BlockSpec((D_MODEL, LANES), fix),
            pl.BlockSpec((LANES, D_MODEL), fix),
        ],
        out_specs=[
            pl.BlockSpec((tm, N_ZB * D_MODEL), row),
            pl.BlockSpec((D_MODEL, tm), col),
            pl.BlockSpec((D_MODEL, tm), col),
            pl.BlockSpec((D_MODEL, tm), col),
            pl.BlockSpec((tm, LANES), row),
        ],
        out_shape=[
            jax.ShapeDtypeStruct((t, N_ZB * D_MODEL), BF16),
            jax.ShapeDtypeStruct((D_MODEL, t), F32),
            jax.ShapeDtypeStruct((D_MODEL, t), F32),
            jax.ShapeDtypeStruct((D_MODEL, t), BF16),
            jax.ShapeDtypeStruct((t, LANES), F32),
        ],
        compiler_params=_cparams(("parallel",)),
        name="proj",
    )(x, ln, wmain, wsh, wsl, bias, qn, kn, r64, e64)


def _mlstm_kernel(qk_ref, v_ref, gs_ref, hnw_ref, tril_ref, place_ref,
                  hn_ref, ka_ref, qa_ref, c_out, n_out, m_out,
                  c_scr, n_scr, m_scr, crow_scr):
    c = pl.program_id(1)
    lc = M_CHUNK

    @pl.when(c == 0)
    def _():
        c_scr[...] = jnp.zeros_like(c_scr)
        n_scr[...] = jnp.zeros_like(n_scr)
        m_scr[...] = jnp.zeros_like(m_scr)
        crow_scr[...] = jnp.zeros_like(crow_scr)

    g = gs_ref[0]
    lane = lax.broadcasted_iota(I32, (lc, LANES), 1)
    pos = c * lc + lax.broadcasted_iota(I32, (lc, LANES), 0)
    valid = pos >= PAD_FRONT
    is_ig = lane < G_LF
    gsum = jnp.where(valid & jnp.logical_not(is_ig), g, 0.0)
    bcum = _dot_exact_rhs01_lhs(tril_ref[...], _split3(gsum))
    comb = jnp.where(is_ig, jnp.where(valid, g, -jnp.inf), bcum)
    comb_t = comb.T

    row_t = lax.broadcasted_iota(I32, (lc, lc), 0)
    col_s = lax.broadcasted_iota(I32, (lc, lc), 1)
    causal = col_s <= row_t

    qk = qk_ref[0]
    vv = v_ref[0]
    for h in range(M_HEADS):
        q = qk[:, h * M_DK:(h + 1) * M_DK]
        k = qk[:, (M_HEADS + h) * M_DK:(M_HEADS + h + 1) * M_DK]
        v = vv[:, h * M_DV:(h + 1) * M_DV]
        m_prev = m_scr[h, 0:1, 0:1]
        b_col = bcum[:, G_LF + h:G_LF + h + 1]
        ig_col = comb[:, G_IG + h:G_IG + h + 1]
        b_row = comb_t[G_LF + h:G_LF + h + 1, :]
        ig_row = comb_t[G_IG + h:G_IG + h + 1, :]

        log_d = jnp.where(causal, b_col - b_row + ig_row, -jnp.inf)
        inter = b_col + m_prev
        m_row = jnp.maximum(inter, jnp.max(log_d, axis=1, keepdims=True))
        s = _dot_nt(q, k) * jnp.exp(log_d - m_row)
        w_int = jnp.exp(inter - m_row)
        cmat = c_scr[h]
        nvec = n_scr[h, 0:1, :]
        num = _dot(s.astype(BF16), v) + w_int * _dot_nt(q, cmat.astype(BF16))
        den = jnp.sum(s, axis=1, keepdims=True) + w_int * jnp.sum(q.astype(F32) * nvec, axis=1, keepdims=True)
        hh = num / jnp.maximum(jnp.abs(den), jnp.exp(-m_row))
        hn = hh * lax.rsqrt(jnp.mean(hh * hh, axis=1, keepdims=True) + EPS) * hnw_ref[:, h * M_DV:(h + 1) * M_DV]
        hn_ref[0, :, h * M_DV:(h + 1) * M_DV] = hn.astype(BF16)

        b_last = b_col[lc - 1:lc, :]
        m_new = jnp.maximum(b_last + m_prev, jnp.max(b_last - b_row + ig_row, axis=1, keepdims=True))
        decay = jnp.exp(b_last + m_prev - m_new)
        ws_col = jnp.exp(b_last - b_col + ig_col - m_new)
        kw = k.astype(F32) * ws_col
        c_scr[h] = decay * cmat + _dot_tn(v, kw.astype(BF16))
        n_scr[h] = jnp.broadcast_to(decay * nvec + jnp.sum(kw, axis=0, keepdims=True), n_scr.shape[1:])
        m_scr[h] = jnp.broadcast_to(m_new, m_scr.shape[1:])

    cg = bcum + crow_scr[0:1, :]
    crow_scr[...] = jnp.broadcast_to(cg[lc - 1:lc, :], crow_scr.shape)
    p0, p1, p2 = _split3(cg)
    p0 = jnp.where(lane == 0, jnp.ones_like(p0), p0)
    ka_ref[0] = (_dot(p0, place_ref[0]) + _dot(p1, place_ref[1]) + _dot(p2, place_ref[2])).astype(BF16)
    qa_ref[0] = (_dot(p0, place_ref[3]) + _dot(p1, place_ref[4]) + _dot(p2, place_ref[5])).astype(BF16)

    @pl.when(c == pl.num_programs(1) - 1)
    def _():
        c_out[0] = c_scr[...]
        n_out[0] = n_scr[...]
        m_out[0] = m_scr[...]


def _dot_exact_rhs01_lhs(m01, parts):
    acc = _dot(m01, parts[0])
    for p in parts[1:]:
        acc = acc + _dot(m01, p)
    return acc


BIAS_LANES = 6


def _bias_placement():
    pm = np.zeros((6, LANES, LANES), np.float32)
    for h in range(F_HEADS):
        base = BIAS_LANES * h
        for t in range(3):
            pm[t, G_FF + h, base + t] = -1.0
            pm[0, 0, base + 3 + t] = 1.0
            pm[3 + t, G_FF + h, base + 3 + t] = 1.0
            pm[3, 0, base + t] = 1.0
    return jnp.asarray(pm, BF16)


def _mlstm_prompt(z3, gs3, hnw, tril):
    b, lp, _ = z3.shape
    lc = M_CHUNK
    nc = lp // lc
    return pl.pallas_call(
        _mlstm_kernel,
        grid=(b, nc),
        in_specs=[
            pl.BlockSpec((1, lc, D_MODEL), lambda i, c: (i, c, ZB_MQK)),
            pl.BlockSpec((1, lc, D_MODEL), lambda i, c: (i, c, ZB_MV)),
            pl.BlockSpec((1, lc, LANES), lambda i, c: (i, c, 0)),
            pl.BlockSpec((1, D_MODEL), lambda i, c: (0, 0)),
            pl.BlockSpec((lc, lc), lambda i, c: (0, 0)),
            pl.BlockSpec((6, LANES, LANES), lambda i, c: (0, 0, 0)),
        ],
        out_specs=[
            pl.BlockSpec((1, lc, D_MODEL), lambda i, c: (i, c, 0)),
            pl.BlockSpec((1, lc, LANES), lambda i, c: (i, c, 0)),
            pl.BlockSpec((1, lc, LANES), lambda i, c: (i, c, 0)),
            pl.BlockSpec((1, M_HEADS, M_DV, M_DK), lambda i, c: (i, 0, 0, 0)),
            pl.BlockSpec((1, M_HEADS, 8, M_DK), lambda i, c: (i, 0, 0, 0)),
            pl.BlockSpec((1, M_HEADS, 8, LANES), lambda i, c: (i, 0, 0, 0)),
        ],
        out_shape=[
            jax.ShapeDtypeStruct((b, lp, D_MODEL), BF16),
            jax.ShapeDtypeStruct((b, lp, LANES), BF16),
            jax.ShapeDtypeStruct((b, lp, LANES), BF16),
            jax.ShapeDtypeStruct((b, M_HEADS, M_DV, M_DK), F32),
            jax.ShapeDtypeStruct((b, M_HEADS, 8, M_DK), F32),
            jax.ShapeDtypeStruct((b, M_HEADS, 8, LANES), F32),
        ],
        scratch_shapes=[
            pltpu.VMEM((M_HEADS, M_DV, M_DK), F32),
            pltpu.VMEM((M_HEADS, 8, M_DK), F32),
            pltpu.VMEM((M_HEADS, 8, LANES), F32),
            pltpu.VMEM((8, LANES), F32),
        ],
        compiler_params=_cparams(("parallel", "arbitrary")),
        name="mlstm",
    )(z3, z3, gs3, hnw, tril, _bias_placement())


MSTEP_SEQS = 8


def _mstep_kernel(qk_ref, v_ref, gs_ref, m_ref, n_ref, c_ref, hnw_ref,
                  hn_ref, c_out, n_out, m_out):
    sb = MSTEP_SEQS
    qk = qk_ref[...]
    vv = v_ref[...]
    g = gs_ref[...]
    row = lax.broadcasted_iota(I32, (sb, 1), 0)
    m_new_all = jnp.zeros((sb, LANES), F32)
    lane = lax.broadcasted_iota(I32, (sb, LANES), 1)
    for h in range(M_HEADS):
        q = qk[:, h * M_DK:(h + 1) * M_DK]
        k = qk[:, (M_HEADS + h) * M_DK:(M_HEADS + h + 1) * M_DK]
        v = vv[:, h * M_DV:(h + 1) * M_DV]
        ig = g[:, G_IG + h:G_IG + h + 1]
        lf = g[:, G_LF + h:G_LF + h + 1]
        m_prev = m_ref[:, h:h + 1]
        inter = lf + m_prev
        m_row = jnp.maximum(inter, ig)
        d = jnp.exp(ig - m_row)
        w_int = jnp.exp(inter - m_row)
        qf = q.astype(F32)
        kf = k.astype(F32)
        s = jnp.sum(qf * kf, axis=1, keepdims=True) * d
        nvec = n_ref[:, h * M_DK:(h + 1) * M_DK]
        den = s + w_int * jnp.sum(qf * nvec, axis=1, keepdims=True)
        kw = (kf * d).astype(BF16)
        hrows = []
        for i in range(sb):
            cmat = c_ref[i, h]
            sel = row == i
            qi = jnp.where(sel, q, jnp.zeros_like(q))
            qc = _dot_nt(qi, cmat.astype(BF16))
            hrows.append(jnp.where(sel, qc, 0.0))
            vi = jnp.where(sel, v, jnp.zeros_like(v))
            c_out[i, h] = w_int[i:i + 1, :] * cmat + _dot_tn(vi, kw)
        qc_all = functools.reduce(lambda a, b: a + b, hrows)
        num = s * v.astype(F32) + w_int * qc_all
        hh = num / jnp.maximum(jnp.abs(den), jnp.exp(-m_row))
        hn = hh * lax.rsqrt(jnp.mean(hh * hh, axis=1, keepdims=True) + EPS) * hnw_ref[:, h * M_DV:(h + 1) * M_DV]
        hn_ref[:, h * M_DV:(h + 1) * M_DV] = hn.astype(BF16)
        n_out[:, h * M_DK:(h + 1) * M_DK] = w_int * nvec + kf * d
        m_new_all = jnp.where(lane == h, m_row, m_new_all)
    m_out[...] = m_new_all


def _mlstm_step(zs, gss, m_in, n_in, c_in, hnw):
    db = zs.shape[0]
    sb = MSTEP_SEQS
    return pl.pallas_call(
        _mstep_kernel,
        grid=(db // sb,),
        in_specs=[
            pl.BlockSpec((sb, D_MODEL), lambda i: (i, ZB_MQK)),
            pl.BlockSpec((sb, D_MODEL), lambda i: (i, ZB_MV)),
            pl.BlockSpec((sb, LANES), lambda i: (i, 0)),
            pl.BlockSpec((sb, LANES), lambda i: (i, 0)),
            pl.BlockSpec((sb, M_HEADS * M_DK), lambda i: (i, 0)),
            pl.BlockSpec((sb, M_HEADS, M_DV, M_DK), lambda i: (i, 0, 0, 0)),
            pl.BlockSpec((1, D_MODEL), lambda i: (0, 0)),
        ],
        out_specs=[
            pl.BlockSpec((sb, D_MODEL), lambda i: (i, 0)),
            pl.BlockSpec((sb, M_HEADS, M_DV, M_DK), lambda i: (i, 0, 0, 0)),
            pl.BlockSpec((sb, M_HEADS * M_DK), lambda i: (i, 0)),
            pl.BlockSpec((sb, LANES), lambda i: (i, 0)),
        ],
        out_shape=[
            jax.ShapeDtypeStruct((db, D_MODEL), BF16),
            jax.ShapeDtypeStruct((db, M_HEADS, M_DV, M_DK), F32),
            jax.ShapeDtypeStruct((db, M_HEADS * M_DK), F32),
            jax.ShapeDtypeStruct((db, LANES), F32),
        ],
        compiler_params=_cparams(("parallel",)),
        name="mstep",
    )(zs, zs, gss, m_in, n_in, c_in, hnw)


def _fox_kernel(qi_tab, ki_tab, q_ref, qa_ref, k_ref, ka_ref, vt_ref, o_ref,
                qb_scr, acc_scr, m_scr, l_scr, a_scr, st_scr, pt_scr):
    step = pl.program_id(1)
    qi = qi_tab[step]
    ki = ki_tab[step]
    tq = tk = ATT_BLOCK

    @pl.when(ki == 0)
    def _():
        acc_scr[...] = jnp.zeros_like(acc_scr)
        m_scr[...] = jnp.full_like(m_scr, -jnp.inf)
        l_scr[...] = jnp.zeros_like(l_scr)
        lane = lax.broadcasted_iota(I32, (tq, LANES), 1)
        qa = qa_ref[0]
        for h in range(F_HEADS):
            q2 = q_ref[0, :, (h // 2) * LANES:(h // 2 + 1) * LANES]
            own = (lane >= (h % 2) * F_DH) & (lane < (h % 2 + 1) * F_DH)
            bias = (lane >= h * BIAS_LANES) & (lane < (h + 1) * BIAS_LANES)
            qb_scr[h] = jnp.concatenate([jnp.where(own, q2, jnp.zeros_like(q2)),
                                         jnp.where(bias, qa, jnp.zeros_like(qa))], axis=1)

    def sweep(masked):
        ka = ka_ref[0]
        if masked:
            key = ki * tk + lax.broadcasted_iota(I32, (tk, tq), 0)
            qry = qi * tq + lax.broadcasted_iota(I32, (tk, tq), 1)
            visible = (key <= qry) & ((key >= PAD_FRONT) | (qry < PAD_FRONT))
        for p in range(F_HEADS // 2):
            kb = jnp.concatenate([k_ref[0, :, p * LANES:(p + 1) * LANES], ka], axis=1)
            for h in (2 * p, 2 * p + 1):
                st_scr[h] = _dot_nt(kb, qb_scr[h])
        for h in range(F_HEADS):
            st = st_scr[h]
            if masked:
                st = jnp.where(visible, st, -jnp.inf)
            m_prev = m_scr[h, 0:1, :]
            m_next = jnp.maximum(m_prev, jnp.max(st, axis=0, keepdims=True))
            alpha = jnp.exp(m_prev - m_next)
            pt = jnp.exp(st - m_next)
            l_scr[h] = jnp.broadcast_to(alpha * l_scr[h, 0:1, :] + jnp.sum(pt, axis=0, keepdims=True), (8, tq))
            m_scr[h] = jnp.broadcast_to(m_next, (8, tq))
            a_scr[h] = jnp.broadcast_to(alpha, (8, tq))
            pt_scr[h] = pt.astype(BF16)
        for h in range(F_HEADS):
            acc_scr[h] = a_scr[h, 0:1, :] * acc_scr[h] + _dot(vt_ref[h * F_DH:(h + 1) * F_DH, :], pt_scr[h])

    edge = (ki == qi) | (ki == 0)

    @pl.when(edge)
    def _():
        sweep(True)

    @pl.when(jnp.logical_not(edge))
    def _():
        sweep(False)

    @pl.when(ki == qi)
    def _():
        for p in range(F_HEADS // 2):
            o2 = jnp.concatenate([acc_scr[2 * p] * (1.0 / l_scr[2 * p, 0:1, :]),
                                  acc_scr[2 * p + 1] * (1.0 / l_scr[2 * p + 1, 0:1, :])], axis=0)
            o_ref[0, :, p * LANES:(p + 1) * LANES] = o2.T.astype(BF16)


def _fox_prompt(z3, qa, ka, vt16):
    b, lp, _ = z3.shape
    blk = ATT_BLOCK
    nb = lp // blk
    qi_tab = np.concatenate([np.full((i + 1,), i, np.int32) for i in range(nb)])
    ki_tab = np.concatenate([np.arange(i + 1, dtype=np.int32) for i in range(nb)])
    grid_spec = pltpu.PrefetchScalarGridSpec(
        num_scalar_prefetch=2,
        grid=(b, len(qi_tab)),
        in_specs=[
            pl.BlockSpec((1, blk, D_MODEL), lambda i, s, qt, kt: (i, qt[s], ZB_FQ)),
            pl.BlockSpec((1, blk, LANES), lambda i, s, qt, kt: (i, qt[s], 0)),
            pl.BlockSpec((1, blk, D_MODEL), lambda i, s, qt, kt: (i, kt[s], ZB_FK)),
            pl.BlockSpec((1, blk, LANES), lambda i, s, qt, kt: (i, kt[s], 0)),
            pl.BlockSpec((D_MODEL, blk), lambda i, s, qt, kt: (0, i * nb + kt[s])),
        ],
        out_specs=pl.BlockSpec((1, blk, D_MODEL), lambda i, s, qt, kt: (i, qt[s], 0)),
        scratch_shapes=[
            pltpu.VMEM((F_HEADS, blk, 2 * LANES), BF16),
            pltpu.VMEM((F_HEADS, F_DH, blk), F32),
            pltpu.VMEM((F_HEADS, 8, blk), F32),
            pltpu.VMEM((F_HEADS, 8, blk), F32),
            pltpu.VMEM((F_HEADS, 8, blk), F32),
            pltpu.VMEM((F_HEADS, blk, blk), F32),
            pltpu.VMEM((F_HEADS, blk, blk), BF16),
        ],
    )
    return pl.pallas_call(
        _fox_kernel,
        grid_spec=grid_spec,
        out_shape=jax.ShapeDtypeStruct((b, lp, D_MODEL), BF16),
        compiler_params=_cparams(("parallel", "arbitrary")),
        name="fox",
    )(jnp.asarray(qi_tab), jnp.asarray(ki_tab), z3, qa, z3, ka, vt16)


DEC_PAGES = 8


def _lane_bcast_cols(row):
    full = jnp.broadcast_to(row, (LANES, D_MODEL)).T
    return full.reshape(F_HEADS, F_DH, LANES)


def _per_head(x):
    return x.reshape(F_HEADS, 1, LANES)


def _dec_kernel(pt_ref, q_ref, kn_ref, vn_ref, lfn_ref, triu_ref, ones_ref, *rest):
    np_ = DEC_PAGES
    k_refs, v_refs, lf_refs = rest[:np_], rest[np_:2 * np_], rest[2 * np_:3 * np_]
    o_ref, qb_scr, acc_scr, m_scr, l_scr, c_scr = rest[3 * np_:]
    p = pl.program_id(1)

    @pl.when(p == 0)
    def _():
        qb_scr[...] = _lane_bcast_cols(q_ref[0])
        acc_scr[...] = jnp.zeros_like(acc_scr)
        m_scr[...] = jnp.full_like(m_scr, -jnp.inf)
        l_scr[...] = jnp.zeros_like(l_scr)
        c_scr[...] = jnp.zeros_like(c_scr)

    qb = qb_scr[...]
    carry = c_scr[...]
    us = []
    for i in range(np_):
        cum = _dot_exact_rhs01(_split3(lf_refs[i][0]), triu_ref[...]) + carry
        carry = jnp.broadcast_to(cum[:, LANES - 1:LANES], carry.shape)
        s = jnp.sum(qb * k_refs[i][0], axis=1)
        us.append(s - cum)
    c_scr[...] = carry
    m_prev = m_scr[...]
    m_cur = functools.reduce(jnp.maximum, [jnp.max(u, axis=1, keepdims=True) for u in us])
    m_next = jnp.maximum(m_prev, m_cur)
    alpha = jnp.exp(m_prev - m_next)
    acc = _per_head(alpha) * acc_scr[...]
    l_new = alpha * l_scr[...]
    for i in range(np_):
        pr = jnp.exp(us[i] - m_next)
        l_new = l_new + jnp.sum(pr, axis=1, keepdims=True)
        acc = acc + _per_head(pr) * v_refs[i][0]
    acc_scr[...] = acc
    l_scr[...] = l_new
    m_scr[...] = m_next

    @pl.when(p == pl.num_programs(1) - 1)
    def _():
        s_new = jnp.sum(qb * _lane_bcast_cols(kn_ref[0]), axis=1)
        u_new = s_new - (carry + lfn_ref[0])
        m_fin = jnp.maximum(m_next, u_new)
        a_fin = jnp.exp(m_next - m_fin)
        pn = jnp.exp(u_new - m_fin)
        l_fin = a_fin * l_new + pn
        tot = (_per_head(a_fin) * acc + _per_head(pn * (1.0 / LANES)) * _lane_bcast_cols(vn_ref[0])) / _per_head(l_fin)
        out8 = _dot_nt_exact_lhs01(ones_ref[...], _split3(tot.reshape(D_MODEL, LANES)))
        o_ref[0] = out8[0:1, :]


def _dot_nt_exact_lhs01(m01, parts):
    acc = _dot_nt(m01, parts[0])
    for part in parts[1:]:
        acc = acc + _dot_nt(m01, part)
    return acc


def _fox_decode(page_table, q3, kn3, vn3, lfn3, cache_kt, cache_vt, cache_lft):
    db, npg = page_table.shape
    pg = cache_kt.shape[3]
    assert pg == LANES and npg % DEC_PAGES == 0
    row3 = lambda s, p, pt: (s, 0, 0)
    fix = lambda s, p, pt: (0, 0)

    def page(i, nd):
        return lambda s, p, pt: (pt[s, p * DEC_PAGES + i],) + (0,) * nd

    in_specs = [
        pl.BlockSpec((1, 1, D_MODEL), row3),
        pl.BlockSpec((1, 1, D_MODEL), row3),
        pl.BlockSpec((1, 1, D_MODEL), row3),
        pl.BlockSpec((1, F_HEADS, LANES), row3),
        pl.BlockSpec((LANES, LANES), fix),
        pl.BlockSpec((8, LANES), fix),
    ]
    in_specs += [pl.BlockSpec((1, F_HEADS, F_DH, LANES), page(i, 3)) for i in range(DEC_PAGES)]
    in_specs += [pl.BlockSpec((1, F_HEADS, F_DH, LANES), page(i, 3)) for i in range(DEC_PAGES)]
    in_specs += [pl.BlockSpec((1, F_HEADS, LANES), page(i, 2)) for i in range(DEC_PAGES)]
    grid_spec = pltpu.PrefetchScalarGridSpec(
        num_scalar_prefetch=1,
        grid=(db, npg // DEC_PAGES),
        in_specs=in_specs,
        out_specs=pl.BlockSpec((1, 1, D_MODEL), row3),
        scratch_shapes=[
            pltpu.VMEM((F_HEADS, F_DH, LANES), F32),
            pltpu.VMEM((F_HEADS, F_DH, LANES), F32),
            pltpu.VMEM((F_HEADS, LANES), F32),
            pltpu.VMEM((F_HEADS, LANES), F32),
            pltpu.VMEM((F_HEADS, LANES), F32),
        ],
    )
    triu = jnp.asarray(np.triu(np.ones((LANES, LANES), np.float32)), BF16)
    ones = jnp.ones((8, LANES), BF16)
    return pl.pallas_call(
        _dec_kernel,
        grid_spec=grid_spec,
        out_shape=jax.ShapeDtypeStruct((db, 1, D_MODEL), F32),
        compiler_params=_cparams(("parallel", "arbitrary")),
        name="dec",
    )(page_table, q3, kn3, vn3, lfn3, triu, ones,
      *([cache_kt] * DEC_PAGES), *([cache_vt] * DEC_PAGES), *([cache_lft] * DEC_PAGES))


TOKEN_SUBLANES = D_MODEL // LANES


def _store_token_tiles(ref, x):
    n = x.shape[0]
    for g in range(TOKEN_SUBLANES):
        ref[pl.ds(g, n, stride=TOKEN_SUBLANES), :] = x[:, g * LANES:(g + 1) * LANES]


def _load_token_tiles(ref, n):
    return jnp.concatenate([ref[pl.ds(g, n, stride=TOKEN_SUBLANES), :] for g in range(TOKEN_SUBLANES)], axis=1)


def _merge_kernel(x_ref, hn_ref, so_ref, hf_ref, ga_ref, gb_ref, wbm_ref, wbf_ref, wo_ref, ln2_ref,
                  wrh_ref, wrl_ref, br_ref, x1_ref, t_ref, lg_ref):
    hm = hn_ref[...] * so_ref[...]
    ya = _dot(hm, wbm_ref[...])
    yb = _dot(hf_ref[...], wbf_ref[...])
    u = ga_ref[...].astype(F32) * ya + gb_ref[...].astype(F32) * yb
    x1 = x_ref[...] + _dot(u.astype(BF16), wo_ref[...])
    x1_ref[...] = x1
    t = x1 * lax.rsqrt(jnp.mean(x1 * x1, axis=-1, keepdims=True) + EPS) * ln2_ref[...]
    _store_token_tiles(t_ref, t)
    th, tl = _split2(t)
    wrh = wrh_ref[...]
    lg_ref[...] = _dot(th, wrh) + _dot(tl, wrh) + _dot(th, wrl_ref[...]) + br_ref[...]


def _merge(x, hn, z, hf, wbm, wbf, wo, ln2, wrh, wrl, br, tm):
    t = x.shape[0]
    row = lambda i: (i, 0)
    fix = lambda i: (0, 0)
    full = pl.BlockSpec((D_MODEL, D_MODEL), fix)
    return pl.pallas_call(
        _merge_kernel,
        grid=(t // tm,),
        in_specs=[
            pl.BlockSpec((tm, D_MODEL), row),
            pl.BlockSpec((tm, D_MODEL), row),
            pl.BlockSpec((tm, D_MODEL), lambda i: (i, ZB_MO)),
            pl.BlockSpec((tm, D_MODEL), row),
            pl.BlockSpec((tm, D_MODEL), lambda i: (i, ZB_GA)),
            pl.BlockSpec((tm, D_MODEL), lambda i: (i, ZB_GB)),
            full, full, full,
            pl.BlockSpec((1, D_MODEL), fix),
            pl.BlockSpec((D_MODEL, LANES), fix),
            pl.BlockSpec((D_MODEL, LANES), fix),
            pl.BlockSpec((1, LANES), fix),
        ],
        out_specs=[
            pl.BlockSpec((tm, D_MODEL), row),
            pl.BlockSpec((tm * TOKEN_SUBLANES, LANES), row),
            pl.BlockSpec((tm, LANES), row),
        ],
        out_shape=[
            jax.ShapeDtypeStruct((t, D_MODEL), F32),
            jax.ShapeDtypeStruct((t * TOKEN_SUBLANES, LANES), F32),
            jax.ShapeDtypeStruct((t, LANES), F32),
        ],
        compiler_params=_cparams(("parallel",)),
        name="merge",
    )(x, hn, z, hf, z, z, wbm, wbf, wo, ln2, wrh, wrl, br)


def _route_kernel(lg_ref, tril_ref, ids_ref, wts_ref, rank_ref, cnt_ref, carry_scr):
    i = pl.program_id(0)

    @pl.when(i == 0)
    def _():
        carry_scr[...] = jnp.zeros_like(carry_scr)

    tm = lg_ref.shape[0]
    lane = lax.broadcasted_iota(I32, (tm, LANES), 1)
    lanef = lane.astype(F32)
    lg = jnp.where(lane < N_EXPERTS, lg_ref[...], -jnp.inf)
    vals, idxs, hots = [], [], []
    for _ in range(TOP_K):
        mx = jnp.max(lg, axis=1, keepdims=True)
        idx = jnp.min(jnp.where(lg == mx, lanef, float(LANES)), axis=1, keepdims=True)
        hot = lanef == idx
        lg = jnp.where(hot, -jnp.inf, lg)
        vals.append(mx)
        idxs.append(idx)
        hots.append(hot)
    es = [jnp.exp(v - vals[0]) for v in vals]
    tot = functools.reduce(lambda a, b: a + b, es)
    sel = functools.reduce(lambda a, b: a | b, hots)
    a01 = jnp.where(sel, 1.0, 0.0)
    before = _dot(tril_ref[...], a01.astype(BF16)) + carry_scr[0:1, :]
    carry_scr[...] = jnp.broadcast_to(carry_scr[0:1, :] + jnp.sum(a01, axis=0, keepdims=True), carry_scr.shape)
    ids = jnp.zeros((tm, LANES), I32)
    wts = jnp.zeros((tm, LANES), F32)
    rank = jnp.zeros((tm, LANES), I32)
    for kk in range(TOP_K):
        r = jnp.sum(jnp.where(hots[kk], before, 0.0), axis=1, keepdims=True)
        ids = jnp.where(lane == kk, idxs[kk].astype(I32), ids)
        wts = jnp.where(lane == kk, es[kk] / tot, wts)
        rank = jnp.where(lane == kk, r.astype(I32), rank)
    ids_ref[...] = ids
    wts_ref[...] = wts
    rank_ref[...] = rank
    cnt_ref[...] = carry_scr[...].astype(I32)


def _route(logits, tril_strict, tm):
    t = logits.shape[0]
    row = lambda i: (i, 0)
    return pl.pallas_call(
        _route_kernel,
        grid=(t // tm,),
        in_specs=[pl.BlockSpec((tm, LANES), row), pl.BlockSpec((tm, tm), lambda i: (0, 0))],
        out_specs=[pl.BlockSpec((tm, LANES), row)] * 3 + [pl.BlockSpec((8, LANES), lambda i: (0, 0))],
        out_shape=[
            jax.ShapeDtypeStruct((t, LANES), I32),
            jax.ShapeDtypeStruct((t, LANES), F32),
            jax.ShapeDtypeStruct((t, LANES), I32),
            jax.ShapeDtypeStruct((8, LANES), I32),
        ],
        scratch_shapes=[pltpu.VMEM((8, LANES), F32)],
        compiler_params=_cparams(("arbitrary",)),
        name="route",
    )(logits, tril_strict)


def _tile_rows(i):
    return pl.ds(pl.multiple_of(i * TOKEN_SUBLANES, TOKEN_SUBLANES), TOKEN_SUBLANES)


def _gather_kernel(idx_ref, src_ref, out_ref, sem):
    n = out_ref.shape[0] // TOKEN_SUBLANES

    def issue(r, carry):
        pltpu.make_async_copy(src_ref.at[_tile_rows(idx_ref[0, 0, r]), :], out_ref.at[_tile_rows(r), :], sem).start()
        return carry

    lax.fori_loop(0, n, issue, 0, unroll=8)
    pltpu.make_async_copy(src_ref.at[pl.ds(0, n * TOKEN_SUBLANES), :], out_ref, sem).wait()


def _gather_rows(src, idx, tile):
    n = idx.shape[0]
    return pl.pallas_call(
        _gather_kernel,
        grid=(n // tile,),
        in_specs=[
            pl.BlockSpec((1, 1, tile), lambda i: (i, 0, 0), memory_space=pltpu.SMEM),
            pl.BlockSpec(memory_space=pl.ANY),
        ],
        out_specs=pl.BlockSpec((tile * TOKEN_SUBLANES, LANES), lambda i: (i, 0)),
        out_shape=jax.ShapeDtypeStruct((n * TOKEN_SUBLANES, LANES), src.dtype),
        scratch_shapes=[pltpu.SemaphoreType.DMA],
        compiler_params=_cparams(("arbitrary",)),
        name="gather",
    )(idx.reshape(n // tile, 1, tile), src)


def _experts_kernel(te_ref, nu_ref, xs_ref, w1_ref, b1_ref, w2_ref, b2_ref, ys_ref, w1b_scr, w2b_scr):
    j = pl.program_id(0)
    prev = te_ref[jnp.maximum(j - 1, 0)]

    @pl.when((j == 0) | (te_ref[j] != prev))
    def _():
        w1b_scr[...] = w1_ref[0].astype(BF16)
        w2b_scr[...] = w2_ref[0].astype(BF16)

    @pl.when(j < nu_ref[0])
    def _():
        x = _load_token_tiles(xs_ref, EXP_TILE)
        hcat = _dot(x.astype(BF16), w1b_scr[...]) + b1_ref[0]
        g = jnp.minimum(hcat[:, :D_FF], SWIGLU_LIMIT)
        u = jnp.clip(hcat[:, D_FF:], -SWIGLU_LIMIT, SWIGLU_LIMIT)
        a = g * jax.nn.sigmoid(SWIGLU_ALPHA * g) * (u + 1.0)
        _store_token_tiles(ys_ref, _dot(a.astype(BF16), w2b_scr[...]) + b2_ref[0])

    @pl.when(j >= nu_ref[0])
    def _():
        ys_ref[...] = jnp.zeros_like(ys_ref)


def _experts(tile_expert, n_used, xs, w1, b1, w2, b2):
    r = xs.shape[0] // TOKEN_SUBLANES
    tm = EXP_TILE
    grid_spec = pltpu.PrefetchScalarGridSpec(
        num_scalar_prefetch=2,
        grid=(r // tm,),
        in_specs=[
            pl.BlockSpec((tm * TOKEN_SUBLANES, LANES), lambda j, te, nu: (j, 0)),
            pl.BlockSpec((1, D_MODEL, 2 * D_FF), lambda j, te, nu: (te[j], 0, 0)),
            pl.BlockSpec((1, 1, 2 * D_FF), lambda j, te, nu: (te[j], 0, 0)),
            pl.BlockSpec((1, D_FF, D_MODEL), lambda j, te, nu: (te[j], 0, 0)),
            pl.BlockSpec((1, 1, D_MODEL), lambda j, te, nu: (te[j], 0, 0)),
        ],
        out_specs=pl.BlockSpec((tm * TOKEN_SUBLANES, LANES), lambda j, te, nu: (j, 0)),
        scratch_shapes=[
            pltpu.VMEM((D_MODEL, 2 * D_FF), BF16),
            pltpu.VMEM((D_FF, D_MODEL), BF16),
        ],
    )
    return pl.pallas_call(
        _experts_kernel,
        grid_spec=grid_spec,
        out_shape=jax.ShapeDtypeStruct((r * TOKEN_SUBLANES, LANES), F32),
        compiler_params=_cparams(("arbitrary",)),
        name="experts",
    )(tile_expert, n_used, xs, w1, b1.reshape(N_EXPERTS, 1, 2 * D_FF), w2, b2.reshape(N_EXPERTS, 1, D_MODEL))


def _combine_kernel(pos_ref, ys_ref, x1_ref, wts_ref, lnf_ref, y_ref, rows_scr, sem):
    tc = x1_ref.shape[0]

    def issue(r, carry):
        for kk in range(TOP_K):
            pltpu.make_async_copy(ys_ref.at[_tile_rows(pos_ref[0, kk, r]), :],
                                  rows_scr.at[kk, _tile_rows(r), :], sem).start()
        return carry

    lax.fori_loop(0, tc, issue, 0, unroll=2)
    for kk in range(TOP_K):
        pltpu.make_async_copy(ys_ref.at[pl.ds(0, tc * TOKEN_SUBLANES), :], rows_scr.at[kk], sem).wait()
    w = wts_ref[...]
    acc = x1_ref[...]
    for kk in range(TOP_K):
        acc = acc + w[:, kk:kk + 1] * _load_token_tiles(rows_scr.at[kk], tc)
    y_ref[...] = acc * lax.rsqrt(jnp.mean(acc * acc, axis=-1, keepdims=True) + EPS) * lnf_ref[...]


def _combine(pos, ys, x1, wts, lnf):
    t = x1.shape[0]
    tc = COMB_TILE
    row = lambda i: (i, 0)
    pos3 = pos.reshape(t // tc, tc, TOP_K).transpose(0, 2, 1)
    return pl.pallas_call(
        _combine_kernel,
        grid=(t // tc,),
        in_specs=[
            pl.BlockSpec((1, TOP_K, tc), lambda i: (i, 0, 0), memory_space=pltpu.SMEM),
            pl.BlockSpec(memory_space=pl.ANY),
            pl.BlockSpec((tc, D_MODEL), row),
            pl.BlockSpec((tc, LANES), row),
            pl.BlockSpec((1, D_MODEL), lambda i: (0, 0)),
        ],
        out_specs=pl.BlockSpec((tc, D_MODEL), row),
        out_shape=jax.ShapeDtypeStruct((t, D_MODEL), F32),
        scratch_shapes=[pltpu.VMEM((TOP_K, tc * TOKEN_SUBLANES, LANES), F32), pltpu.SemaphoreType.DMA],
        compiler_params=_cparams(("arbitrary",)),
        name="combine",
    )(pos3, ys, x1, wts, lnf)


def _consts():
    r64 = np.zeros((D_MODEL, LANES), np.float32)
    e64 = np.zeros((LANES, D_MODEL), np.float32)
    for h in range(F_HEADS):
        r64[h * F_DH:(h + 1) * F_DH, h] = 1.0 / F_DH
        e64[h, h * F_DH:(h + 1) * F_DH] = 1.0
    return jnp.asarray(r64, BF16), jnp.asarray(e64, BF16), jnp.asarray(e64, F32)


def _tril(n, strict=False):
    return jnp.asarray(np.tril(np.ones((n, n), np.float32), -1 if strict else 0), BF16)


def _round_up(a, b):
    return (a + b - 1) // b * b


def _moe_layer(t_all, x1_all, lg_all, w1, b1, w2, b2, lnf):
    tmoe = x1_all.shape[0]
    ids, wts, rank, cnt = _route(lg_all, _tril(COMB_TILE, strict=True), COMB_TILE)
    counts = cnt[0, :N_EXPERTS]
    pcounts = (counts + EXP_TILE - 1) // EXP_TILE * EXP_TILE
    ends = jnp.cumsum(pcounts)
    starts = ends - pcounts
    pos = starts[ids[:, :TOP_K]] + rank[:, :TOP_K]
    r_pad = _round_up(TOP_K * tmoe + N_EXPERTS * (EXP_TILE - 1), EXP_TILE)
    src = jnp.zeros((r_pad,), I32).at[pos.reshape(-1)].set(jnp.repeat(jnp.arange(tmoe, dtype=I32), TOP_K))
    n_tiles = r_pad // EXP_TILE
    tile_start = jnp.arange(n_tiles, dtype=I32) * EXP_TILE
    tile_expert = jnp.minimum(jnp.sum((ends[None, :] <= tile_start[:, None]).astype(I32), axis=1), N_EXPERTS - 1)
    n_used = (ends[-1] // EXP_TILE).astype(I32)[None]
    xsrt = _gather_rows(t_all, src, GATHER_TILE)
    ysrt = _experts(tile_expert, n_used, xsrt, w1, b1, w2, b2)
    return _combine(pos, ysrt, x1_all, wts, lnf)


def kernel(x_prompt, x_sample, cache_k, cache_v, cache_logf, state_C, state_n, state_m, page_table, meta_tokens, ln1, w_in, mlstm_b_i, mlstm_b_f, mlstm_head_norm, fox_b_f, fox_q_norm, fox_k_norm, w_branch_mlstm, w_branch_fox, w_out, ln2, w_router, b_router, w_exp_in, b_exp_in, w_exp_out, b_exp_out, ln_final):
    depth = w_in.shape[0]
    b, seq, _ = x_prompt.shape
    db, ds, _ = x_sample.shape
    assert ds == 1 and seq % ATT_BLOCK == 0 and db % MSTEP_SEQS == 0
    l_true = seq + N_META
    lp = seq + ATT_BLOCK
    tp = b * lp
    n_phys, pg = cache_k.shape[1], cache_k.shape[2]
    r64, e64, e64f = _consts()

    xp = jnp.concatenate([jnp.zeros((b, PAD_FRONT, D_MODEL), F32),
                          jnp.broadcast_to(meta_tokens[None].astype(F32), (b, N_META, D_MODEL)),
                          x_prompt], axis=1).reshape(tp, D_MODEL)
    xs = x_sample.reshape(db, D_MODEL)

    outs = {k: [] for k in ("kp", "vp", "lfp", "ks", "vs", "lfs", "cp", "np", "mp", "cs", "ns", "ms")}
    offs = np.cumsum((0,) + (M_HEADS * M_DK, M_HEADS * M_DK, M_HEADS * M_DV, M_HEADS * M_DV, M_HEADS, M_HEADS,
                             F_HEADS * F_DH, F_HEADS * F_DH, F_HEADS * F_DH, F_HEADS, 2 * D_MODEL))
    seg = lambda w, i: w[:, offs[i]:offs[i + 1]]
    for l in range(depth):
        w = w_in[l]
        gates = seg(w, 10)
        wmain = jnp.stack([jnp.concatenate([seg(w, 0), seg(w, 1)], axis=1), seg(w, 2), seg(w, 3), seg(w, 6),
                           seg(w, 7), seg(w, 8), gates[:, :D_MODEL], gates[:, D_MODEL:]]).astype(BF16)
        wsm = jnp.concatenate([seg(w, 4), seg(w, 5), seg(w, 9),
                               jnp.zeros((D_MODEL, LANES - G_END), F32)], axis=1)
        wsh = wsm.astype(BF16)
        wsl = (wsm - wsh.astype(F32)).astype(BF16)
        bias = jnp.concatenate([mlstm_b_i[l], mlstm_b_f[l], fox_b_f[l], jnp.zeros((LANES - G_END,), F32)])[None]
        qn = jnp.tile(fox_q_norm[l], F_HEADS)[None]
        kn = jnp.tile(fox_k_norm[l], F_HEADS)[None]
        hnw = mlstm_head_norm[l][None]
        ln1l = ln1[l][None]
        wbm = w_branch_mlstm[l].astype(BF16)
        wbf = w_branch_fox[l].astype(BF16)
        wo = w_out[l].astype(BF16)
        wr = jnp.concatenate([w_router[l], jnp.zeros((D_MODEL, LANES - N_EXPERTS), F32)], axis=1)
        wrh = wr.astype(BF16)
        wrl = (wr - wrh.astype(F32)).astype(BF16)
        br = jnp.concatenate([b_router[l], jnp.zeros((LANES - N_EXPERTS,), F32)])[None]

        zp, kpt32, vpt32, vpt16, gsp = _proj(xp, ln1l, wmain, wsh, wsl, bias, qn, kn, r64, e64, ROW_TILE)
        zp3 = zp.reshape(b, lp, N_ZB * D_MODEL)
        hnp, kap, qap, c_p, n_p, m_p = _mlstm_prompt(zp3, gsp.reshape(b, lp, LANES), hnw, _tril(M_CHUNK))
        hfp = _fox_prompt(zp3, qap, kap, vpt16)
        x1p, tpn, lgp = _merge(xp, hnp.reshape(tp, D_MODEL), zp, hfp.reshape(tp, D_MODEL), wbm, wbf, wo,
                               ln2[l][None], wrh, wrl, br, ROW_TILE)
        unpad = lambda a: jnp.transpose(a.reshape(F_HEADS, F_DH, b, lp)[:, :, :, PAD_FRONT:], (2, 3, 0, 1))
        outs["kp"].append(unpad(kpt32))
        outs["vp"].append(unpad(vpt32))
        outs["lfp"].append(gsp.reshape(b, lp, LANES)[:, PAD_FRONT:, G_FF:G_END])
        outs["cp"].append(c_p)
        outs["np"].append(n_p[:, :, 0, :])
        outs["mp"].append(m_p[:, :, 0, 0])

        zs, kst32, vst32, _, gss = _proj(xs, ln1l, wmain, wsh, wsl, bias, qn, kn, r64, e64, db)
        vs32 = vst32.T
        m_in = jnp.concatenate([state_m[l], jnp.zeros((db, LANES - M_HEADS), F32)], axis=1)
        hns, c_s, n_s, m_s = _mlstm_step(zs, gss, m_in, state_n[l].reshape(db, M_HEADS * M_DK), state_C[l], hnw)
        n_s = n_s.reshape(db, M_HEADS, M_DK)
        q3 = zs[:, ZB_FQ * D_MODEL:(ZB_FQ + 1) * D_MODEL].astype(F32).reshape(db, 1, D_MODEL)
        kn3 = zs[:, ZB_FK * D_MODEL:(ZB_FK + 1) * D_MODEL].astype(F32).reshape(db, 1, D_MODEL)
        hfs = _fox_decode(page_table, q3, kn3, vs32.reshape(db, 1, D_MODEL),
                          jnp.broadcast_to(gss[:, G_FF:G_END, None], (db, F_HEADS, LANES)),
                          jnp.transpose(cache_k[l], (0, 2, 3, 1)), jnp.transpose(cache_v[l], (0, 2, 3, 1)),
                          jnp.transpose(cache_logf[l], (0, 2, 1)))
        x1s, tsn, lgs = _merge(xs, hns, zs, hfs.reshape(db, D_MODEL).astype(BF16), wbm, wbf, wo,
                               ln2[l][None], wrh, wrl, br, db)
        outs["ks"].append(jnp.transpose(kst32.reshape(F_HEADS, F_DH, db), (2, 0, 1)).reshape(db, 1, F_HEADS, F_DH))
        outs["vs"].append(jnp.transpose(vst32.reshape(F_HEADS, F_DH, db), (2, 0, 1)).reshape(db, 1, F_HEADS, F_DH))
        outs["lfs"].append(gss[:, G_FF:G_END].reshape(db, 1, F_HEADS))
        outs["cs"].append(c_s)
        outs["ns"].append(n_s)
        outs["ms"].append(m_s[:, :M_HEADS])

        tt = tp + db
        tmoe = tt
        assert tmoe % COMB_TILE == 0
        t_all = jnp.concatenate([tpn, tsn], axis=0)
        x1_all = jnp.concatenate([x1p, x1s], axis=0)
        lg_all = jnp.concatenate([lgp, lgs], axis=0)
        lnf = ln_final[None] if l == depth - 1 else jnp.ones((1, D_MODEL), F32)
        y_all = _moe_layer(t_all, x1_all, lg_all, w_exp_in[l], b_exp_in[l], w_exp_out[l], b_exp_out[l], lnf)
        if l < depth - 1:
            raise NotImplementedError("deeper stacks need the un-normalised residual stream as well")

    y_prompt = y_all[:tp].reshape(b, lp, D_MODEL)[:, ATT_BLOCK:]
    y_sample = y_all[tp:tp + db].reshape(db, 1, D_MODEL)
    st = lambda k: jnp.stack(outs[k])
    return (y_prompt, y_sample, st("kp"), st("vp"), st("lfp"), st("ks"), st("vs"), st("lfs"),
            st("cp"), st("np"), st("mp"), st("cs"), st("ns"), st("ms"))
```

```python
import functools

import numpy as np
import jax
import jax.numpy as jnp
from jax import lax
from jax.experimental import pallas as pl
from jax.experimental.pallas import tpu as pltpu

F32 = jnp.float32
BF16 = jnp.bfloat16
I32 = jnp.int32

D_MODEL = 1024
N_META = 16
M_HEADS, M_DK, M_DV = 4, 128, 256
F_HEADS, F_DH = 16, 64
N_EXPERTS, TOP_K, D_FF = 32, 4, 1024
GATE_CAP = 15.0
SWIGLU_LIMIT = 7.0
SWIGLU_ALPHA = 1.702
EPS = 1e-6
ATTN_SCALE = F_DH ** -0.5
LOG2E = 1.4426950408889634

LANES = 128
ATT_BLOCK = 256
ATT_SUB = 128
PAD_FRONT = ATT_BLOCK - N_META
M_CHUNK = 128
ROW_TILE = 512
EXP_TILE = 256
COMB_TILE = 128
VMEM_LIMIT = 56 * 1024 * 1024

G_IG = 0
G_LF = M_HEADS
G_FF = 2 * M_HEADS
G_END = G_FF + F_HEADS

WB_MQK, WB_MV, WB_MO, WB_FQ, WB_FK, WB_FV, WB_GA, WB_GB = range(8)
N_WB = 8
ZB_MQK, ZB_MV, ZB_MO, ZB_FQ, ZB_FK, ZB_GA, ZB_GB = range(7)
N_ZB = 7


def _dot(a, b):
    return jnp.dot(a, b, preferred_element_type=F32)


def _dot_nt(a, b):
    return lax.dot_general(a, b, (((1,), (1,)), ((), ())), preferred_element_type=F32)


def _dot_tn(a, b):
    return lax.dot_general(a, b, (((0,), (0,)), ((), ())), preferred_element_type=F32)


def _split2(x):
    hi = x.astype(BF16)
    lo = (x - hi.astype(F32)).astype(BF16)
    return hi, lo


def _split3(x):
    a = x.astype(BF16)
    r = x - a.astype(F32)
    b = r.astype(BF16)
    c = (r - b.astype(F32)).astype(BF16)
    return a, b, c


def _dot_exact_rhs01(parts, m01):
    acc = _dot(parts[0], m01)
    for p in parts[1:]:
        acc = acc + _dot(p, m01)
    return acc


def _log_sigmoid(x):
    return jnp.minimum(x, 0.0) - jnp.log1p(jnp.exp(-jnp.abs(x)))


def _cparams(sem):
    return pltpu.CompilerParams(dimension_semantics=sem, vmem_limit_bytes=VMEM_LIMIT)


def _head_norm(y, g, r_ref, e_ref):
    ms = _dot((y * y).astype(BF16), r_ref[...])
    rs = lax.rsqrt(ms + EPS)
    rsx = _dot_exact_rhs01(_split2(rs), e_ref[...])
    return y * rsx * g


def _proj_kernel(x_ref, ln_ref, w_ref, wsh_ref, wsl_ref, bias_ref, qn_ref, kn_ref, r_ref, e_ref,
                 z_ref, kt32_ref, vt32_ref, vt16_ref, gs_ref):
    x = x_ref[...]
    ms = jnp.mean(x * x, axis=-1, keepdims=True)
    h = x * lax.rsqrt(ms + EPS) * ln_ref[...]
    hh, hl = _split2(h)
    wsh = wsh_ref[...]
    g = _dot(hh, wsh) + _dot(hl, wsh) + _dot(hh, wsl_ref[...]) + bias_ref[...]
    lane = lax.broadcasted_iota(I32, g.shape, 1)
    cap = GATE_CAP * jnp.tanh(g / GATE_CAP)
    ls = _log_sigmoid(jnp.where(lane < G_FF, cap, g))
    gs_ref[...] = jnp.where(lane < G_LF, cap, jnp.where(lane < G_END, ls, 0.0))

    def y_of(wb):
        return _dot(hh, w_ref[wb])

    def put(zb, val):
        z_ref[:, zb * D_MODEL:(zb + 1) * D_MODEL] = val.astype(BF16)

    y = y_of(WB_MQK)
    col = lax.broadcasted_iota(I32, y.shape, 1)
    put(ZB_MQK, jnp.where(col < M_HEADS * M_DK, y * (M_DK ** -0.5), y))
    put(ZB_MV, y_of(WB_MV))
    put(ZB_MO, jax.nn.sigmoid(y_of(WB_MO)))
    put(ZB_FQ, _head_norm(y_of(WB_FQ), qn_ref[...], r_ref, e_ref) * (ATTN_SCALE * LOG2E))
    kn = _head_norm(y_of(WB_FK), kn_ref[...], r_ref, e_ref)
    kt32_ref[...] = kn.T
    put(ZB_FK, kn)
    vt = y_of(WB_FV).T
    vt32_ref[...] = vt
    vt16_ref[...] = vt.astype(BF16)
    put(ZB_GA, jax.nn.sigmoid(y_of(WB_GA)))
    put(ZB_GB, jax.nn.sigmoid(y_of(WB_GB)))


def _proj(x, ln, wmain, wsh, wsl, bias, qn, kn, r64, e64, tm):
    t = x.shape[0]
    row = lambda i: (i, 0)
    col = lambda i: (0, i)
    fix = lambda i: (0, 0)
    return pl.pallas_call(
        _proj_kernel,
        grid=(t // tm,),
        in_specs=[
            pl.BlockSpec((tm, D_MODEL), row),
            pl.BlockSpec((1, D_MODEL), fix),
            pl.BlockSpec((N_WB, D_MODEL, D_MODEL), lambda i: (0, 0, 0), pipeline_mode=pl.Buffered(1)),
            pl.BlockSpec((D_MODEL, LANES), fix),
            pl.BlockSpec((D_MODEL, LANES), fix),
            pl.BlockSpec((1, LANES), fix),
            pl.BlockSpec((1, D_MODEL), fix),
            pl.BlockSpec((1, D_MODEL), fix),
            pl.BlockSpec((D_MODEL, LANES), fix),
            pl.BlockSpec((LANES, D_MODEL), fix),
        ],
        out_specs=[
            pl.BlockSpec((tm, N_ZB * D_MODEL), row),
            pl.BlockSpec((D_MODEL, tm), col),
            pl.BlockSpec((D_MODEL, tm), col),
            pl.BlockSpec((D_MODEL, tm), col),
            pl.BlockSpec((tm, LANES), row),
        ],
        out_shape=[
            jax.ShapeDtypeStruct((t, N_ZB * D_MODEL), BF16),
            jax.ShapeDtypeStruct((D_MODEL, t), F32),
            jax.ShapeDtypeStruct((D_MODEL, t), F32),
            jax.ShapeDtypeStruct((D_MODEL, t), BF16),
            jax.ShapeDtypeStruct((t, LANES), F32),
        ],
        compiler_params=_cparams(("parallel",)),
        name="proj",
    )(x, ln, wmain, wsh, wsl, bias, qn, kn, r64, e64)


def _mlstm_kernel(qk_ref, v_ref, gs_ref, hnw_ref, tril_ref, place_ref,
                  hn_ref, ka_ref, qa_ref, c_out, n_out, m_out,
                  c_scr, n_scr, m_scr, crow_scr):
    c = pl.program_id(1)
    lc = M_CHUNK

    @pl.when(c == 0)
    def _():
        c_scr[...] = jnp.zeros_like(c_scr)
        n_scr[...] = jnp.zeros_like(n_scr)
        m_scr[...] = jnp.zeros_like(m_scr)
        crow_scr[...] = jnp.zeros_like(crow_scr)

    g = gs_ref[0]
    lane = lax.broadcasted_iota(I32, (lc, LANES), 1)
    pos = c * lc + lax.broadcasted_iota(I32, (lc, LANES), 0)
    valid = pos >= PAD_FRONT
    is_ig = lane < G_LF
    gsum = jnp.where(valid & jnp.logical_not(is_ig), g, 0.0)
    bcum = _dot_exact_rhs01_lhs(tril_ref[...], _split3(gsum))
    comb = jnp.where(is_ig, jnp.where(valid, g, -jnp.inf), bcum)
    comb_t = comb.T

    row_t = lax.broadcasted_iota(I32, (lc, lc), 0)
    col_s = lax.broadcasted_iota(I32, (lc, lc), 1)
    causal = col_s <= row_t

    qk = qk_ref[0]
    vv = v_ref[0]
    for h in range(M_HEADS):
        q = qk[:, h * M_DK:(h + 1) * M_DK]
        k = qk[:, (M_HEADS + h) * M_DK:(M_HEADS + h + 1) * M_DK]
        v = vv[:, h * M_DV:(h + 1) * M_DV]
        m_prev = m_scr[h, 0:1, 0:1]
        b_col = bcum[:, G_LF + h:G_LF + h + 1]
        ig_col = comb[:, G_IG + h:G_IG + h + 1]
        b_row = comb_t[G_LF + h:G_LF + h + 1, :]
        ig_row = comb_t[G_IG + h:G_IG + h + 1, :]

        log_d = jnp.where(causal, b_col - b_row + ig_row, -jnp.inf)
        inter = b_col + m_prev
        m_row = jnp.maximum(inter, jnp.max(log_d, axis=1, keepdims=True))
        s = _dot_nt(q, k) * jnp.exp(log_d - m_row)
        w_int = jnp.exp(inter - m_row)
        cmat = c_scr[h]
        nvec = n_scr[h, 0:1, :]
        num = _dot(s.astype(BF16), v) + w_int * _dot_nt(q, cmat.astype(BF16))
        den = jnp.sum(s, axis=1, keepdims=True) + w_int * jnp.sum(q.astype(F32) * nvec, axis=1, keepdims=True)
        hh = num / jnp.maximum(jnp.abs(den), jnp.exp(-m_row))
        hn = hh * lax.rsqrt(jnp.mean(hh * hh, axis=1, keepdims=True) + EPS) * hnw_ref[:, h * M_DV:(h + 1) * M_DV]
        hn_ref[0, :, h * M_DV:(h + 1) * M_DV] = hn.astype(BF16)

        b_last = b_col[lc - 1:lc, :]
        m_new = jnp.maximum(b_last + m_prev, jnp.max(b_last - b_row + ig_row, axis=1, keepdims=True))
        decay = jnp.exp(b_last + m_prev - m_new)
        ws_col = jnp.exp(b_last - b_col + ig_col - m_new)
        kw = k.astype(F32) * ws_col
        c_scr[h] = decay * cmat + _dot_tn(v, kw.astype(BF16))
        n_scr[h] = jnp.broadcast_to(decay * nvec + jnp.sum(kw, axis=0, keepdims=True), n_scr.shape[1:])
        m_scr[h] = jnp.broadcast_to(m_new, m_scr.shape[1:])

    cg = bcum + crow_scr[0:1, :]
    crow_scr[...] = jnp.broadcast_to(cg[lc - 1:lc, :], crow_scr.shape)
    p0, p1, p2 = _split3(cg * LOG2E)
    p0 = jnp.where(lane == 0, jnp.ones_like(p0), p0)
    ka_ref[0] = (_dot(p0, place_ref[0]) + _dot(p1, place_ref[1]) + _dot(p2, place_ref[2])).astype(BF16)
    qa_ref[0] = (_dot(p0, place_ref[3]) + _dot(p1, place_ref[4]) + _dot(p2, place_ref[5])).astype(BF16)

    @pl.when(c == pl.num_programs(1) - 1)
    def _():
        c_out[0] = c_scr[...]
        n_out[0] = n_scr[...]
        m_out[0] = m_scr[...]


def _dot_exact_rhs01_lhs(m01, parts):
    acc = _dot(m01, parts[0])
    for p in parts[1:]:
        acc = acc + _dot(m01, p)
    return acc


BIAS_LANES = 6


def _bias_placement():
    pm = np.zeros((6, LANES, LANES), np.float32)
    for h in range(F_HEADS):
        base = BIAS_LANES * h
        for t in range(3):
            pm[t, G_FF + h, base + t] = -1.0
            pm[0, 0, base + 3 + t] = 1.0
            pm[3 + t, G_FF + h, base + 3 + t] = 1.0
            pm[3, 0, base + t] = 1.0
    return jnp.asarray(pm, BF16)


def _mlstm_prompt(z3, gs3, hnw, tril):
    b, lp, _ = z3.shape
    lc = M_CHUNK
    nc = lp // lc
    return pl.pallas_call(
        _mlstm_kernel,
        grid=(b, nc),
        in_specs=[
            pl.BlockSpec((1, lc, D_MODEL), lambda i, c: (i, c, ZB_MQK)),
            pl.BlockSpec((1, lc, D_MODEL), lambda i, c: (i, c, ZB_MV)),
            pl.BlockSpec((1, lc, LANES), lambda i, c: (i, c, 0)),
            pl.BlockSpec((1, D_MODEL), lambda i, c: (0, 0)),
            pl.BlockSpec((lc, lc), lambda i, c: (0, 0)),
            pl.BlockSpec((6, LANES, LANES), lambda i, c: (0, 0, 0)),
        ],
        out_specs=[
            pl.BlockSpec((1, lc, D_MODEL), lambda i, c: (i, c, 0)),
            pl.BlockSpec((1, lc, LANES), lambda i, c: (i, c, 0)),
            pl.BlockSpec((1, lc, LANES), lambda i, c: (i, c, 0)),
            pl.BlockSpec((1, M_HEADS, M_DV, M_DK), lambda i, c: (i, 0, 0, 0)),
            pl.BlockSpec((1, M_HEADS, 8, M_DK), lambda i, c: (i, 0, 0, 0)),
            pl.BlockSpec((1, M_HEADS, 8, LANES), lambda i, c: (i, 0, 0, 0)),
        ],
        out_shape=[
            jax.ShapeDtypeStruct((b, lp, D_MODEL), BF16),
            jax.ShapeDtypeStruct((b, lp, LANES), BF16),
            jax.ShapeDtypeStruct((b, lp, LANES), BF16),
            jax.ShapeDtypeStruct((b, M_HEADS, M_DV, M_DK), F32),
            jax.ShapeDtypeStruct((b, M_HEADS, 8, M_DK), F32),
            jax.ShapeDtypeStruct((b, M_HEADS, 8, LANES), F32),
        ],
        scratch_shapes=[
            pltpu.VMEM((M_HEADS, M_DV, M_DK), F32),
            pltpu.VMEM((M_HEADS, 8, M_DK), F32),
            pltpu.VMEM((M_HEADS, 8, LANES), F32),
            pltpu.VMEM((8, LANES), F32),
        ],
        compiler_params=_cparams(("parallel", "arbitrary")),
        name="mlstm",
    )(z3, z3, gs3, hnw, tril, _bias_placement())


MSTEP_SEQS = 8


def _mstep_kernel(qk_ref, v_ref, gs_ref, m_ref, n_ref, c_ref, hnw_ref,
                  hn_ref, c_out, n_out, m_out):
    sb = MSTEP_SEQS
    qk = qk_ref[...]
    vv = v_ref[...]
    g = gs_ref[...]
    row = lax.broadcasted_iota(I32, (sb, 1), 0)
    m_new_all = jnp.zeros((sb, LANES), F32)
    lane = lax.broadcasted_iota(I32, (sb, LANES), 1)
    for h in range(M_HEADS):
        q = qk[:, h * M_DK:(h + 1) * M_DK]
        k = qk[:, (M_HEADS + h) * M_DK:(M_HEADS + h + 1) * M_DK]
        v = vv[:, h * M_DV:(h + 1) * M_DV]
        ig = g[:, G_IG + h:G_IG + h + 1]
        lf = g[:, G_LF + h:G_LF + h + 1]
        m_prev = m_ref[:, h:h + 1]
        inter = lf + m_prev
        m_row = jnp.maximum(inter, ig)
        d = jnp.exp(ig - m_row)
        w_int = jnp.exp(inter - m_row)
        qf = q.astype(F32)
        kf = k.astype(F32)
        s = jnp.sum(qf * kf, axis=1, keepdims=True) * d
        nvec = n_ref[:, h * M_DK:(h + 1) * M_DK]
        den = s + w_int * jnp.sum(qf * nvec, axis=1, keepdims=True)
        kw = (kf * d).astype(BF16)
        hrows = []
        for i in range(sb):
            cmat = c_ref[i, h]
            sel = row == i
            qi = jnp.where(sel, q, jnp.zeros_like(q))
            qc = _dot_nt(qi, cmat.astype(BF16))
            hrows.append(jnp.where(sel, qc, 0.0))
            vi = jnp.where(sel, v, jnp.zeros_like(v))
            c_out[i, h] = w_int[i:i + 1, :] * cmat + _dot_tn(vi, kw)
        qc_all = functools.reduce(lambda a, b: a + b, hrows)
        num = s * v.astype(F32) + w_int * qc_all
        hh = num / jnp.maximum(jnp.abs(den), jnp.exp(-m_row))
        hn = hh * lax.rsqrt(jnp.mean(hh * hh, axis=1, keepdims=True) + EPS) * hnw_ref[:, h * M_DV:(h + 1) * M_DV]
        hn_ref[:, h * M_DV:(h + 1) * M_DV] = hn.astype(BF16)
        n_out[:, h * M_DK:(h + 1) * M_DK] = w_int * nvec + kf * d
        m_new_all = jnp.where(lane == h, m_row, m_new_all)
    m_out[...] = m_new_all


def _mlstm_step(zs, gss, m_in, n_in, c_in, hnw):
    db = zs.shape[0]
    sb = MSTEP_SEQS
    return pl.pallas_call(
        _mstep_kernel,
        grid=(db // sb,),
        in_specs=[
            pl.BlockSpec((sb, D_MODEL), lambda i: (i, ZB_MQK)),
            pl.BlockSpec((sb, D_MODEL), lambda i: (i, ZB_MV)),
            pl.BlockSpec((sb, LANES), lambda i: (i, 0)),
            pl.BlockSpec((sb, LANES), lambda i: (i, 0)),
            pl.BlockSpec((sb, M_HEADS * M_DK), lambda i: (i, 0)),
            pl.BlockSpec((sb, M_HEADS, M_DV, M_DK), lambda i: (i, 0, 0, 0)),
            pl.BlockSpec((1, D_MODEL), lambda i: (0, 0)),
        ],
        out_specs=[
            pl.BlockSpec((sb, D_MODEL), lambda i: (i, 0)),
            pl.BlockSpec((sb, M_HEADS, M_DV, M_DK), lambda i: (i, 0, 0, 0)),
            pl.BlockSpec((sb, M_HEADS * M_DK), lambda i: (i, 0)),
            pl.BlockSpec((sb, LANES), lambda i: (i, 0)),
        ],
        out_shape=[
            jax.ShapeDtypeStruct((db, D_MODEL), BF16),
            jax.ShapeDtypeStruct((db, M_HEADS, M_DV, M_DK), F32),
            jax.ShapeDtypeStruct((db, M_HEADS * M_DK), F32),
            jax.ShapeDtypeStruct((db, LANES), F32),
        ],
        compiler_params=_cparams(("parallel",)),
        name="mstep",
    )(zs, zs, gss, m_in, n_in, c_in, hnw)


def _fox_kernel(qi_tab, ki_tab, q_ref, qa_ref, k_ref, ka_ref, vt_ref, o_ref,
                qb_scr, acc_scr, m_scr, l_scr, a_scr, st_scr, pt_scr):
    step = pl.program_id(1)
    qi = qi_tab[step]
    ki = ki_tab[step]
    tq = tk = ATT_BLOCK

    @pl.when(ki == 0)
    def _():
        acc_scr[...] = jnp.zeros_like(acc_scr)
        m_scr[...] = jnp.full_like(m_scr, -jnp.inf)
        l_scr[...] = jnp.zeros_like(l_scr)
        lane = lax.broadcasted_iota(I32, (tq, LANES), 1)
        qa = qa_ref[0]
        for h in range(F_HEADS):
            q2 = q_ref[0, :, (h // 2) * LANES:(h // 2 + 1) * LANES]
            own = (lane >= (h % 2) * F_DH) & (lane < (h % 2 + 1) * F_DH)
            bias = (lane >= h * BIAS_LANES) & (lane < (h + 1) * BIAS_LANES)
            qb_scr[h] = jnp.concatenate([jnp.where(own, q2, jnp.zeros_like(q2)),
                                         jnp.where(bias, qa, jnp.zeros_like(qa))], axis=1)

    def sweep(masked):
        ka = ka_ref[0]
        if masked:
            key = ki * tk + lax.broadcasted_iota(I32, (tk, tq), 0)
            qry = qi * tq + lax.broadcasted_iota(I32, (tk, tq), 1)
            visible = (key <= qry) & ((key >= PAD_FRONT) | (qry < PAD_FRONT))
        for p in range(F_HEADS // 2):
            kb = jnp.concatenate([k_ref[0, :, p * LANES:(p + 1) * LANES], ka], axis=1)
            for h in (2 * p, 2 * p + 1):
                st_scr[h] = _dot_nt(kb, qb_scr[h])
        for h in range(F_HEADS):
            for c0 in range(0, tq, LANES):
                cs = slice(c0, c0 + LANES)
                st = st_scr[h, :, cs]
                if masked:
                    st = jnp.where(visible[:, cs], st, -jnp.inf)
                m_prev = m_scr[h, 0:1, cs]
                m_next = jnp.maximum(m_prev, jnp.max(st, axis=0, keepdims=True))
                alpha = jnp.exp2(m_prev - m_next)
                pt = jnp.exp2(st - m_next)
                l_scr[h, :, cs] = jnp.broadcast_to(
                    alpha * l_scr[h, 0:1, cs] + jnp.sum(pt, axis=0, keepdims=True), (8, LANES))
                m_scr[h, :, cs] = jnp.broadcast_to(m_next, (8, LANES))
                a_scr[h, :, cs] = jnp.broadcast_to(alpha, (8, LANES))
                pt_scr[h, :, cs] = pt.astype(BF16)
        for h in range(F_HEADS):
            acc_scr[h] = a_scr[h, 0:1, :] * acc_scr[h] + _dot(vt_ref[h * F_DH:(h + 1) * F_DH, :], pt_scr[h])

    edge = (ki == qi) | (ki == 0)

    @pl.when(edge)
    def _():
        sweep(True)

    @pl.when(jnp.logical_not(edge))
    def _():
        sweep(False)

    @pl.when(ki == qi)
    def _():
        for p in range(F_HEADS // 2):
            o2 = jnp.concatenate([acc_scr[2 * p] * (1.0 / l_scr[2 * p, 0:1, :]),
                                  acc_scr[2 * p + 1] * (1.0 / l_scr[2 * p + 1, 0:1, :])], axis=0)
            o_ref[0, :, p * LANES:(p + 1) * LANES] = o2.T.astype(BF16)


def _fox_prompt(z3, qa, ka, vt16):
    b, lp, _ = z3.shape
    blk = ATT_BLOCK
    nb = lp // blk
    qi_tab = np.concatenate([np.full((i + 1,), i, np.int32) for i in range(nb)])
    ki_tab = np.concatenate([np.arange(i + 1, dtype=np.int32) for i in range(nb)])
    grid_spec = pltpu.PrefetchScalarGridSpec(
        num_scalar_prefetch=2,
        grid=(b, len(qi_tab)),
        in_specs=[
            pl.BlockSpec((1, blk, D_MODEL), lambda i, s, qt, kt: (i, qt[s], ZB_FQ)),
            pl.BlockSpec((1, blk, LANES), lambda i, s, qt, kt: (i, qt[s], 0)),
            pl.BlockSpec((1, blk, D_MODEL), lambda i, s, qt, kt: (i, kt[s], ZB_FK)),
            pl.BlockSpec((1, blk, LANES), lambda i, s, qt, kt: (i, kt[s], 0)),
            pl.BlockSpec((D_MODEL, blk), lambda i, s, qt, kt: (0, i * nb + kt[s])),
        ],
        out_specs=pl.BlockSpec((1, blk, D_MODEL), lambda i, s, qt, kt: (i, qt[s], 0)),
        scratch_shapes=[
            pltpu.VMEM((F_HEADS, blk, 2 * LANES), BF16),
            pltpu.VMEM((F_HEADS, F_DH, blk), F32),
            pltpu.VMEM((F_HEADS, 8, blk), F32),
            pltpu.VMEM((F_HEADS, 8, blk), F32),
            pltpu.VMEM((F_HEADS, 8, blk), F32),
            pltpu.VMEM((F_HEADS, blk, blk), F32),
            pltpu.VMEM((F_HEADS, blk, blk), BF16),
        ],
    )
    return pl.pallas_call(
        _fox_kernel,
        grid_spec=grid_spec,
        out_shape=jax.ShapeDtypeStruct((b, lp, D_MODEL), BF16),
        compiler_params=_cparams(("parallel", "arbitrary")),
        name="fox",
    )(jnp.asarray(qi_tab), jnp.asarray(ki_tab), z3, qa, z3, ka, vt16)


DEC_PAGES = 8


def _lane_bcast_cols(row):
    full = jnp.broadcast_to(row, (LANES, D_MODEL)).T
    return full.reshape(F_HEADS, F_DH, LANES)


def _per_head(x):
    return x.reshape(F_HEADS, 1, LANES)


def _dec_kernel(pt_ref, q_ref, kn_ref, vn_ref, lfn_ref, triu_ref, ones_ref, *rest):
    np_ = DEC_PAGES
    k_refs, v_refs, lf_refs = rest[:np_], rest[np_:2 * np_], rest[2 * np_:3 * np_]
    o_ref, qb_scr, acc_scr, m_scr, l_scr, c_scr = rest[3 * np_:]
    p = pl.program_id(1)

    @pl.when(p == 0)
    def _():
        qb_scr[...] = _lane_bcast_cols(q_ref[0])
        acc_scr[...] = jnp.zeros_like(acc_scr)
        m_scr[...] = jnp.full_like(m_scr, -jnp.inf)
        l_scr[...] = jnp.zeros_like(l_scr)
        c_scr[...] = jnp.zeros_like(c_scr)

    qb = qb_scr[...]
    carry = c_scr[...]
    us = []
    for i in range(np_):
        cum = _dot_exact_rhs01(_split3(lf_refs[i][0]), triu_ref[...]) + carry
        carry = jnp.broadcast_to(cum[:, LANES - 1:LANES], carry.shape)
        s = jnp.sum(qb * k_refs[i][0], axis=1)
        us.append(s - cum * LOG2E)
    c_scr[...] = carry
    m_prev = m_scr[...]
    m_cur = functools.reduce(jnp.maximum, [jnp.max(u, axis=1, keepdims=True) for u in us])
    m_next = jnp.maximum(m_prev, m_cur)
    alpha = jnp.exp2(m_prev - m_next)
    acc = _per_head(alpha) * acc_scr[...]
    l_new = alpha * l_scr[...]
    for i in range(np_):
        pr = jnp.exp2(us[i] - m_next)
        l_new = l_new + jnp.sum(pr, axis=1, keepdims=True)
        acc = acc + _per_head(pr) * v_refs[i][0]
    acc_scr[...] = acc
    l_scr[...] = l_new
    m_scr[...] = m_next

    @pl.when(p == pl.num_programs(1) - 1)
    def _():
        s_new = jnp.sum(qb * _lane_bcast_cols(kn_ref[0]), axis=1)
        u_new = s_new - (carry + lfn_ref[0]) * LOG2E
        m_fin = jnp.maximum(m_next, u_new)
        a_fin = jnp.exp2(m_next - m_fin)
        pn = jnp.exp2(u_new - m_fin)
        l_fin = a_fin * l_new + pn
        tot = (_per_head(a_fin) * acc + _per_head(pn * (1.0 / LANES)) * _lane_bcast_cols(vn_ref[0])) / _per_head(l_fin)
        out8 = _dot_nt_exact_lhs01(ones_ref[...], _split3(tot.reshape(D_MODEL, LANES)))
        o_ref[0] = out8[0:1, :]


def _dot_nt_exact_lhs01(m01, parts):
    acc = _dot_nt(m01, parts[0])
    for part in parts[1:]:
        acc = acc + _dot_nt(m01, part)
    return acc


def _fox_decode(page_table, q3, kn3, vn3, lfn3, cache_kt, cache_vt, cache_lft):
    db, npg = page_table.shape
    pg = cache_kt.shape[3]
    assert pg == LANES and npg % DEC_PAGES == 0
    row3 = lambda s, p, pt: (s, 0, 0)
    fix = lambda s, p, pt: (0, 0)

    def page(i, nd):
        return lambda s, p, pt: (pt[s, p * DEC_PAGES + i],) + (0,) * nd

    in_specs = [
        pl.BlockSpec((1, 1, D_MODEL), row3),
        pl.BlockSpec((1, 1, D_MODEL), row3),
        pl.BlockSpec((1, 1, D_MODEL), row3),
        pl.BlockSpec((1, F_HEADS, LANES), row3),
        pl.BlockSpec((LANES, LANES), fix),
        pl.BlockSpec((8, LANES), fix),
    ]
    in_specs += [pl.BlockSpec((1, F_HEADS, F_DH, LANES), page(i, 3)) for i in range(DEC_PAGES)]
    in_specs += [pl.BlockSpec((1, F_HEADS, F_DH, LANES), page(i, 3)) for i in range(DEC_PAGES)]
    in_specs += [pl.BlockSpec((1, F_HEADS, LANES), page(i, 2)) for i in range(DEC_PAGES)]
    grid_spec = pltpu.PrefetchScalarGridSpec(
        num_scalar_prefetch=1,
        grid=(db, npg // DEC_PAGES),
        in_specs=in_specs,
        out_specs=pl.BlockSpec((1, 1, D_MODEL), row3),
        scratch_shapes=[
            pltpu.VMEM((F_HEADS, F_DH, LANES), F32),
            pltpu.VMEM((F_HEADS, F_DH, LANES), F32),
            pltpu.VMEM((F_HEADS, LANES), F32),
            pltpu.VMEM((F_HEADS, LANES), F32),
            pltpu.VMEM((F_HEADS, LANES), F32),
        ],
    )
    triu = jnp.asarray(np.triu(np.ones((LANES, LANES), np.float32)), BF16)
    ones = jnp.ones((8, LANES), BF16)
    return pl.pallas_call(
        _dec_kernel,
        grid_spec=grid_spec,
        out_shape=jax.ShapeDtypeStruct((db, 1, D_MODEL), F32),
        compiler_params=_cparams(("parallel", "arbitrary")),
        name="dec",
    )(page_table, q3, kn3, vn3, lfn3, triu, ones,
      *([cache_kt] * DEC_PAGES), *([cache_vt] * DEC_PAGES), *([cache_lft] * DEC_PAGES))


TOKEN_SUBLANES = D_MODEL // LANES


def _store_token_tiles(ref, x):
    n = x.shape[0]
    for g in range(TOKEN_SUBLANES):
        ref[pl.ds(g, n, stride=TOKEN_SUBLANES), :] = x[:, g * LANES:(g + 1) * LANES]


def _load_token_tiles(ref, n):
    return jnp.concatenate([ref[pl.ds(g, n, stride=TOKEN_SUBLANES), :] for g in range(TOKEN_SUBLANES)], axis=1)


def _merge_kernel(x_ref, hn_ref, so_ref, hf_ref, ga_ref, gb_ref, wbm_ref, wbf_ref, wo_ref, ln2_ref,
                  wrh_ref, wrl_ref, br_ref, x1_ref, t_ref, lg_ref):
    hm = hn_ref[...] * so_ref[...]
    ya = _dot(hm, wbm_ref[...])
    yb = _dot(hf_ref[...], wbf_ref[...])
    u = ga_ref[...].astype(F32) * ya + gb_ref[...].astype(F32) * yb
    x1 = x_ref[...] + _dot(u.astype(BF16), wo_ref[...])
    x1_ref[...] = x1
    t = x1 * lax.rsqrt(jnp.mean(x1 * x1, axis=-1, keepdims=True) + EPS) * ln2_ref[...]
    _store_token_tiles(t_ref, t)
    th, tl = _split2(t)
    wrh = wrh_ref[...]
    lg_ref[...] = _dot(th, wrh) + _dot(tl, wrh) + _dot(th, wrl_ref[...]) + br_ref[...]


def _merge(x, hn, z, hf, wbm, wbf, wo, ln2, wrh, wrl, br, tm):
    t = x.shape[0]
    row = lambda i: (i, 0)
    fix = lambda i: (0, 0)
    full = pl.BlockSpec((D_MODEL, D_MODEL), fix)
    return pl.pallas_call(
        _merge_kernel,
        grid=(t // tm,),
        in_specs=[
            pl.BlockSpec((tm, D_MODEL), row),
            pl.BlockSpec((tm, D_MODEL), row),
            pl.BlockSpec((tm, D_MODEL), lambda i: (i, ZB_MO)),
            pl.BlockSpec((tm, D_MODEL), row),
            pl.BlockSpec((tm, D_MODEL), lambda i: (i, ZB_GA)),
            pl.BlockSpec((tm, D_MODEL), lambda i: (i, ZB_GB)),
            full, full, full,
            pl.BlockSpec((1, D_MODEL), fix),
            pl.BlockSpec((D_MODEL, LANES), fix),
            pl.BlockSpec((D_MODEL, LANES), fix),
            pl.BlockSpec((1, LANES), fix),
        ],
        out_specs=[
            pl.BlockSpec((tm, D_MODEL), row),
            pl.BlockSpec((tm * TOKEN_SUBLANES, LANES), row),
            pl.BlockSpec((tm, LANES), row),
        ],
        out_shape=[
            jax.ShapeDtypeStruct((t, D_MODEL), F32),
            jax.ShapeDtypeStruct((t * TOKEN_SUBLANES, LANES), F32),
            jax.ShapeDtypeStruct((t, LANES), F32),
        ],
        compiler_params=_cparams(("parallel",)),
        name="merge",
    )(x, hn, z, hf, z, z, wbm, wbf, wo, ln2, wrh, wrl, br)


def _route_kernel(n_valid, lg_ref, tril_ref, ids_ref, wts_ref, rank_ref, cnt_ref, carry_scr):
    i = pl.program_id(0)

    @pl.when(i == 0)
    def _():
        carry_scr[...] = jnp.zeros_like(carry_scr)

    tm = lg_ref.shape[0]
    lane = lax.broadcasted_iota(I32, (tm, LANES), 1)
    lanef = lane.astype(F32)
    lg = jnp.where(lane < N_EXPERTS, lg_ref[...], -jnp.inf)
    vals, idxs, hots = [], [], []
    for _ in range(TOP_K):
        mx = jnp.max(lg, axis=1, keepdims=True)
        idx = jnp.min(jnp.where(lg == mx, lanef, float(LANES)), axis=1, keepdims=True)
        hot = lanef == idx
        lg = jnp.where(hot, -jnp.inf, lg)
        vals.append(mx)
        idxs.append(idx)
        hots.append(hot)
    es = [jnp.exp(v - vals[0]) for v in vals]
    tot = functools.reduce(lambda a, b: a + b, es)
    sel = functools.reduce(lambda a, b: a | b, hots)
    real = i * tm + lax.broadcasted_iota(I32, (tm, LANES), 0) < n_valid
    a01 = jnp.where(sel & real, 1.0, 0.0)
    before = _dot(tril_ref[...], a01.astype(BF16)) + carry_scr[0:1, :]
    carry_scr[...] = jnp.broadcast_to(carry_scr[0:1, :] + jnp.sum(a01, axis=0, keepdims=True), carry_scr.shape)
    ids = jnp.zeros((tm, LANES), I32)
    wts = jnp.zeros((tm, LANES), F32)
    rank = jnp.zeros((tm, LANES), I32)
    for kk in range(TOP_K):
        r = jnp.sum(jnp.where(hots[kk], before, 0.0), axis=1, keepdims=True)
        ids = jnp.where(lane == kk, idxs[kk].astype(I32), ids)
        wts = jnp.where(lane == kk, es[kk] / tot, wts)
        rank = jnp.where(lane == kk, r.astype(I32), rank)
    ids_ref[...] = ids
    wts_ref[...] = wts
    rank_ref[...] = rank
    cnt_ref[...] = carry_scr[...].astype(I32)


def _route(logits, tril_strict, tm, n_valid):
    t = logits.shape[0]
    row = lambda i: (i, 0)
    return pl.pallas_call(
        functools.partial(_route_kernel, n_valid),
        grid=(t // tm,),
        in_specs=[pl.BlockSpec((tm, LANES), row), pl.BlockSpec((tm, tm), lambda i: (0, 0))],
        out_specs=[pl.BlockSpec((tm, LANES), row)] * 3 + [pl.BlockSpec((8, LANES), lambda i: (0, 0))],
        out_shape=[
            jax.ShapeDtypeStruct((t, LANES), I32),
            jax.ShapeDtypeStruct((t, LANES), F32),
            jax.ShapeDtypeStruct((t, LANES), I32),
            jax.ShapeDtypeStruct((8, LANES), I32),
        ],
        scratch_shapes=[pltpu.VMEM((8, LANES), F32)],
        compiler_params=_cparams(("arbitrary",)),
        name="route",
    )(logits, tril_strict)


def _tile_rows(i):
    return pl.ds(pl.multiple_of(i * TOKEN_SUBLANES, TOKEN_SUBLANES), TOKEN_SUBLANES)


def _dispatch_kernel(pad_lo, pad_hi, n_pad, pos_ref, t_ref, xs_ref, zero_scr, sem, zsem):
    tc = COMB_TILE

    def issue(r, carry):
        src = t_ref.at[_tile_rows(r), :]
        for kk in range(TOP_K):
            pltpu.make_async_copy(src, xs_ref.at[_tile_rows(pos_ref[0, kk, r]), :], sem).start()
        return carry

    lax.fori_loop(0, tc, issue, 0, unroll=2)

    @pl.when(pl.program_id(0) == 0)
    def _():
        zero_scr[...] = jnp.zeros_like(zero_scr)

        def per_expert(e, carry):
            def per_slot(s, c2):
                pltpu.make_async_copy(zero_scr, xs_ref.at[_tile_rows(s), :], zsem).start()
                return c2
            return lax.fori_loop(pad_lo[e], pad_hi[e], per_slot, carry)

        lax.fori_loop(0, pad_lo.shape[0], per_expert, 0)

        def drain(s, carry):
            pltpu.make_async_copy(zero_scr, xs_ref.at[pl.ds(0, TOKEN_SUBLANES), :], zsem).wait()
            return carry

        lax.fori_loop(0, n_pad[0], drain, 0)

    for kk in range(TOP_K):
        pltpu.make_async_copy(t_ref, xs_ref.at[pl.ds(0, tc * TOKEN_SUBLANES), :], sem).wait()


def _dispatch(pad_lo, pad_hi, n_pad, pos3, t_all, r_pad):
    tiles = pos3.shape[0]
    tc = COMB_TILE
    grid_spec = pltpu.PrefetchScalarGridSpec(
        num_scalar_prefetch=3,
        grid=(tiles,),
        in_specs=[
            pl.BlockSpec((1, TOP_K, tc), lambda i, lo, hi, n: (i, 0, 0), memory_space=pltpu.SMEM),
            pl.BlockSpec((tc * TOKEN_SUBLANES, LANES), lambda i, lo, hi, n: (i, 0)),
        ],
        out_specs=pl.BlockSpec(memory_space=pl.ANY),
        scratch_shapes=[pltpu.VMEM((TOKEN_SUBLANES, LANES), F32), pltpu.SemaphoreType.DMA, pltpu.SemaphoreType.DMA],
    )
    return pl.pallas_call(
        _dispatch_kernel,
        grid_spec=grid_spec,
        out_shape=jax.ShapeDtypeStruct((r_pad * TOKEN_SUBLANES, LANES), F32),
        compiler_params=_cparams(("arbitrary",)),
        name="dispatch",
    )(pad_lo, pad_hi, n_pad, pos3, t_all)


def _experts_kernel(te_ref, nu_ref, xs_ref, w1_ref, b1_ref, w2_ref, b2_ref, ys_ref, w1b_scr, w2b_scr):
    j = pl.program_id(0)
    prev = te_ref[jnp.maximum(j - 1, 0)]

    @pl.when((j == 0) | (te_ref[j] != prev))
    def _():
        w1b_scr[...] = w1_ref[0].astype(BF16)
        w2b_scr[...] = w2_ref[0].astype(BF16)

    @pl.when(j < nu_ref[0])
    def _():
        x = _load_token_tiles(xs_ref, EXP_TILE)
        hcat = _dot(x.astype(BF16), w1b_scr[...]) + b1_ref[0]
        g = jnp.minimum(hcat[:, :D_FF], SWIGLU_LIMIT)
        u = jnp.clip(hcat[:, D_FF:], -SWIGLU_LIMIT, SWIGLU_LIMIT)
        a = g * jax.nn.sigmoid(SWIGLU_ALPHA * g) * (u + 1.0)
        _store_token_tiles(ys_ref, _dot(a.astype(BF16), w2b_scr[...]) + b2_ref[0])

    @pl.when(j >= nu_ref[0])
    def _():
        ys_ref[...] = jnp.zeros_like(ys_ref)


def _experts(tile_expert, n_used, xs, w1, b1, w2, b2):
    r = xs.shape[0] // TOKEN_SUBLANES
    tm = EXP_TILE
    grid_spec = pltpu.PrefetchScalarGridSpec(
        num_scalar_prefetch=2,
        grid=(r // tm,),
        in_specs=[
            pl.BlockSpec((tm * TOKEN_SUBLANES, LANES), lambda j, te, nu: (jnp.minimum(j, nu[0] - 1), 0)),
            pl.BlockSpec((1, D_MODEL, 2 * D_FF), lambda j, te, nu: (te[j], 0, 0)),
            pl.BlockSpec((1, 1, 2 * D_FF), lambda j, te, nu: (te[j], 0, 0)),
            pl.BlockSpec((1, D_FF, D_MODEL), lambda j, te, nu: (te[j], 0, 0)),
            pl.BlockSpec((1, 1, D_MODEL), lambda j, te, nu: (te[j], 0, 0)),
        ],
        out_specs=pl.BlockSpec((tm * TOKEN_SUBLANES, LANES), lambda j, te, nu: (j, 0)),
        scratch_shapes=[
            pltpu.VMEM((D_MODEL, 2 * D_FF), BF16),
            pltpu.VMEM((D_FF, D_MODEL), BF16),
        ],
    )
    return pl.pallas_call(
        _experts_kernel,
        grid_spec=grid_spec,
        out_shape=jax.ShapeDtypeStruct((r * TOKEN_SUBLANES, LANES), F32),
        compiler_params=_cparams(("arbitrary",)),
        name="experts",
    )(tile_expert, n_used, xs, w1, b1.reshape(N_EXPERTS, 1, 2 * D_FF), w2, b2.reshape(N_EXPERTS, 1, D_MODEL))


def _combine_kernel(pos_ref, ys_ref, x1_ref, wts_ref, lnf_ref, y_ref, rows_scr, sem):
    tc = x1_ref.shape[0]

    def issue(r, carry):
        for kk in range(TOP_K):
            pltpu.make_async_copy(ys_ref.at[_tile_rows(pos_ref[0, kk, r]), :],
                                  rows_scr.at[kk, _tile_rows(r), :], sem).start()
        return carry

    lax.fori_loop(0, tc, issue, 0, unroll=2)
    for kk in range(TOP_K):
        pltpu.make_async_copy(ys_ref.at[pl.ds(0, tc * TOKEN_SUBLANES), :], rows_scr.at[kk], sem).wait()
    w = wts_ref[...]
    acc = x1_ref[...]
    for kk in range(TOP_K):
        acc = acc + w[:, kk:kk + 1] * _load_token_tiles(rows_scr.at[kk], tc)
    y_ref[...] = acc * lax.rsqrt(jnp.mean(acc * acc, axis=-1, keepdims=True) + EPS) * lnf_ref[...]


def _combine(pos3, ys, x1, wts, lnf):
    t = x1.shape[0]
    tc = COMB_TILE
    row = lambda i: (i, 0)
    return pl.pallas_call(
        _combine_kernel,
        grid=(t // tc,),
        in_specs=[
            pl.BlockSpec((1, TOP_K, tc), lambda i: (i, 0, 0), memory_space=pltpu.SMEM),
            pl.BlockSpec(memory_space=pl.ANY),
            pl.BlockSpec((tc, D_MODEL), row),
            pl.BlockSpec((tc, LANES), row),
            pl.BlockSpec((1, D_MODEL), lambda i: (0, 0)),
        ],
        out_specs=pl.BlockSpec((tc, D_MODEL), row),
        out_shape=jax.ShapeDtypeStruct((t, D_MODEL), F32),
        scratch_shapes=[pltpu.VMEM((TOP_K, tc * TOKEN_SUBLANES, LANES), F32), pltpu.SemaphoreType.DMA],
        compiler_params=_cparams(("arbitrary",)),
        name="combine",
    )(pos3, ys, x1, wts, lnf)


def _consts():
    r64 = np.zeros((D_MODEL, LANES), np.float32)
    e64 = np.zeros((LANES, D_MODEL), np.float32)
    for h in range(F_HEADS):
        r64[h * F_DH:(h + 1) * F_DH, h] = 1.0 / F_DH
        e64[h, h * F_DH:(h + 1) * F_DH] = 1.0
    return jnp.asarray(r64, BF16), jnp.asarray(e64, BF16), jnp.asarray(e64, F32)


def _tril(n, strict=False):
    return jnp.asarray(np.tril(np.ones((n, n), np.float32), -1 if strict else 0), BF16)


def _round_up(a, b):
    return (a + b - 1) // b * b


def _moe_layer(t_all, x1_all, lg_all, w1, b1, w2, b2, lnf):
    tmoe = x1_all.shape[0]
    t_route = _round_up(tmoe, ROW_TILE)
    lg_pad = jnp.concatenate([lg_all, jnp.zeros((t_route - tmoe, LANES), F32)], axis=0)
    ids, wts, rank, cnt = _route(lg_pad, _tril(ROW_TILE, strict=True), ROW_TILE, tmoe)
    counts = cnt[0, :N_EXPERTS]
    pcounts = (counts + EXP_TILE - 1) // EXP_TILE * EXP_TILE
    ends = jnp.cumsum(pcounts)
    starts = ends - pcounts
    pos = starts[ids[:tmoe, :TOP_K]] + rank[:tmoe, :TOP_K]
    pos3 = pos.reshape(tmoe // COMB_TILE, COMB_TILE, TOP_K).transpose(0, 2, 1)
    r_pad = _round_up(TOP_K * tmoe + N_EXPERTS * (EXP_TILE - 1), EXP_TILE)
    n_tiles = r_pad // EXP_TILE
    tile_start = jnp.arange(n_tiles, dtype=I32) * EXP_TILE
    tile_expert = jnp.minimum(jnp.sum((ends[None, :] <= tile_start[:, None]).astype(I32), axis=1), N_EXPERTS - 1)
    n_used = (ends[-1] // EXP_TILE).astype(I32)[None]
    pad_lo = jnp.concatenate([starts + counts, ends[-1:]]).astype(I32)
    pad_hi = jnp.concatenate([ends, jnp.full((1,), r_pad, ends.dtype)]).astype(I32)
    n_pad = jnp.sum(pad_hi - pad_lo).astype(I32)[None]
    xsrt = _dispatch(pad_lo, pad_hi, n_pad, pos3, t_all, r_pad)
    ysrt = _experts(tile_expert, n_used, xsrt, w1, b1, w2, b2)
    return _combine(pos3, ysrt, x1_all, wts, lnf)


def kernel(x_prompt, x_sample, cache_k, cache_v, cache_logf, state_C, state_n, state_m, page_table, meta_tokens, ln1, w_in, mlstm_b_i, mlstm_b_f, mlstm_head_norm, fox_b_f, fox_q_norm, fox_k_norm, w_branch_mlstm, w_branch_fox, w_out, ln2, w_router, b_router, w_exp_in, b_exp_in, w_exp_out, b_exp_out, ln_final):
    depth = w_in.shape[0]
    b, seq, _ = x_prompt.shape
    db, ds, _ = x_sample.shape
    assert ds == 1 and seq % ATT_BLOCK == 0 and db % MSTEP_SEQS == 0
    l_true = seq + N_META
    lp = seq + ATT_BLOCK
    tp = b * lp
    n_phys, pg = cache_k.shape[1], cache_k.shape[2]
    r64, e64, e64f = _consts()

    xp = jnp.concatenate([jnp.zeros((b, PAD_FRONT, D_MODEL), F32),
                          jnp.broadcast_to(meta_tokens[None].astype(F32), (b, N_META, D_MODEL)),
                          x_prompt], axis=1).reshape(tp, D_MODEL)
    xs = x_sample.reshape(db, D_MODEL)

    outs = {k: [] for k in ("kp", "vp", "lfp", "ks", "vs", "lfs", "cp", "np", "mp", "cs", "ns", "ms")}
    offs = np.cumsum((0,) + (M_HEADS * M_DK, M_HEADS * M_DK, M_HEADS * M_DV, M_HEADS * M_DV, M_HEADS, M_HEADS,
                             F_HEADS * F_DH, F_HEADS * F_DH, F_HEADS * F_DH, F_HEADS, 2 * D_MODEL))
    seg = lambda w, i: w[:, offs[i]:offs[i + 1]]
    for l in range(depth):
        w = w_in[l]
        gates = seg(w, 10)
        wmain = jnp.stack([jnp.concatenate([seg(w, 0), seg(w, 1)], axis=1), seg(w, 2), seg(w, 3), seg(w, 6),
                           seg(w, 7), seg(w, 8), gates[:, :D_MODEL], gates[:, D_MODEL:]]).astype(BF16)
        wsm = jnp.concatenate([seg(w, 4), seg(w, 5), seg(w, 9),
                               jnp.zeros((D_MODEL, LANES - G_END), F32)], axis=1)
        wsh = wsm.astype(BF16)
        wsl = (wsm - wsh.astype(F32)).astype(BF16)
        bias = jnp.concatenate([mlstm_b_i[l], mlstm_b_f[l], fox_b_f[l], jnp.zeros((LANES - G_END,), F32)])[None]
        qn = jnp.tile(fox_q_norm[l], F_HEADS)[None]
        kn = jnp.tile(fox_k_norm[l], F_HEADS)[None]
        hnw = mlstm_head_norm[l][None]
        ln1l = ln1[l][None]
        wbm = w_branch_mlstm[l].astype(BF16)
        wbf = w_branch_fox[l].astype(BF16)
        wo = w_out[l].astype(BF16)
        wr = jnp.concatenate([w_router[l], jnp.zeros((D_MODEL, LANES - N_EXPERTS), F32)], axis=1)
        wrh = wr.astype(BF16)
        wrl = (wr - wrh.astype(F32)).astype(BF16)
        br = jnp.concatenate([b_router[l], jnp.zeros((LANES - N_EXPERTS,), F32)])[None]

        zp, kpt32, vpt32, vpt16, gsp = _proj(xp, ln1l, wmain, wsh, wsl, bias, qn, kn, r64, e64, ROW_TILE)
        zp3 = zp.reshape(b, lp, N_ZB * D_MODEL)
        hnp, kap, qap, c_p, n_p, m_p = _mlstm_prompt(zp3, gsp.reshape(b, lp, LANES), hnw, _tril(M_CHUNK))
        hfp = _fox_prompt(zp3, qap, kap, vpt16)
        x1p, tpn, lgp = _merge(xp, hnp.reshape(tp, D_MODEL), zp, hfp.reshape(tp, D_MODEL), wbm, wbf, wo,
                               ln2[l][None], wrh, wrl, br, ROW_TILE)
        unpad = lambda a: jnp.transpose(a.reshape(F_HEADS, F_DH, b, lp)[:, :, :, PAD_FRONT:], (2, 3, 0, 1))
        outs["kp"].append(unpad(kpt32))
        outs["vp"].append(unpad(vpt32))
        outs["lfp"].append(gsp.reshape(b, lp, LANES)[:, PAD_FRONT:, G_FF:G_END])
        outs["cp"].append(c_p)
        outs["np"].append(n_p[:, :, 0, :])
        outs["mp"].append(m_p[:, :, 0, 0])

        zs, kst32, vst32, _, gss = _proj(xs, ln1l, wmain, wsh, wsl, bias, qn, kn, r64, e64, db)
        vs32 = vst32.T
        m_in = jnp.concatenate([state_m[l], jnp.zeros((db, LANES - M_HEADS), F32)], axis=1)
        hns, c_s, n_s, m_s = _mlstm_step(zs, gss, m_in, state_n[l].reshape(db, M_HEADS * M_DK), state_C[l], hnw)
        n_s = n_s.reshape(db, M_HEADS, M_DK)
        q3 = zs[:, ZB_FQ * D_MODEL:(ZB_FQ + 1) * D_MODEL].astype(F32).reshape(db, 1, D_MODEL)
        kn3 = zs[:, ZB_FK * D_MODEL:(ZB_FK + 1) * D_MODEL].astype(F32).reshape(db, 1, D_MODEL)
        hfs = _fox_decode(page_table, q3, kn3, vs32.reshape(db, 1, D_MODEL),
                          jnp.broadcast_to(gss[:, G_FF:G_END, None], (db, F_HEADS, LANES)),
                          jnp.transpose(cache_k[l], (0, 2, 3, 1)), jnp.transpose(cache_v[l], (0, 2, 3, 1)),
                          jnp.transpose(cache_logf[l], (0, 2, 1)))
        x1s, tsn, lgs = _merge(xs, hns, zs, hfs.reshape(db, D_MODEL).astype(BF16), wbm, wbf, wo,
                               ln2[l][None], wrh, wrl, br, db)
        outs["ks"].append(jnp.transpose(kst32.reshape(F_HEADS, F_DH, db), (2, 0, 1)).reshape(db, 1, F_HEADS, F_DH))
        outs["vs"].append(jnp.transpose(vst32.reshape(F_HEADS, F_DH, db), (2, 0, 1)).reshape(db, 1, F_HEADS, F_DH))
        outs["lfs"].append(gss[:, G_FF:G_END].reshape(db, 1, F_HEADS))
        outs["cs"].append(c_s)
        outs["ns"].append(n_s)
        outs["ms"].append(m_s[:, :M_HEADS])

        tt = tp + db
        tmoe = tt
        assert tmoe % COMB_TILE == 0
        t_all = jnp.concatenate([tpn, tsn], axis=0)
        x1_all = jnp.concatenate([x1p, x1s], axis=0)
        lg_all = jnp.concatenate([lgp, lgs], axis=0)
        lnf = ln_final[None] if l == depth - 1 else jnp.ones((1, D_MODEL), F32)
        y_all = _moe_layer(t_all, x1_all, lg_all, w_exp_in[l], b_exp_in[l], w_exp_out[l], b_exp_out[l], lnf)
        if l < depth - 1:
            raise NotImplementedError("deeper stacks need the un-normalised residual stream as well")

    y_prompt = y_all[:tp].reshape(b, lp, D_MODEL)[:, ATT_BLOCK:]
    y_sample = y_all[tp:tp + db].reshape(db, 1, D_MODEL)
    st = lambda k: jnp.stack(outs[k])
    return (y_prompt, y_sample, st("kp"), st("vp"), st("lfp"), st("ks"), st("vs"), st("lfs"),
            st("cp"), st("np"), st("mp"), st("cs"), st("ns"), st("ms"))
```

```python
import functools

import numpy as np
import jax
import jax.numpy as jnp
from jax import lax
from jax.experimental import pallas as pl
from jax.experimental.pallas import tpu as pltpu

F32 = jnp.float32
BF16 = jnp.bfloat16
I32 = jnp.int32

D_MODEL = 1024
N_META = 16
M_HEADS, M_DK, M_DV = 4, 128, 256
F_HEADS, F_DH = 16, 64
N_EXPERTS, TOP_K, D_FF = 32, 4, 1024
GATE_CAP = 15.0
SWIGLU_LIMIT = 7.0
SWIGLU_ALPHA = 1.702
EPS = 1e-6
ATTN_SCALE = F_DH ** -0.5
LOG2E = 1.4426950408889634

LANES = 128
ATT_BLOCK = 256
ATT_SUB = 128
PAD_FRONT = ATT_BLOCK - N_META
M_CHUNK = 128
ROW_TILE = 512
EXP_TILE = 256
COMB_TILE = 128
PAD_TILES = ATT_BLOCK // COMB_TILE
VMEM_LIMIT = 56 * 1024 * 1024

G_IG = 0
G_LF = M_HEADS
G_FF = 2 * M_HEADS
G_END = G_FF + F_HEADS

WB_MQK, WB_MV, WB_MO, WB_FQ, WB_FK, WB_FV, WB_GA, WB_GB = range(8)
N_WB = 8
ZB_MQK, ZB_MV, ZB_MO, ZB_FQ, ZB_FK, ZB_GA, ZB_GB = range(7)
N_ZB = 7


def _dot(a, b):
    return jnp.dot(a, b, preferred_element_type=F32)


def _dot_nt(a, b):
    return lax.dot_general(a, b, (((1,), (1,)), ((), ())), preferred_element_type=F32)


def _dot_tn(a, b):
    return lax.dot_general(a, b, (((0,), (0,)), ((), ())), preferred_element_type=F32)


def _split2(x):
    hi = x.astype(BF16)
    lo = (x - hi.astype(F32)).astype(BF16)
    return hi, lo


def _split3(x):
    a = x.astype(BF16)
    r = x - a.astype(F32)
    b = r.astype(BF16)
    c = (r - b.astype(F32)).astype(BF16)
    return a, b, c


def _dot_exact_rhs01(parts, m01):
    acc = _dot(parts[0], m01)
    for p in parts[1:]:
        acc = acc + _dot(p, m01)
    return acc


def _log_sigmoid(x):
    return jnp.minimum(x, 0.0) - jnp.log1p(jnp.exp(-jnp.abs(x)))


def _cparams(sem):
    return pltpu.CompilerParams(dimension_semantics=sem, vmem_limit_bytes=VMEM_LIMIT)


def _head_norm(y, g, r_ref, e_ref):
    ms = _dot((y * y).astype(BF16), r_ref[...])
    rs = lax.rsqrt(ms + EPS)
    rsx = _dot_exact_rhs01(_split2(rs), e_ref[...])
    return y * rsx * g


def _proj_kernel(x_ref, ln_ref, w_ref, wsh_ref, wsl_ref, bias_ref, qn_ref, kn_ref, r_ref, e_ref,
                 z_ref, kt32_ref, vt32_ref, vt16_ref, gs_ref):
    x = x_ref[...]
    ms = jnp.mean(x * x, axis=-1, keepdims=True)
    h = x * lax.rsqrt(ms + EPS) * ln_ref[...]
    hh, hl = _split2(h)
    wsh = wsh_ref[...]
    g = _dot(hh, wsh) + _dot(hl, wsh) + _dot(hh, wsl_ref[...]) + bias_ref[...]
    lane = lax.broadcasted_iota(I32, g.shape, 1)
    cap = GATE_CAP * jnp.tanh(g / GATE_CAP)
    ls = _log_sigmoid(jnp.where(lane < G_FF, cap, g))
    gs_ref[...] = jnp.where(lane < G_LF, cap, jnp.where(lane < G_END, ls, 0.0))

    def y_of(wb):
        return _dot(hh, w_ref[wb])

    def put(zb, val):
        z_ref[:, zb * D_MODEL:(zb + 1) * D_MODEL] = val.astype(BF16)

    y = y_of(WB_MQK)
    col = lax.broadcasted_iota(I32, y.shape, 1)
    put(ZB_MQK, jnp.where(col < M_HEADS * M_DK, y * (M_DK ** -0.5), y))
    put(ZB_MV, y_of(WB_MV))
    put(ZB_MO, jax.nn.sigmoid(y_of(WB_MO)))
    put(ZB_FQ, _head_norm(y_of(WB_FQ), qn_ref[...], r_ref, e_ref) * (ATTN_SCALE * LOG2E))
    kn = _head_norm(y_of(WB_FK), kn_ref[...], r_ref, e_ref)
    kt32_ref[...] = kn.T
    put(ZB_FK, kn)
    vt = y_of(WB_FV).T
    vt32_ref[...] = vt
    vt16_ref[...] = vt.astype(BF16)
    put(ZB_GA, jax.nn.sigmoid(y_of(WB_GA)))
    put(ZB_GB, jax.nn.sigmoid(y_of(WB_GB)))


def _proj(x, ln, wmain, wsh, wsl, bias, qn, kn, r64, e64, tm):
    t = x.shape[0]
    row = lambda i: (i, 0)
    col = lambda i: (0, i)
    fix = lambda i: (0, 0)
    return pl.pallas_call(
        _proj_kernel,
        grid=(t // tm,),
        in_specs=[
            pl.BlockSpec((tm, D_MODEL), row),
            pl.BlockSpec((1, D_MODEL), fix),
            pl.BlockSpec((N_WB, D_MODEL, D_MODEL), lambda i: (0, 0, 0), pipeline_mode=pl.Buffered(1)),
            pl.BlockSpec((D_MODEL, LANES), fix),
            pl.BlockSpec((D_MODEL, LANES), fix),
            pl.BlockSpec((1, LANES), fix),
            pl.BlockSpec((1, D_MODEL), fix),
            pl.BlockSpec((1, D_MODEL), fix),
            pl.BlockSpec((D_MODEL, LANES), fix),
            pl.BlockSpec((LANES, D_MODEL), fix),
        ],
        out_specs=[
            pl.BlockSpec((tm, N_ZB * D_MODEL), row),
            pl.BlockSpec((D_MODEL, tm), col),
            pl.BlockSpec((D_MODEL, tm), col),
            pl.BlockSpec((D_MODEL, tm), col),
            pl.BlockSpec((tm, LANES), row),
        ],
        out_shape=[
            jax.ShapeDtypeStruct((t, N_ZB * D_MODEL), BF16),
            jax.ShapeDtypeStruct((D_MODEL, t), F32),
            jax.ShapeDtypeStruct((D_MODEL, t), F32),
            jax.ShapeDtypeStruct((D_MODEL, t), BF16),
            jax.ShapeDtypeStruct((t, LANES), F32),
        ],
        compiler_params=_cparams(("parallel",)),
        name="proj",
    )(x, ln, wmain, wsh, wsl, bias, qn, kn, r64, e64)


def _mlstm_kernel(qk_ref, v_ref, gs_ref, hnw_ref, tril_ref, place_ref,
                  hn_ref, ka_ref, qa_ref, c_out, n_out, m_out,
                  c_scr, n_scr, m_scr, crow_scr):
    c = pl.program_id(1)
    lc = M_CHUNK

    @pl.when(c == 0)
    def _():
        c_scr[...] = jnp.zeros_like(c_scr)
        n_scr[...] = jnp.zeros_like(n_scr)
        m_scr[...] = jnp.zeros_like(m_scr)
        crow_scr[...] = jnp.zeros_like(crow_scr)

    g = gs_ref[0]
    lane = lax.broadcasted_iota(I32, (lc, LANES), 1)
    pos = c * lc + lax.broadcasted_iota(I32, (lc, LANES), 0)
    valid = pos >= PAD_FRONT
    is_ig = lane < G_LF
    gsum = jnp.where(valid & jnp.logical_not(is_ig), g, 0.0)
    bcum = _dot_exact_rhs01_lhs(tril_ref[...], _split3(gsum))
    comb = jnp.where(is_ig, jnp.where(valid, g, -jnp.inf), bcum)
    comb_t = comb.T

    row_t = lax.broadcasted_iota(I32, (lc, lc), 0)
    col_s = lax.broadcasted_iota(I32, (lc, lc), 1)
    causal = col_s <= row_t

    qk = qk_ref[0]
    vv = v_ref[0]
    for h in range(M_HEADS):
        q = qk[:, h * M_DK:(h + 1) * M_DK]
        k = qk[:, (M_HEADS + h) * M_DK:(M_HEADS + h + 1) * M_DK]
        v = vv[:, h * M_DV:(h + 1) * M_DV]
        m_prev = m_scr[h, 0:1, 0:1]
        b_col = bcum[:, G_LF + h:G_LF + h + 1]
        ig_col = comb[:, G_IG + h:G_IG + h + 1]
        b_row = comb_t[G_LF + h:G_LF + h + 1, :]
        ig_row = comb_t[G_IG + h:G_IG + h + 1, :]

        log_d = jnp.where(causal, b_col - b_row + ig_row, -jnp.inf)
        inter = b_col + m_prev
        m_row = jnp.maximum(inter, jnp.max(log_d, axis=1, keepdims=True))
        s = _dot_nt(q, k) * jnp.exp(log_d - m_row)
        w_int = jnp.exp(inter - m_row)
        cmat = c_scr[h]
        nvec = n_scr[h, 0:1, :]
        num = _dot(s.astype(BF16), v) + w_int * _dot_nt(q, cmat.astype(BF16))
        den = jnp.sum(s, axis=1, keepdims=True) + w_int * jnp.sum(q.astype(F32) * nvec, axis=1, keepdims=True)
        hh = num / jnp.maximum(jnp.abs(den), jnp.exp(-m_row))
        hn = hh * lax.rsqrt(jnp.mean(hh * hh, axis=1, keepdims=True) + EPS) * hnw_ref[:, h * M_DV:(h + 1) * M_DV]
        hn_ref[0, :, h * M_DV:(h + 1) * M_DV] = hn.astype(BF16)

        b_last = b_col[lc - 1:lc, :]
        m_new = jnp.maximum(b_last + m_prev, jnp.max(b_last - b_row + ig_row, axis=1, keepdims=True))
        decay = jnp.exp(b_last + m_prev - m_new)
        ws_col = jnp.exp(b_last - b_col + ig_col - m_new)
        kw = k.astype(F32) * ws_col
        c_scr[h] = decay * cmat + _dot_tn(v, kw.astype(BF16))
        n_scr[h] = jnp.broadcast_to(decay * nvec + jnp.sum(kw, axis=0, keepdims=True), n_scr.shape[1:])
        m_scr[h] = jnp.broadcast_to(m_new, m_scr.shape[1:])

    cg = bcum + crow_scr[0:1, :]
    crow_scr[...] = jnp.broadcast_to(cg[lc - 1:lc, :], crow_scr.shape)
    p0, p1, p2 = _split3(cg * LOG2E)
    p0 = jnp.where(lane == 0, jnp.ones_like(p0), p0)
    ka_ref[0] = (_dot(p0, place_ref[0]) + _dot(p1, place_ref[1]) + _dot(p2, place_ref[2])).astype(BF16)
    qa_ref[0] = (_dot(p0, place_ref[3]) + _dot(p1, place_ref[4]) + _dot(p2, place_ref[5])).astype(BF16)

    @pl.when(c == pl.num_programs(1) - 1)
    def _():
        c_out[0] = c_scr[...]
        n_out[0] = n_scr[...]
        m_out[0] = m_scr[...]


def _dot_exact_rhs01_lhs(m01, parts):
    acc = _dot(m01, parts[0])
    for p in parts[1:]:
        acc = acc + _dot(m01, p)
    return acc


BIAS_LANES = 6


def _bias_placement():
    pm = np.zeros((6, LANES, LANES), np.float32)
    for h in range(F_HEADS):
        base = BIAS_LANES * h
        for t in range(3):
            pm[t, G_FF + h, base + t] = -1.0
            pm[0, 0, base + 3 + t] = 1.0
            pm[3 + t, G_FF + h, base + 3 + t] = 1.0
            pm[3, 0, base + t] = 1.0
    return jnp.asarray(pm, BF16)


def _mlstm_prompt(z3, gs3, hnw, tril):
    b, lp, _ = z3.shape
    lc = M_CHUNK
    nc = lp // lc
    return pl.pallas_call(
        _mlstm_kernel,
        grid=(b, nc),
        in_specs=[
            pl.BlockSpec((1, lc, D_MODEL), lambda i, c: (i, c, ZB_MQK)),
            pl.BlockSpec((1, lc, D_MODEL), lambda i, c: (i, c, ZB_MV)),
            pl.BlockSpec((1, lc, LANES), lambda i, c: (i, c, 0)),
            pl.BlockSpec((1, D_MODEL), lambda i, c: (0, 0)),
            pl.BlockSpec((lc, lc), lambda i, c: (0, 0)),
            pl.BlockSpec((6, LANES, LANES), lambda i, c: (0, 0, 0)),
        ],
        out_specs=[
            pl.BlockSpec((1, lc, D_MODEL), lambda i, c: (i, c, 0)),
            pl.BlockSpec((1, lc, LANES), lambda i, c: (i, c, 0)),
            pl.BlockSpec((1, lc, LANES), lambda i, c: (i, c, 0)),
            pl.BlockSpec((1, M_HEADS, M_DV, M_DK), lambda i, c: (i, 0, 0, 0)),
            pl.BlockSpec((1, M_HEADS, 8, M_DK), lambda i, c: (i, 0, 0, 0)),
            pl.BlockSpec((1, M_HEADS, 8, LANES), lambda i, c: (i, 0, 0, 0)),
        ],
        out_shape=[
            jax.ShapeDtypeStruct((b, lp, D_MODEL), BF16),
            jax.ShapeDtypeStruct((b, lp, LANES), BF16),
            jax.ShapeDtypeStruct((b, lp, LANES), BF16),
            jax.ShapeDtypeStruct((b, M_HEADS, M_DV, M_DK), F32),
            jax.ShapeDtypeStruct((b, M_HEADS, 8, M_DK), F32),
            jax.ShapeDtypeStruct((b, M_HEADS, 8, LANES), F32),
        ],
        scratch_shapes=[
            pltpu.VMEM((M_HEADS, M_DV, M_DK), F32),
            pltpu.VMEM((M_HEADS, 8, M_DK), F32),
            pltpu.VMEM((M_HEADS, 8, LANES), F32),
            pltpu.VMEM((8, LANES), F32),
        ],
        compiler_params=_cparams(("parallel", "arbitrary")),
        name="mlstm",
    )(z3, z3, gs3, hnw, tril, _bias_placement())


MSTEP_SEQS = 8


def _mstep_kernel(qk_ref, v_ref, gs_ref, m_ref, n_ref, c_ref, hnw_ref,
                  hn_ref, c_out, n_out, m_out):
    sb = MSTEP_SEQS
    qk = qk_ref[...]
    vv = v_ref[...]
    g = gs_ref[...]
    row = lax.broadcasted_iota(I32, (sb, 1), 0)
    m_new_all = jnp.zeros((sb, LANES), F32)
    lane = lax.broadcasted_iota(I32, (sb, LANES), 1)
    for h in range(M_HEADS):
        q = qk[:, h * M_DK:(h + 1) * M_DK]
        k = qk[:, (M_HEADS + h) * M_DK:(M_HEADS + h + 1) * M_DK]
        v = vv[:, h * M_DV:(h + 1) * M_DV]
        ig = g[:, G_IG + h:G_IG + h + 1]
        lf = g[:, G_LF + h:G_LF + h + 1]
        m_prev = m_ref[:, h:h + 1]
        inter = lf + m_prev
        m_row = jnp.maximum(inter, ig)
        d = jnp.exp(ig - m_row)
        w_int = jnp.exp(inter - m_row)
        qf = q.astype(F32)
        kf = k.astype(F32)
        s = jnp.sum(qf * kf, axis=1, keepdims=True) * d
        nvec = n_ref[:, h * M_DK:(h + 1) * M_DK]
        den = s + w_int * jnp.sum(qf * nvec, axis=1, keepdims=True)
        kw = (kf * d).astype(BF16)
        hrows = []
        for i in range(sb):
            cmat = c_ref[i, h]
            sel = row == i
            qi = jnp.where(sel, q, jnp.zeros_like(q))
            qc = _dot_nt(qi, cmat.astype(BF16))
            hrows.append(jnp.where(sel, qc, 0.0))
            vi = jnp.where(sel, v, jnp.zeros_like(v))
            c_out[i, h] = w_int[i:i + 1, :] * cmat + _dot_tn(vi, kw)
        qc_all = functools.reduce(lambda a, b: a + b, hrows)
        num = s * v.astype(F32) + w_int * qc_all
        hh = num / jnp.maximum(jnp.abs(den), jnp.exp(-m_row))
        hn = hh * lax.rsqrt(jnp.mean(hh * hh, axis=1, keepdims=True) + EPS) * hnw_ref[:, h * M_DV:(h + 1) * M_DV]
        hn_ref[:, h * M_DV:(h + 1) * M_DV] = hn.astype(BF16)
        n_out[:, h * M_DK:(h + 1) * M_DK] = w_int * nvec + kf * d
        m_new_all = jnp.where(lane == h, m_row, m_new_all)
    m_out[...] = m_new_all


def _mlstm_step(zs, gss, m_in, n_in, c_in, hnw):
    db = zs.shape[0]
    sb = MSTEP_SEQS
    return pl.pallas_call(
        _mstep_kernel,
        grid=(db // sb,),
        in_specs=[
            pl.BlockSpec((sb, D_MODEL), lambda i: (i, ZB_MQK)),
            pl.BlockSpec((sb, D_MODEL), lambda i: (i, ZB_MV)),
            pl.BlockSpec((sb, LANES), lambda i: (i, 0)),
            pl.BlockSpec((sb, LANES), lambda i: (i, 0)),
            pl.BlockSpec((sb, M_HEADS * M_DK), lambda i: (i, 0)),
            pl.BlockSpec((sb, M_HEADS, M_DV, M_DK), lambda i: (i, 0, 0, 0)),
            pl.BlockSpec((1, D_MODEL), lambda i: (0, 0)),
        ],
        out_specs=[
            pl.BlockSpec((sb, D_MODEL), lambda i: (i, 0)),
            pl.BlockSpec((sb, M_HEADS, M_DV, M_DK), lambda i: (i, 0, 0, 0)),
            pl.BlockSpec((sb, M_HEADS * M_DK), lambda i: (i, 0)),
            pl.BlockSpec((sb, LANES), lambda i: (i, 0)),
        ],
        out_shape=[
            jax.ShapeDtypeStruct((db, D_MODEL), BF16),
            jax.ShapeDtypeStruct((db, M_HEADS, M_DV, M_DK), F32),
            jax.ShapeDtypeStruct((db, M_HEADS * M_DK), F32),
            jax.ShapeDtypeStruct((db, LANES), F32),
        ],
        compiler_params=_cparams(("parallel",)),
        name="mstep",
    )(zs, zs, gss, m_in, n_in, c_in, hnw)


def _fox_kernel(qi_tab, ki_tab, q_ref, qa_ref, k_ref, ka_ref, vt_ref, o_ref,
                qb_scr, acc_scr, m_scr, l_scr, a_scr, st_scr, pt_scr):
    step = pl.program_id(1)
    qi = qi_tab[step]
    ki = ki_tab[step]
    tq = tk = ATT_BLOCK

    @pl.when(ki == 0)
    def _():
        acc_scr[...] = jnp.zeros_like(acc_scr)
        m_scr[...] = jnp.full_like(m_scr, -jnp.inf)
        l_scr[...] = jnp.zeros_like(l_scr)
        lane = lax.broadcasted_iota(I32, (tq, LANES), 1)
        qa = qa_ref[0]
        for h in range(F_HEADS):
            q2 = q_ref[0, :, (h // 2) * LANES:(h // 2 + 1) * LANES]
            own = (lane >= (h % 2) * F_DH) & (lane < (h % 2 + 1) * F_DH)
            bias = (lane >= h * BIAS_LANES) & (lane < (h + 1) * BIAS_LANES)
            qb_scr[h] = jnp.concatenate([jnp.where(own, q2, jnp.zeros_like(q2)),
                                         jnp.where(bias, qa, jnp.zeros_like(qa))], axis=1)

    def sweep(masked):
        ka = ka_ref[0]
        if masked:
            key = ki * tk + lax.broadcasted_iota(I32, (tk, tq), 0)
            qry = qi * tq + lax.broadcasted_iota(I32, (tk, tq), 1)
            visible = (key <= qry) & ((key >= PAD_FRONT) | (qry < PAD_FRONT))
        def score_stage(h):
            kb = jnp.concatenate([k_ref[0, :, (h // 2) * LANES:(h // 2 + 1) * LANES], ka], axis=1)
            st_scr[h] = _dot_nt(kb, qb_scr[h])

        def softmax_stage(h):
            for c0 in range(0, tq, LANES):
                cs = slice(c0, c0 + LANES)
                def scores(r0):
                    st = st_scr[h, r0:r0 + ATT_SUB, cs]
                    return jnp.where(visible[r0:r0 + ATT_SUB, cs], st, -jnp.inf) if masked else st

                m_prev = m_scr[h, 0:1, cs]
                m_next = m_prev
                for r0 in range(0, tk, ATT_SUB):
                    m_next = jnp.maximum(m_next, jnp.max(scores(r0), axis=0, keepdims=True))
                alpha = jnp.exp2(m_prev - m_next)
                l_new = alpha * l_scr[h, 0:1, cs]
                for r0 in range(0, tk, ATT_SUB):
                    pt = jnp.exp2(scores(r0) - m_next)
                    l_new = l_new + jnp.sum(pt, axis=0, keepdims=True)
                    pt_scr[h, r0:r0 + ATT_SUB, cs] = pt.astype(BF16)
                l_scr[h, :, cs] = jnp.broadcast_to(l_new, (8, LANES))
                m_scr[h, :, cs] = jnp.broadcast_to(m_next, (8, LANES))
                a_scr[h, :, cs] = jnp.broadcast_to(alpha, (8, LANES))
        def value_stage(h):
            acc_scr[h] = a_scr[h, 0:1, :] * acc_scr[h] + _dot(vt_ref[h * F_DH:(h + 1) * F_DH, :], pt_scr[h])

        for stage in (score_stage, softmax_stage, value_stage):
            for h in range(F_HEADS):
                stage(h)

    edge = (ki == qi) | (ki == 0)

    @pl.when(edge)
    def _():
        sweep(True)

    @pl.when(jnp.logical_not(edge))
    def _():
        sweep(False)

    @pl.when(ki == qi)
    def _():
        for p in range(F_HEADS // 2):
            o2 = jnp.concatenate([acc_scr[2 * p] * (1.0 / l_scr[2 * p, 0:1, :]),
                                  acc_scr[2 * p + 1] * (1.0 / l_scr[2 * p + 1, 0:1, :])], axis=0)
            o_ref[0, :, p * LANES:(p + 1) * LANES] = o2.T.astype(BF16)


def _fox_prompt(z3, qa, ka, vt16):
    b, lp, _ = z3.shape
    blk = ATT_BLOCK
    nb = lp // blk
    qi_tab = np.concatenate([np.full((i + 1,), i, np.int32) for i in range(nb)])
    ki_tab = np.concatenate([np.arange(i + 1, dtype=np.int32) for i in range(nb)])
    grid_spec = pltpu.PrefetchScalarGridSpec(
        num_scalar_prefetch=2,
        grid=(b, len(qi_tab)),
        in_specs=[
            pl.BlockSpec((1, blk, D_MODEL), lambda i, s, qt, kt: (i, qt[s], ZB_FQ)),
            pl.BlockSpec((1, blk, LANES), lambda i, s, qt, kt: (i, qt[s], 0)),
            pl.BlockSpec((1, blk, D_MODEL), lambda i, s, qt, kt: (i, kt[s], ZB_FK)),
            pl.BlockSpec((1, blk, LANES), lambda i, s, qt, kt: (i, kt[s], 0)),
            pl.BlockSpec((D_MODEL, blk), lambda i, s, qt, kt: (0, i * nb + kt[s])),
        ],
        out_specs=pl.BlockSpec((1, blk, D_MODEL), lambda i, s, qt, kt: (i, qt[s], 0)),
        scratch_shapes=[
            pltpu.VMEM((F_HEADS, blk, 2 * LANES), BF16),
            pltpu.VMEM((F_HEADS, F_DH, blk), F32),
            pltpu.VMEM((F_HEADS, 8, blk), F32),
            pltpu.VMEM((F_HEADS, 8, blk), F32),
            pltpu.VMEM((F_HEADS, 8, blk), F32),
            pltpu.VMEM((F_HEADS, blk, blk), F32),
            pltpu.VMEM((F_HEADS, blk, blk), BF16),
        ],
    )
    return pl.pallas_call(
        _fox_kernel,
        grid_spec=grid_spec,
        out_shape=jax.ShapeDtypeStruct((b, lp, D_MODEL), BF16),
        compiler_params=_cparams(("parallel", "arbitrary")),
        name="fox",
    )(jnp.asarray(qi_tab), jnp.asarray(ki_tab), z3, qa, z3, ka, vt16)


DEC_PAGES = 16


def _lane_bcast_cols(row):
    full = jnp.broadcast_to(row, (LANES, D_MODEL)).T
    return full.reshape(F_HEADS, F_DH, LANES)


def _per_head(x):
    return x.reshape(F_HEADS, 1, LANES)


def _dec_kernel(pt_ref, q_ref, kn_ref, vn_ref, lfn_ref, triu_ref, ones_ref, *rest):
    np_ = DEC_PAGES
    k_refs, v_refs, lf_refs = rest[:np_], rest[np_:2 * np_], rest[2 * np_:3 * np_]
    o_ref, qb_scr, acc_scr, m_scr, l_scr, c_scr = rest[3 * np_:]
    p = pl.program_id(1)

    @pl.when(p == 0)
    def _():
        qb_scr[...] = _lane_bcast_cols(q_ref[0])
        acc_scr[...] = jnp.zeros_like(acc_scr)
        m_scr[...] = jnp.full_like(m_scr, -jnp.inf)
        l_scr[...] = jnp.zeros_like(l_scr)
        c_scr[...] = jnp.zeros_like(c_scr)

    qb = qb_scr[...]
    carry = c_scr[...]
    us = []
    for i in range(np_):
        cum = _dot_exact_rhs01(_split3(lf_refs[i][0]), triu_ref[...]) + carry
        carry = jnp.broadcast_to(cum[:, LANES - 1:LANES], carry.shape)
        s = jnp.sum(qb * k_refs[i][0], axis=1)
        us.append(s - cum * LOG2E)
    c_scr[...] = carry
    m_prev = m_scr[...]
    m_cur = functools.reduce(jnp.maximum, [jnp.max(u, axis=1, keepdims=True) for u in us])
    m_next = jnp.maximum(m_prev, m_cur)
    alpha = jnp.exp2(m_prev - m_next)
    acc = _per_head(alpha) * acc_scr[...]
    l_new = alpha * l_scr[...]
    for i in range(np_):
        pr = jnp.exp2(us[i] - m_next)
        l_new = l_new + jnp.sum(pr, axis=1, keepdims=True)
        acc = acc + _per_head(pr) * v_refs[i][0]
    acc_scr[...] = acc
    l_scr[...] = l_new
    m_scr[...] = m_next

    @pl.when(p == pl.num_programs(1) - 1)
    def _():
        s_new = jnp.sum(qb * _lane_bcast_cols(kn_ref[0]), axis=1)
        u_new = s_new - (carry + lfn_ref[0]) * LOG2E
        m_fin = jnp.maximum(m_next, u_new)
        a_fin = jnp.exp2(m_next - m_fin)
        pn = jnp.exp2(u_new - m_fin)
        l_fin = a_fin * l_new + pn
        tot = (_per_head(a_fin) * acc + _per_head(pn * (1.0 / LANES)) * _lane_bcast_cols(vn_ref[0])) / _per_head(l_fin)
        out8 = _dot_nt_exact_lhs01(ones_ref[...], _split3(tot.reshape(D_MODEL, LANES)))
        o_ref[0] = out8[0:1, :]


def _dot_nt_exact_lhs01(m01, parts):
    acc = _dot_nt(m01, parts[0])
    for part in parts[1:]:
        acc = acc + _dot_nt(m01, part)
    return acc


def _fox_decode(page_table, q3, kn3, vn3, lfn3, cache_kt, cache_vt, cache_lft):
    db, npg = page_table.shape
    pg = cache_kt.shape[3]
    assert pg == LANES and npg % DEC_PAGES == 0
    row3 = lambda s, p, pt: (s, 0, 0)
    fix = lambda s, p, pt: (0, 0)

    def page(i, nd):
        return lambda s, p, pt: (pt[s, p * DEC_PAGES + i],) + (0,) * nd

    in_specs = [
        pl.BlockSpec((1, 1, D_MODEL), row3),
        pl.BlockSpec((1, 1, D_MODEL), row3),
        pl.BlockSpec((1, 1, D_MODEL), row3),
        pl.BlockSpec((1, F_HEADS, LANES), row3),
        pl.BlockSpec((LANES, LANES), fix),
        pl.BlockSpec((8, LANES), fix),
    ]
    in_specs += [pl.BlockSpec((1, F_HEADS, F_DH, LANES), page(i, 3)) for i in range(DEC_PAGES)]
    in_specs += [pl.BlockSpec((1, F_HEADS, F_DH, LANES), page(i, 3)) for i in range(DEC_PAGES)]
    in_specs += [pl.BlockSpec((1, F_HEADS, LANES), page(i, 2)) for i in range(DEC_PAGES)]
    grid_spec = pltpu.PrefetchScalarGridSpec(
        num_scalar_prefetch=1,
        grid=(db, npg // DEC_PAGES),
        in_specs=in_specs,
        out_specs=pl.BlockSpec((1, 1, D_MODEL), row3),
        scratch_shapes=[
            pltpu.VMEM((F_HEADS, F_DH, LANES), F32),
            pltpu.VMEM((F_HEADS, F_DH, LANES), F32),
            pltpu.VMEM((F_HEADS, LANES), F32),
            pltpu.VMEM((F_HEADS, LANES), F32),
            pltpu.VMEM((F_HEADS, LANES), F32),
        ],
    )
    triu = jnp.asarray(np.triu(np.ones((LANES, LANES), np.float32)), BF16)
    ones = jnp.ones((8, LANES), BF16)
    return pl.pallas_call(
        _dec_kernel,
        grid_spec=grid_spec,
        out_shape=jax.ShapeDtypeStruct((db, 1, D_MODEL), F32),
        compiler_params=_cparams(("parallel", "arbitrary")),
        name="dec",
    )(page_table, q3, kn3, vn3, lfn3, triu, ones,
      *([cache_kt] * DEC_PAGES), *([cache_vt] * DEC_PAGES), *([cache_lft] * DEC_PAGES))


TOKEN_SUBLANES = D_MODEL // LANES


def _store_token_tiles(ref, x):
    n = x.shape[0]
    for g in range(TOKEN_SUBLANES):
        ref[pl.ds(g, n, stride=TOKEN_SUBLANES), :] = x[:, g * LANES:(g + 1) * LANES]


def _load_token_tiles(ref, n):
    return jnp.concatenate([ref[pl.ds(g, n, stride=TOKEN_SUBLANES), :] for g in range(TOKEN_SUBLANES)], axis=1)


def _merge_kernel(x_ref, hn_ref, so_ref, hf_ref, ga_ref, gb_ref, wbm_ref, wbf_ref, wo_ref, ln2_ref,
                  wrh_ref, wrl_ref, br_ref, x1_ref, t_ref, lg_ref):
    hm = hn_ref[...] * so_ref[...]
    ya = _dot(hm, wbm_ref[...])
    yb = _dot(hf_ref[...], wbf_ref[...])
    u = ga_ref[...].astype(F32) * ya + gb_ref[...].astype(F32) * yb
    x1 = x_ref[...] + _dot(u.astype(BF16), wo_ref[...])
    x1_ref[...] = x1
    t = x1 * lax.rsqrt(jnp.mean(x1 * x1, axis=-1, keepdims=True) + EPS) * ln2_ref[...]
    _store_token_tiles(t_ref, t)
    th, tl = _split2(t)
    wrh = wrh_ref[...]
    lg_ref[...] = _dot(th, wrh) + _dot(tl, wrh) + _dot(th, wrl_ref[...]) + br_ref[...]


def _merge(x, hn, z, hf, wbm, wbf, wo, ln2, wrh, wrl, br, tm):
    t = x.shape[0]
    row = lambda i: (i, 0)
    fix = lambda i: (0, 0)
    full = pl.BlockSpec((D_MODEL, D_MODEL), fix)
    return pl.pallas_call(
        _merge_kernel,
        grid=(t // tm,),
        in_specs=[
            pl.BlockSpec((tm, D_MODEL), row),
            pl.BlockSpec((tm, D_MODEL), row),
            pl.BlockSpec((tm, D_MODEL), lambda i: (i, ZB_MO)),
            pl.BlockSpec((tm, D_MODEL), row),
            pl.BlockSpec((tm, D_MODEL), lambda i: (i, ZB_GA)),
            pl.BlockSpec((tm, D_MODEL), lambda i: (i, ZB_GB)),
            full, full, full,
            pl.BlockSpec((1, D_MODEL), fix),
            pl.BlockSpec((D_MODEL, LANES), fix),
            pl.BlockSpec((D_MODEL, LANES), fix),
            pl.BlockSpec((1, LANES), fix),
        ],
        out_specs=[
            pl.BlockSpec((tm, D_MODEL), row),
            pl.BlockSpec((tm * TOKEN_SUBLANES, LANES), row),
            pl.BlockSpec((tm, LANES), row),
        ],
        out_shape=[
            jax.ShapeDtypeStruct((t, D_MODEL), F32),
            jax.ShapeDtypeStruct((t * TOKEN_SUBLANES, LANES), F32),
            jax.ShapeDtypeStruct((t, LANES), F32),
        ],
        compiler_params=_cparams(("parallel",)),
        name="merge",
    )(x, hn, z, hf, z, z, wbm, wbf, wo, ln2, wrh, wrl, br)


def _route_kernel(n_valid, lg_ref, tril_ref, ids_ref, wts_ref, rank_ref, cnt_ref, carry_scr):
    i = pl.program_id(0)

    @pl.when(i == 0)
    def _():
        carry_scr[...] = jnp.zeros_like(carry_scr)

    tm = lg_ref.shape[0]
    lane = lax.broadcasted_iota(I32, (tm, LANES), 1)
    lanef = lane.astype(F32)
    lg = jnp.where(lane < N_EXPERTS, lg_ref[...], -jnp.inf)
    vals, idxs, hots = [], [], []
    for _ in range(TOP_K):
        mx = jnp.max(lg, axis=1, keepdims=True)
        idx = jnp.min(jnp.where(lg == mx, lanef, float(LANES)), axis=1, keepdims=True)
        hot = lanef == idx
        lg = jnp.where(hot, -jnp.inf, lg)
        vals.append(mx)
        idxs.append(idx)
        hots.append(hot)
    es = [jnp.exp(v - vals[0]) for v in vals]
    tot = functools.reduce(lambda a, b: a + b, es)
    sel = functools.reduce(lambda a, b: a | b, hots)
    real = i * tm + lax.broadcasted_iota(I32, (tm, LANES), 0) < n_valid
    a01 = jnp.where(sel & real, 1.0, 0.0)
    before = _dot(tril_ref[...], a01.astype(BF16)) + carry_scr[0:1, :]
    carry_scr[...] = jnp.broadcast_to(carry_scr[0:1, :] + jnp.sum(a01, axis=0, keepdims=True), carry_scr.shape)
    ids = jnp.zeros((tm, LANES), I32)
    wts = jnp.zeros((tm, LANES), F32)
    rank = jnp.zeros((tm, LANES), I32)
    for kk in range(TOP_K):
        r = jnp.sum(jnp.where(hots[kk], before, 0.0), axis=1, keepdims=True)
        ids = jnp.where(lane == kk, idxs[kk].astype(I32), ids)
        wts = jnp.where(lane == kk, es[kk] / tot, wts)
        rank = jnp.where(lane == kk, r.astype(I32), rank)
    ids_ref[...] = ids
    wts_ref[...] = wts
    rank_ref[...] = rank
    cnt_ref[...] = carry_scr[...].astype(I32)


def _route(logits, tril_strict, tm, n_valid):
    t = logits.shape[0]
    row = lambda i: (i, 0)
    return pl.pallas_call(
        functools.partial(_route_kernel, n_valid),
        grid=(t // tm,),
        in_specs=[pl.BlockSpec((tm, LANES), row), pl.BlockSpec((tm, tm), lambda i: (0, 0))],
        out_specs=[pl.BlockSpec((tm, LANES), row)] * 3 + [pl.BlockSpec((8, LANES), lambda i: (0, 0))],
        out_shape=[
            jax.ShapeDtypeStruct((t, LANES), I32),
            jax.ShapeDtypeStruct((t, LANES), F32),
            jax.ShapeDtypeStruct((t, LANES), I32),
            jax.ShapeDtypeStruct((8, LANES), I32),
        ],
        scratch_shapes=[pltpu.VMEM((8, LANES), F32)],
        compiler_params=_cparams(("arbitrary",)),
        name="route",
    )(logits, tril_strict)


def _tile_rows(i):
    return pl.ds(pl.multiple_of(i * TOKEN_SUBLANES, TOKEN_SUBLANES), TOKEN_SUBLANES)


def _dispatch_kernel(ntp, pad_lo, pad_hi, n_pad, pos_ref, tp_ref, ts_ref, xs_ref, zero_scr, sem, zsem):
    tc = COMB_TILE
    i = pl.program_id(0)

    def scatter_from(t_ref):
        def issue(r, carry):
            src = t_ref.at[_tile_rows(r), :]
            for kk in range(TOP_K):
                pltpu.make_async_copy(src, xs_ref.at[_tile_rows(pos_ref[0, kk, r]), :], sem).start()
            return carry

        lax.fori_loop(0, tc, issue, 0, unroll=2)

    @pl.when(i < ntp)
    def _():
        scatter_from(tp_ref)

    @pl.when(i >= ntp)
    def _():
        scatter_from(ts_ref)

    @pl.when(i == 0)
    def _():
        zero_scr[...] = jnp.zeros_like(zero_scr)

        def per_expert(e, carry):
            def per_slot(s, c2):
                pltpu.make_async_copy(zero_scr, xs_ref.at[_tile_rows(s), :], zsem).start()
                return c2
            return lax.fori_loop(pad_lo[e], pad_hi[e], per_slot, carry)

        lax.fori_loop(0, pad_lo.shape[0], per_expert, 0)

        def drain(s, carry):
            pltpu.make_async_copy(zero_scr, xs_ref.at[pl.ds(0, TOKEN_SUBLANES), :], zsem).wait()
            return carry

        lax.fori_loop(0, n_pad[0], drain, 0)

    for kk in range(TOP_K):
        pltpu.make_async_copy(tp_ref, xs_ref.at[pl.ds(0, tc * TOKEN_SUBLANES), :], sem).wait()


def _dispatch(pad_lo, pad_hi, n_pad, pos3, t_p, t_s, r_pad):
    tc = COMB_TILE
    ntp = t_p.shape[0] // (tc * TOKEN_SUBLANES)
    assert t_s.shape[0] == tc * TOKEN_SUBLANES and pos3.shape[0] == ntp + 1
    grid_spec = pltpu.PrefetchScalarGridSpec(
        num_scalar_prefetch=3,
        grid=(ntp + 1,),
        in_specs=[
            pl.BlockSpec((1, TOP_K, tc), lambda i, lo, hi, n: (i, 0, 0), memory_space=pltpu.SMEM),
            pl.BlockSpec((tc * TOKEN_SUBLANES, LANES), lambda i, lo, hi, n: (jnp.minimum(i, ntp - 1), 0)),
            pl.BlockSpec((tc * TOKEN_SUBLANES, LANES), lambda i, lo, hi, n: (0, 0)),
        ],
        out_specs=pl.BlockSpec(memory_space=pl.ANY),
        scratch_shapes=[pltpu.VMEM((TOKEN_SUBLANES, LANES), F32), pltpu.SemaphoreType.DMA, pltpu.SemaphoreType.DMA],
    )
    return pl.pallas_call(
        functools.partial(_dispatch_kernel, ntp),
        grid_spec=grid_spec,
        out_shape=jax.ShapeDtypeStruct((r_pad * TOKEN_SUBLANES, LANES), F32),
        compiler_params=_cparams(("arbitrary",)),
        name="dispatch",
    )(pad_lo, pad_hi, n_pad, pos3, t_p, t_s)


def _experts_kernel(te_ref, nu_ref, xs_ref, w1_ref, b1_ref, w2_ref, b2_ref, ys_ref, w1b_scr, w2b_scr):
    j = pl.program_id(0)
    prev = te_ref[jnp.maximum(j - 1, 0)]

    @pl.when((j == 0) | (te_ref[j] != prev))
    def _():
        w1b_scr[...] = w1_ref[0].astype(BF16)
        w2b_scr[...] = w2_ref[0].astype(BF16)

    @pl.when(j < nu_ref[0])
    def _():
        x = _load_token_tiles(xs_ref, EXP_TILE)
        hcat = _dot(x.astype(BF16), w1b_scr[...]) + b1_ref[0]
        g = jnp.minimum(hcat[:, :D_FF], SWIGLU_LIMIT)
        u = jnp.clip(hcat[:, D_FF:], -SWIGLU_LIMIT, SWIGLU_LIMIT)
        a = g * jax.nn.sigmoid(SWIGLU_ALPHA * g) * (u + 1.0)
        _store_token_tiles(ys_ref, _dot(a.astype(BF16), w2b_scr[...]) + b2_ref[0])

    @pl.when(j >= nu_ref[0])
    def _():
        ys_ref[...] = jnp.zeros_like(ys_ref)


def _experts(tile_expert, n_used, xs, w1, b1, w2, b2):
    r = xs.shape[0] // TOKEN_SUBLANES
    tm = EXP_TILE
    grid_spec = pltpu.PrefetchScalarGridSpec(
        num_scalar_prefetch=2,
        grid=(r // tm,),
        in_specs=[
            pl.BlockSpec((tm * TOKEN_SUBLANES, LANES), lambda j, te, nu: (jnp.minimum(j, nu[0] - 1), 0)),
            pl.BlockSpec((1, D_MODEL, 2 * D_FF), lambda j, te, nu: (te[j], 0, 0)),
            pl.BlockSpec((1, 1, 2 * D_FF), lambda j, te, nu: (te[j], 0, 0)),
            pl.BlockSpec((1, D_FF, D_MODEL), lambda j, te, nu: (te[j], 0, 0)),
            pl.BlockSpec((1, 1, D_MODEL), lambda j, te, nu: (te[j], 0, 0)),
        ],
        out_specs=pl.BlockSpec((tm * TOKEN_SUBLANES, LANES), lambda j, te, nu: (j, 0)),
        scratch_shapes=[
            pltpu.VMEM((D_MODEL, 2 * D_FF), BF16),
            pltpu.VMEM((D_FF, D_MODEL), BF16),
        ],
    )
    return pl.pallas_call(
        _experts_kernel,
        grid_spec=grid_spec,
        out_shape=jax.ShapeDtypeStruct((r * TOKEN_SUBLANES, LANES), F32),
        compiler_params=_cparams(("arbitrary",)),
        name="experts",
    )(tile_expert, n_used, xs, w1, b1.reshape(N_EXPERTS, 1, 2 * D_FF), w2, b2.reshape(N_EXPERTS, 1, D_MODEL))


def _combine_kernel(ntp, tps, pos_ref, ys_ref, x1p_ref, x1s_ref, wts_ref, lnf_ref, yp_ref, ysm_ref, rows_scr, sem):
    tc = COMB_TILE
    i = pl.program_id(0)
    is_prompt = i < ntp
    wanted = jnp.logical_or(jnp.logical_not(is_prompt), lax.rem(i, tps) >= PAD_TILES)

    def finish(x1):
        w = wts_ref[...]
        acc = x1
        for kk in range(TOP_K):
            acc = acc + w[:, kk:kk + 1] * _load_token_tiles(rows_scr.at[kk], tc)
        return acc * lax.rsqrt(jnp.mean(acc * acc, axis=-1, keepdims=True) + EPS) * lnf_ref[...]

    @pl.when(wanted)
    def _():
        def issue(r, carry):
            for kk in range(TOP_K):
                pltpu.make_async_copy(ys_ref.at[_tile_rows(pos_ref[0, kk, r]), :],
                                      rows_scr.at[kk, _tile_rows(r), :], sem).start()
            return carry

        lax.fori_loop(0, tc, issue, 0, unroll=2)
        for kk in range(TOP_K):
            pltpu.make_async_copy(ys_ref.at[pl.ds(0, tc * TOKEN_SUBLANES), :], rows_scr.at[kk], sem).wait()

        @pl.when(is_prompt)
        def _():
            yp_ref[0] = finish(x1p_ref[...])

        @pl.when(jnp.logical_not(is_prompt))
        def _():
            ysm_ref[...] = finish(x1s_ref[...])


def _combine(pos3, ys, x1p, x1s, wts, lnf, b, lp):
    tc = COMB_TILE
    ntp = x1p.shape[0] // tc
    tps = lp // tc
    seq_tiles = tps - PAD_TILES
    assert x1s.shape[0] == tc and pos3.shape[0] == ntp + 1

    def yp_map(i):
        blk = jnp.where(i >= ntp, seq_tiles - 1, jnp.maximum(lax.rem(i, tps) - PAD_TILES, 0))
        return (jnp.minimum(i // tps, b - 1), blk, 0)

    return pl.pallas_call(
        functools.partial(_combine_kernel, ntp, tps),
        grid=(ntp + 1,),
        in_specs=[
            pl.BlockSpec((1, TOP_K, tc), lambda i: (i, 0, 0), memory_space=pltpu.SMEM),
            pl.BlockSpec(memory_space=pl.ANY),
            pl.BlockSpec((tc, D_MODEL), lambda i: (jnp.minimum(i, ntp - 1), 0)),
            pl.BlockSpec((tc, D_MODEL), lambda i: (0, 0)),
            pl.BlockSpec((tc, LANES), lambda i: (i, 0)),
            pl.BlockSpec((1, D_MODEL), lambda i: (0, 0)),
        ],
        out_specs=[
            pl.BlockSpec((1, tc, D_MODEL), yp_map),
            pl.BlockSpec((tc, D_MODEL), lambda i: (0, 0)),
        ],
        out_shape=[
            jax.ShapeDtypeStruct((b, seq_tiles * tc, D_MODEL), F32),
            jax.ShapeDtypeStruct((tc, D_MODEL), F32),
        ],
        scratch_shapes=[pltpu.VMEM((TOP_K, tc * TOKEN_SUBLANES, LANES), F32), pltpu.SemaphoreType.DMA],
        compiler_params=_cparams(("arbitrary",)),
        name="combine",
    )(pos3, ys, x1p, x1s, wts, lnf)


def _consts():
    r64 = np.zeros((D_MODEL, LANES), np.float32)
    e64 = np.zeros((LANES, D_MODEL), np.float32)
    for h in range(F_HEADS):
        r64[h * F_DH:(h + 1) * F_DH, h] = 1.0 / F_DH
        e64[h, h * F_DH:(h + 1) * F_DH] = 1.0
    return jnp.asarray(r64, BF16), jnp.asarray(e64, BF16), jnp.asarray(e64, F32)


def _tril(n, strict=False):
    return jnp.asarray(np.tril(np.ones((n, n), np.float32), -1 if strict else 0), BF16)


def _round_up(a, b):
    return (a + b - 1) // b * b


def _moe_layer(t_p, t_s, x1_p, x1_s, lg_all, w1, b1, w2, b2, lnf, b, lp):
    tmoe = x1_p.shape[0] + x1_s.shape[0]
    t_route = _round_up(tmoe, ROW_TILE)
    lg_pad = jnp.concatenate([lg_all, jnp.zeros((t_route - tmoe, LANES), F32)], axis=0)
    ids, wts, rank, cnt = _route(lg_pad, _tril(ROW_TILE, strict=True), ROW_TILE, tmoe)
    counts = cnt[0, :N_EXPERTS]
    pcounts = (counts + EXP_TILE - 1) // EXP_TILE * EXP_TILE
    ends = jnp.cumsum(pcounts)
    starts = ends - pcounts
    pos = starts[ids[:tmoe, :TOP_K]] + rank[:tmoe, :TOP_K]
    pos3 = pos.reshape(tmoe // COMB_TILE, COMB_TILE, TOP_K).transpose(0, 2, 1)
    r_pad = _round_up(TOP_K * tmoe + N_EXPERTS * (EXP_TILE - 1), EXP_TILE)
    n_tiles = r_pad // EXP_TILE
    tile_start = jnp.arange(n_tiles, dtype=I32) * EXP_TILE
    tile_expert = jnp.minimum(jnp.sum((ends[None, :] <= tile_start[:, None]).astype(I32), axis=1), N_EXPERTS - 1)
    n_used = (ends[-1] // EXP_TILE).astype(I32)[None]
    pad_lo = jnp.concatenate([starts + counts, ends[-1:]]).astype(I32)
    pad_hi = jnp.concatenate([ends, jnp.full((1,), r_pad, ends.dtype)]).astype(I32)
    n_pad = jnp.sum(pad_hi - pad_lo).astype(I32)[None]
    xsrt = _dispatch(pad_lo, pad_hi, n_pad, pos3, t_p, t_s, r_pad)
    ysrt = _experts(tile_expert, n_used, xsrt, w1, b1, w2, b2)
    return _combine(pos3, ysrt, x1_p, x1_s, wts, lnf, b, lp)


def kernel(x_prompt, x_sample, cache_k, cache_v, cache_logf, state_C, state_n, state_m, page_table, meta_tokens, ln1, w_in, mlstm_b_i, mlstm_b_f, mlstm_head_norm, fox_b_f, fox_q_norm, fox_k_norm, w_branch_mlstm, w_branch_fox, w_out, ln2, w_router, b_router, w_exp_in, b_exp_in, w_exp_out, b_exp_out, ln_final):
    depth = w_in.shape[0]
    b, seq, _ = x_prompt.shape
    db, ds, _ = x_sample.shape
    assert ds == 1 and seq % ATT_BLOCK == 0 and db % MSTEP_SEQS == 0
    l_true = seq + N_META
    lp = seq + ATT_BLOCK
    tp = b * lp
    n_phys, pg = cache_k.shape[1], cache_k.shape[2]
    r64, e64, e64f = _consts()

    xp = jnp.concatenate([jnp.zeros((b, PAD_FRONT, D_MODEL), F32),
                          jnp.broadcast_to(meta_tokens[None].astype(F32), (b, N_META, D_MODEL)),
                          x_prompt], axis=1).reshape(tp, D_MODEL)
    xs = x_sample.reshape(db, D_MODEL)

    outs = {k: [] for k in ("kp", "vp", "lfp", "ks", "vs", "lfs", "cp", "np", "mp", "cs", "ns", "ms")}
    offs = np.cumsum((0,) + (M_HEADS * M_DK, M_HEADS * M_DK, M_HEADS * M_DV, M_HEADS * M_DV, M_HEADS, M_HEADS,
                             F_HEADS * F_DH, F_HEADS * F_DH, F_HEADS * F_DH, F_HEADS, 2 * D_MODEL))
    seg = lambda w, i: w[:, offs[i]:offs[i + 1]]
    for l in range(depth):
        w = w_in[l]
        gates = seg(w, 10)
        wmain = jnp.stack([jnp.concatenate([seg(w, 0), seg(w, 1)], axis=1), seg(w, 2), seg(w, 3), seg(w, 6),
                           seg(w, 7), seg(w, 8), gates[:, :D_MODEL], gates[:, D_MODEL:]]).astype(BF16)
        wsm = jnp.concatenate([seg(w, 4), seg(w, 5), seg(w, 9),
                               jnp.zeros((D_MODEL, LANES - G_END), F32)], axis=1)
        wsh = wsm.astype(BF16)
        wsl = (wsm - wsh.astype(F32)).astype(BF16)
        bias = jnp.concatenate([mlstm_b_i[l], mlstm_b_f[l], fox_b_f[l], jnp.zeros((LANES - G_END,), F32)])[None]
        qn = jnp.tile(fox_q_norm[l], F_HEADS)[None]
        kn = jnp.tile(fox_k_norm[l], F_HEADS)[None]
        hnw = mlstm_head_norm[l][None]
        ln1l = ln1[l][None]
        wbm = w_branch_mlstm[l].astype(BF16)
        wbf = w_branch_fox[l].astype(BF16)
        wo = w_out[l].astype(BF16)
        wr = jnp.concatenate([w_router[l], jnp.zeros((D_MODEL, LANES - N_EXPERTS), F32)], axis=1)
        wrh = wr.astype(BF16)
        wrl = (wr - wrh.astype(F32)).astype(BF16)
        br = jnp.concatenate([b_router[l], jnp.zeros((LANES - N_EXPERTS,), F32)])[None]

        zp, kpt32, vpt32, vpt16, gsp = _proj(xp, ln1l, wmain, wsh, wsl, bias, qn, kn, r64, e64, ROW_TILE)
        zp3 = zp.reshape(b, lp, N_ZB * D_MODEL)
        hnp, kap, qap, c_p, n_p, m_p = _mlstm_prompt(zp3, gsp.reshape(b, lp, LANES), hnw, _tril(M_CHUNK))
        hfp = _fox_prompt(zp3, qap, kap, vpt16)
        x1p, tpn, lgp = _merge(xp, hnp.reshape(tp, D_MODEL), zp, hfp.reshape(tp, D_MODEL), wbm, wbf, wo,
                               ln2[l][None], wrh, wrl, br, ROW_TILE)
        unpad = lambda a: jnp.transpose(a.reshape(F_HEADS, F_DH, b, lp)[:, :, :, PAD_FRONT:], (2, 3, 0, 1))
        outs["kp"].append(unpad(kpt32))
        outs["vp"].append(unpad(vpt32))
        outs["lfp"].append(gsp.reshape(b, lp, LANES)[:, PAD_FRONT:, G_FF:G_END])
        outs["cp"].append(c_p)
        outs["np"].append(n_p[:, :, 0, :])
        outs["mp"].append(m_p[:, :, 0, 0])

        zs, kst32, vst32, _, gss = _proj(xs, ln1l, wmain, wsh, wsl, bias, qn, kn, r64, e64, db)
        vs32 = vst32.T
        m_in = jnp.concatenate([state_m[l], jnp.zeros((db, LANES - M_HEADS), F32)], axis=1)
        hns, c_s, n_s, m_s = _mlstm_step(zs, gss, m_in, state_n[l].reshape(db, M_HEADS * M_DK), state_C[l], hnw)
        n_s = n_s.reshape(db, M_HEADS, M_DK)
        q3 = zs[:, ZB_FQ * D_MODEL:(ZB_FQ + 1) * D_MODEL].astype(F32).reshape(db, 1, D_MODEL)
        kn3 = zs[:, ZB_FK * D_MODEL:(ZB_FK + 1) * D_MODEL].astype(F32).reshape(db, 1, D_MODEL)
        hfs = _fox_decode(page_table, q3, kn3, vs32.reshape(db, 1, D_MODEL),
                          jnp.broadcast_to(gss[:, G_FF:G_END, None], (db, F_HEADS, LANES)),
                          jnp.transpose(cache_k[l], (0, 2, 3, 1)), jnp.transpose(cache_v[l], (0, 2, 3, 1)),
                          jnp.transpose(cache_logf[l], (0, 2, 1)))
        x1s, tsn, lgs = _merge(xs, hns, zs, hfs.reshape(db, D_MODEL).astype(BF16), wbm, wbf, wo,
                               ln2[l][None], wrh, wrl, br, db)
        outs["ks"].append(jnp.transpose(kst32.reshape(F_HEADS, F_DH, db), (2, 0, 1)).reshape(db, 1, F_HEADS, F_DH))
        outs["vs"].append(jnp.transpose(vst32.reshape(F_HEADS, F_DH, db), (2, 0, 1)).reshape(db, 1, F_HEADS, F_DH))
        outs["lfs"].append(gss[:, G_FF:G_END].reshape(db, 1, F_HEADS))
        outs["cs"].append(c_s)
        outs["ns"].append(n_s)
        outs["ms"].append(m_s[:, :M_HEADS])

        if l < depth - 1:
            raise NotImplementedError("deeper stacks need the un-normalised residual stream as well")
        lg_all = jnp.concatenate([lgp, lgs], axis=0)
        y_prompt, y_sample = _moe_layer(tpn, tsn, x1p, x1s, lg_all, w_exp_in[l], b_exp_in[l], w_exp_out[l],
                                        b_exp_out[l], ln_final[None], b, lp)

    st = lambda k: jnp.stack(outs[k])
    return (y_prompt, y_sample.reshape(db, 1, D_MODEL), st("kp"), st("vp"), st("lfp"), st("ks"), st("vs"), st("lfs"),
            st("cp"), st("np"), st("mp"), st("cs"), st("ns"), st("ms"))
```

```python
import functools

import numpy as np
import jax
import jax.numpy as jnp
from jax import lax
from jax.experimental import pallas as pl
from jax.experimental.pallas import tpu as pltpu

F32 = jnp.float32
BF16 = jnp.bfloat16
I32 = jnp.int32

D_MODEL = 1024
N_META = 16
M_HEADS, M_DK, M_DV = 4, 128, 256
F_HEADS, F_DH = 16, 64
N_EXPERTS, TOP_K, D_FF = 32, 4, 1024
GATE_CAP = 15.0
SWIGLU_LIMIT = 7.0
SWIGLU_ALPHA = 1.702
EPS = 1e-6
ATTN_SCALE = F_DH ** -0.5
LOG2E = 1.4426950408889634

LANES = 128
ATT_BLOCK = 256
ATT_SUB = 128
PAD_FRONT = ATT_BLOCK - N_META
M_CHUNK = 256
MLSTM_SEQS = 1
ROW_TILE = 512
EXP_TILE = 256
COMB_TILE = 128
PAD_TILES = ATT_BLOCK // COMB_TILE
VMEM_LIMIT = 56 * 1024 * 1024

G_IG = 0
G_LF = M_HEADS
G_FF = 2 * M_HEADS
G_END = G_FF + F_HEADS

WB_MQK, WB_MV, WB_MO, WB_FQ, WB_FK, WB_FV, WB_GA, WB_GB = range(8)
N_WB = 8
ZB_MQK, ZB_MV, ZB_MO, ZB_FQ, ZB_FK, ZB_GA, ZB_GB = range(7)
N_ZB = 7


def _dot(a, b):
    return jnp.dot(a, b, preferred_element_type=F32)


def _dot_nt(a, b):
    return lax.dot_general(a, b, (((1,), (1,)), ((), ())), preferred_element_type=F32)


def _dot_tn(a, b):
    return lax.dot_general(a, b, (((0,), (0,)), ((), ())), preferred_element_type=F32)


def _split2(x):
    hi = x.astype(BF16)
    lo = (x - hi.astype(F32)).astype(BF16)
    return hi, lo


def _split3(x):
    a = x.astype(BF16)
    r = x - a.astype(F32)
    b = r.astype(BF16)
    c = (r - b.astype(F32)).astype(BF16)
    return a, b, c


def _dot_exact_rhs01(parts, m01):
    acc = _dot(parts[0], m01)
    for p in parts[1:]:
        acc = acc + _dot(p, m01)
    return acc


def _log_sigmoid(x):
    return jnp.minimum(x, 0.0) - jnp.log1p(jnp.exp(-jnp.abs(x)))


def _cparams(sem):
    return pltpu.CompilerParams(dimension_semantics=sem, vmem_limit_bytes=VMEM_LIMIT)


def _head_norm(y, g, r_ref, e_ref):
    ms = _dot((y * y).astype(BF16), r_ref[...])
    rs = lax.rsqrt(ms + EPS)
    rsx = _dot_exact_rhs01(_split2(rs), e_ref[...])
    return y * rsx * g


def _proj_kernel(x_ref, ln_ref, w_ref, wsh_ref, wsl_ref, bias_ref, qn_ref, kn_ref, r_ref, e_ref,
                 z_ref, kt32_ref, vt32_ref, vt16_ref, gs_ref):
    x = x_ref[...]
    ms = jnp.mean(x * x, axis=-1, keepdims=True)
    h = x * lax.rsqrt(ms + EPS) * ln_ref[...]
    hh, hl = _split2(h)
    wsh = wsh_ref[...]
    g = _dot(hh, wsh) + _dot(hl, wsh) + _dot(hh, wsl_ref[...]) + bias_ref[...]
    lane = lax.broadcasted_iota(I32, g.shape, 1)
    cap = GATE_CAP * jnp.tanh(g / GATE_CAP)
    ls = _log_sigmoid(jnp.where(lane < G_FF, cap, g))
    gs_ref[...] = jnp.where(lane < G_LF, cap, jnp.where(lane < G_END, ls, 0.0))

    def y_of(wb):
        return _dot(hh, w_ref[wb])

    def put(zb, val):
        z_ref[:, zb * D_MODEL:(zb + 1) * D_MODEL] = val.astype(BF16)

    y = y_of(WB_MQK)
    col = lax.broadcasted_iota(I32, y.shape, 1)
    put(ZB_MQK, jnp.where(col < M_HEADS * M_DK, y * (M_DK ** -0.5), y))
    put(ZB_MV, y_of(WB_MV))
    put(ZB_MO, jax.nn.sigmoid(y_of(WB_MO)))
    put(ZB_FQ, _head_norm(y_of(WB_FQ), qn_ref[...], r_ref, e_ref) * (ATTN_SCALE * LOG2E))
    kn = _head_norm(y_of(WB_FK), kn_ref[...], r_ref, e_ref)
    kt32_ref[...] = kn.T
    put(ZB_FK, kn)
    vt = y_of(WB_FV).T
    vt32_ref[...] = vt
    vt16_ref[...] = vt.astype(BF16)
    put(ZB_GA, jax.nn.sigmoid(y_of(WB_GA)))
    put(ZB_GB, jax.nn.sigmoid(y_of(WB_GB)))


def _proj(x, ln, wmain, wsh, wsl, bias, qn, kn, r64, e64, tm):
    t = x.shape[0]
    row = lambda i: (i, 0)
    col = lambda i: (0, i)
    fix = lambda i: (0, 0)
    return pl.pallas_call(
        _proj_kernel,
        grid=(t // tm,),
        in_specs=[
            pl.BlockSpec((tm, D_MODEL), row),
            pl.BlockSpec((1, D_MODEL), fix),
            pl.BlockSpec((N_WB, D_MODEL, D_MODEL), lambda i: (0, 0, 0), pipeline_mode=pl.Buffered(1)),
            pl.BlockSpec((D_MODEL, LANES), fix),
            pl.BlockSpec((D_MODEL, LANES), fix),
            pl.BlockSpec((1, LANES), fix),
            pl.BlockSpec((1, D_MODEL), fix),
            pl.BlockSpec((1, D_MODEL), fix),
            pl.BlockSpec((D_MODEL, LANES), fix),
            pl.BlockSpec((LANES, D_MODEL), fix),
        ],
        out_specs=[
            pl.BlockSpec((tm, N_ZB * D_MODEL), row),
            pl.BlockSpec((D_MODEL, tm), col),
            pl.BlockSpec((D_MODEL, tm), col),
            pl.BlockSpec((D_MODEL, tm), col),
            pl.BlockSpec((tm, LANES), row),
        ],
        out_shape=[
            jax.ShapeDtypeStruct((t, N_ZB * D_MODEL), BF16),
            jax.ShapeDtypeStruct((D_MODEL, t), F32),
            jax.ShapeDtypeStruct((D_MODEL, t), F32),
            jax.ShapeDtypeStruct((D_MODEL, t), BF16),
            jax.ShapeDtypeStruct((t, LANES), F32),
        ],
        compiler_params=_cparams(("parallel",)),
        name="proj",
    )(x, ln, wmain, wsh, wsl, bias, qn, kn, r64, e64)


def _mlstm_kernel(qk_ref, v_ref, gs_ref, hnw_ref, tril_ref, place_ref,
                  hn_ref, ka_ref, qa_ref, c_out, n_out, m_out,
                  c_scr, n_scr, m_scr, crow_scr):
    c = pl.program_id(1)
    lc = M_CHUNK

    @pl.when(c == 0)
    def _():
        c_scr[...] = jnp.zeros_like(c_scr)
        n_scr[...] = jnp.zeros_like(n_scr)
        m_scr[...] = jnp.zeros_like(m_scr)
        crow_scr[...] = jnp.zeros_like(crow_scr)

    lane = lax.broadcasted_iota(I32, (lc, LANES), 1)
    pos = c * lc + lax.broadcasted_iota(I32, (lc, LANES), 0)
    valid = pos >= PAD_FRONT
    is_ig = lane < G_LF
    row_t = lax.broadcasted_iota(I32, (lc, lc), 0)
    col_s = lax.broadcasted_iota(I32, (lc, lc), 1)
    causal = col_s <= row_t

    def one_sequence(sq):
        g = gs_ref[sq]
        gsum = jnp.where(valid & jnp.logical_not(is_ig), g, 0.0)
        bcum = _dot_exact_rhs01_lhs(tril_ref[...], _split3(gsum))
        comb = jnp.where(is_ig, jnp.where(valid, g, -jnp.inf), bcum)
        comb_t = comb.T
        qk = qk_ref[sq]
        vv = v_ref[sq]
        for h in range(M_HEADS):
            q = qk[:, h * M_DK:(h + 1) * M_DK]
            k = qk[:, (M_HEADS + h) * M_DK:(M_HEADS + h + 1) * M_DK]
            v = vv[:, h * M_DV:(h + 1) * M_DV]
            m_prev = m_scr[sq, h, 0:1, 0:1]
            b_col = bcum[:, G_LF + h:G_LF + h + 1]
            ig_col = comb[:, G_IG + h:G_IG + h + 1]
            b_row = comb_t[G_LF + h:G_LF + h + 1, :]
            ig_row = comb_t[G_IG + h:G_IG + h + 1, :]

            log_d = jnp.where(causal, b_col - b_row + ig_row, -jnp.inf)
            inter = b_col + m_prev
            m_row = jnp.maximum(inter, jnp.max(log_d, axis=1, keepdims=True))
            s = _dot_nt(q, k) * jnp.exp(log_d - m_row)
            w_int = jnp.exp(inter - m_row)
            cmat = c_scr[sq, h]
            nvec = n_scr[sq, h, 0:1, :]
            num = _dot(s.astype(BF16), v) + w_int * _dot_nt(q, cmat.astype(BF16))
            den = jnp.sum(s, axis=1, keepdims=True) + w_int * jnp.sum(q.astype(F32) * nvec, axis=1, keepdims=True)
            hh = num / jnp.maximum(jnp.abs(den), jnp.exp(-m_row))
            hn = (hh * lax.rsqrt(jnp.mean(hh * hh, axis=1, keepdims=True) + EPS)
                  * hnw_ref[:, h * M_DV:(h + 1) * M_DV])
            hn_ref[sq, :, h * M_DV:(h + 1) * M_DV] = hn.astype(BF16)

            b_last = b_col[lc - 1:lc, :]
            m_new = jnp.maximum(b_last + m_prev, jnp.max(b_last - b_row + ig_row, axis=1, keepdims=True))
            decay = jnp.exp(b_last + m_prev - m_new)
            ws_col = jnp.exp(b_last - b_col + ig_col - m_new)
            kw = k.astype(F32) * ws_col
            c_scr[sq, h] = decay * cmat + _dot_tn(v, kw.astype(BF16))
            n_scr[sq, h] = jnp.broadcast_to(decay * nvec + jnp.sum(kw, axis=0, keepdims=True), n_scr.shape[2:])
            m_scr[sq, h] = jnp.broadcast_to(m_new, m_scr.shape[2:])

        cg = bcum + crow_scr[sq, 0:1, :]
        crow_scr[sq] = jnp.broadcast_to(cg[lc - 1:lc, :], crow_scr.shape[1:])
        p0, p1, p2 = _split3(cg * LOG2E)
        p0 = jnp.where(lane == 0, jnp.ones_like(p0), p0)
        ka_ref[sq] = (_dot(p0, place_ref[0]) + _dot(p1, place_ref[1]) + _dot(p2, place_ref[2])).astype(BF16)
        qa_ref[sq] = (_dot(p0, place_ref[3]) + _dot(p1, place_ref[4]) + _dot(p2, place_ref[5])).astype(BF16)

    for sq in range(MLSTM_SEQS):
        one_sequence(sq)

    @pl.when(c == pl.num_programs(1) - 1)
    def _():
        c_out[...] = c_scr[...]
        n_out[...] = n_scr[...]
        m_out[...] = m_scr[...]


def _dot_exact_rhs01_lhs(m01, parts):
    acc = _dot(m01, parts[0])
    for p in parts[1:]:
        acc = acc + _dot(m01, p)
    return acc


BIAS_LANES = 6


def _bias_placement():
    pm = np.zeros((6, LANES, LANES), np.float32)
    for h in range(F_HEADS):
        base = BIAS_LANES * h
        for t in range(3):
            pm[t, G_FF + h, base + t] = -1.0
            pm[0, 0, base + 3 + t] = 1.0
            pm[3 + t, G_FF + h, base + 3 + t] = 1.0
            pm[3, 0, base + t] = 1.0
    return jnp.asarray(pm, BF16)


def _mlstm_prompt(z3, gs3, hnw, tril):
    b, lp, _ = z3.shape
    lc = M_CHUNK
    nc = lp // lc
    sq = MLSTM_SEQS
    assert b % sq == 0
    return pl.pallas_call(
        _mlstm_kernel,
        grid=(b // sq, nc),
        in_specs=[
            pl.BlockSpec((sq, lc, D_MODEL), lambda i, c: (i, c, ZB_MQK)),
            pl.BlockSpec((sq, lc, D_MODEL), lambda i, c: (i, c, ZB_MV)),
            pl.BlockSpec((sq, lc, LANES), lambda i, c: (i, c, 0)),
            pl.BlockSpec((1, D_MODEL), lambda i, c: (0, 0)),
            pl.BlockSpec((lc, lc), lambda i, c: (0, 0)),
            pl.BlockSpec((6, LANES, LANES), lambda i, c: (0, 0, 0)),
        ],
        out_specs=[
            pl.BlockSpec((sq, lc, D_MODEL), lambda i, c: (i, c, 0)),
            pl.BlockSpec((sq, lc, LANES), lambda i, c: (i, c, 0)),
            pl.BlockSpec((sq, lc, LANES), lambda i, c: (i, c, 0)),
            pl.BlockSpec((sq, M_HEADS, M_DV, M_DK), lambda i, c: (i, 0, 0, 0)),
            pl.BlockSpec((sq, M_HEADS, 8, M_DK), lambda i, c: (i, 0, 0, 0)),
            pl.BlockSpec((sq, M_HEADS, 8, LANES), lambda i, c: (i, 0, 0, 0)),
        ],
        out_shape=[
            jax.ShapeDtypeStruct((b, lp, D_MODEL), BF16),
            jax.ShapeDtypeStruct((b, lp, LANES), BF16),
            jax.ShapeDtypeStruct((b, lp, LANES), BF16),
            jax.ShapeDtypeStruct((b, M_HEADS, M_DV, M_DK), F32),
            jax.ShapeDtypeStruct((b, M_HEADS, 8, M_DK), F32),
            jax.ShapeDtypeStruct((b, M_HEADS, 8, LANES), F32),
        ],
        scratch_shapes=[
            pltpu.VMEM((sq, M_HEADS, M_DV, M_DK), F32),
            pltpu.VMEM((sq, M_HEADS, 8, M_DK), F32),
            pltpu.VMEM((sq, M_HEADS, 8, LANES), F32),
            pltpu.VMEM((sq, 8, LANES), F32),
        ],
        compiler_params=_cparams(("parallel", "arbitrary")),
        name="mlstm",
    )(z3, z3, gs3, hnw, tril, _bias_placement())


MSTEP_SEQS = 8


def _mstep_kernel(qk_ref, v_ref, gs_ref, m_ref, n_ref, c_ref, hnw_ref,
                  hn_ref, c_out, n_out, m_out):
    sb = MSTEP_SEQS
    qk = qk_ref[...]
    vv = v_ref[...]
    g = gs_ref[...]
    row = lax.broadcasted_iota(I32, (sb, 1), 0)
    m_new_all = jnp.zeros((sb, LANES), F32)
    lane = lax.broadcasted_iota(I32, (sb, LANES), 1)
    for h in range(M_HEADS):
        q = qk[:, h * M_DK:(h + 1) * M_DK]
        k = qk[:, (M_HEADS + h) * M_DK:(M_HEADS + h + 1) * M_DK]
        v = vv[:, h * M_DV:(h + 1) * M_DV]
        ig = g[:, G_IG + h:G_IG + h + 1]
        lf = g[:, G_LF + h:G_LF + h + 1]
        m_prev = m_ref[:, h:h + 1]
        inter = lf + m_prev
        m_row = jnp.maximum(inter, ig)
        d = jnp.exp(ig - m_row)
        w_int = jnp.exp(inter - m_row)
        qf = q.astype(F32)
        kf = k.astype(F32)
        s = jnp.sum(qf * kf, axis=1, keepdims=True) * d
        nvec = n_ref[:, h * M_DK:(h + 1) * M_DK]
        den = s + w_int * jnp.sum(qf * nvec, axis=1, keepdims=True)
        kw = (kf * d).astype(BF16)
        hrows = []
        for i in range(sb):
            cmat = c_ref[i, h]
            sel = row == i
            qi = jnp.where(sel, q, jnp.zeros_like(q))
            qc = _dot_nt(qi, cmat.astype(BF16))
            hrows.append(jnp.where(sel, qc, 0.0))
            vi = jnp.where(sel, v, jnp.zeros_like(v))
            c_out[i, h] = w_int[i:i + 1, :] * cmat + _dot_tn(vi, kw)
        qc_all = functools.reduce(lambda a, b: a + b, hrows)
        num = s * v.astype(F32) + w_int * qc_all
        hh = num / jnp.maximum(jnp.abs(den), jnp.exp(-m_row))
        hn = hh * lax.rsqrt(jnp.mean(hh * hh, axis=1, keepdims=True) + EPS) * hnw_ref[:, h * M_DV:(h + 1) * M_DV]
        hn_ref[:, h * M_DV:(h + 1) * M_DV] = hn.astype(BF16)
        n_out[:, h * M_DK:(h + 1) * M_DK] = w_int * nvec + kf * d
        m_new_all = jnp.where(lane == h, m_row, m_new_all)
    m_out[...] = m_new_all


def _mlstm_step(zs, gss, m_in, n_in, c_in, hnw):
    db = zs.shape[0]
    sb = MSTEP_SEQS
    return pl.pallas_call(
        _mstep_kernel,
        grid=(db // sb,),
        in_specs=[
            pl.BlockSpec((sb, D_MODEL), lambda i: (i, ZB_MQK)),
            pl.BlockSpec((sb, D_MODEL), lambda i: (i, ZB_MV)),
            pl.BlockSpec((sb, LANES), lambda i: (i, 0)),
            pl.BlockSpec((sb, LANES), lambda i: (i, 0)),
            pl.BlockSpec((sb, M_HEADS * M_DK), lambda i: (i, 0)),
            pl.BlockSpec((sb, M_HEADS, M_DV, M_DK), lambda i: (i, 0, 0, 0)),
            pl.BlockSpec((1, D_MODEL), lambda i: (0, 0)),
        ],
        out_specs=[
            pl.BlockSpec((sb, D_MODEL), lambda i: (i, 0)),
            pl.BlockSpec((sb, M_HEADS, M_DV, M_DK), lambda i: (i, 0, 0, 0)),
            pl.BlockSpec((sb, M_HEADS * M_DK), lambda i: (i, 0)),
            pl.BlockSpec((sb, LANES), lambda i: (i, 0)),
        ],
        out_shape=[
            jax.ShapeDtypeStruct((db, D_MODEL), BF16),
            jax.ShapeDtypeStruct((db, M_HEADS, M_DV, M_DK), F32),
            jax.ShapeDtypeStruct((db, M_HEADS * M_DK), F32),
            jax.ShapeDtypeStruct((db, LANES), F32),
        ],
        compiler_params=_cparams(("parallel",)),
        name="mstep",
    )(zs, zs, gss, m_in, n_in, c_in, hnw)


def _fox_kernel(qi_tab, ki_tab, q_ref, qa_ref, k_ref, ka_ref, vt_ref, o_ref,
                qb_scr, acc_scr, m_scr, l_scr, a_scr, st_scr, pt_scr):
    step = pl.program_id(1)
    qi = qi_tab[step]
    ki = ki_tab[step]
    tq = tk = ATT_BLOCK

    @pl.when(ki == 0)
    def _():
        acc_scr[...] = jnp.zeros_like(acc_scr)
        m_scr[...] = jnp.full_like(m_scr, -jnp.inf)
        l_scr[...] = jnp.zeros_like(l_scr)
        lane = lax.broadcasted_iota(I32, (tq, LANES), 1)
        qa = qa_ref[0]
        for h in range(F_HEADS):
            q2 = q_ref[0, :, (h // 2) * LANES:(h // 2 + 1) * LANES]
            own = (lane >= (h % 2) * F_DH) & (lane < (h % 2 + 1) * F_DH)
            bias = (lane >= h * BIAS_LANES) & (lane < (h + 1) * BIAS_LANES)
            qb_scr[h] = jnp.concatenate([jnp.where(own, q2, jnp.zeros_like(q2)),
                                         jnp.where(bias, qa, jnp.zeros_like(qa))], axis=1)

    def sweep(masked):
        ka = ka_ref[0]
        if masked:
            key = ki * tk + lax.broadcasted_iota(I32, (tk, tq), 0)
            qry = qi * tq + lax.broadcasted_iota(I32, (tk, tq), 1)
            visible = (key <= qry) & ((key >= PAD_FRONT) | (qry < PAD_FRONT))
        def score_stage(h):
            kb = jnp.concatenate([k_ref[0, :, (h // 2) * LANES:(h // 2 + 1) * LANES], ka], axis=1)
            st_scr[h] = _dot_nt(kb, qb_scr[h])

        def softmax_stage(h):
            for c0 in range(0, tq, LANES):
                cs = slice(c0, c0 + LANES)
                def scores(r0):
                    st = st_scr[h, r0:r0 + ATT_SUB, cs]
                    return jnp.where(visible[r0:r0 + ATT_SUB, cs], st, -jnp.inf) if masked else st

                m_prev = m_scr[h, 0:1, cs]
                m_next = m_prev
                for r0 in range(0, tk, ATT_SUB):
                    m_next = jnp.maximum(m_next, jnp.max(scores(r0), axis=0, keepdims=True))
                alpha = jnp.exp2(m_prev - m_next)
                l_new = alpha * l_scr[h, 0:1, cs]
                for r0 in range(0, tk, ATT_SUB):
                    pt = jnp.exp2(scores(r0) - m_next)
                    l_new = l_new + jnp.sum(pt, axis=0, keepdims=True)
                    pt_scr[h, r0:r0 + ATT_SUB, cs] = pt.astype(BF16)
                l_scr[h, :, cs] = jnp.broadcast_to(l_new, (8, LANES))
                m_scr[h, :, cs] = jnp.broadcast_to(m_next, (8, LANES))
                a_scr[h, :, cs] = jnp.broadcast_to(alpha, (8, LANES))
        def value_stage(h):
            acc_scr[h] = a_scr[h, 0:1, :] * acc_scr[h] + _dot(vt_ref[h * F_DH:(h + 1) * F_DH, :], pt_scr[h])

        for stage in (score_stage, softmax_stage, value_stage):
            for h in range(F_HEADS):
                stage(h)

    edge = (ki == qi) | (ki == 0)

    @pl.when(edge)
    def _():
        sweep(True)

    @pl.when(jnp.logical_not(edge))
    def _():
        sweep(False)

    @pl.when(ki == qi)
    def _():
        for p in range(F_HEADS // 2):
            o2 = jnp.concatenate([acc_scr[2 * p] * (1.0 / l_scr[2 * p, 0:1, :]),
                                  acc_scr[2 * p + 1] * (1.0 / l_scr[2 * p + 1, 0:1, :])], axis=0)
            o_ref[0, :, p * LANES:(p + 1) * LANES] = o2.T.astype(BF16)


def _fox_prompt(z3, qa, ka, vt16):
    b, lp, _ = z3.shape
    blk = ATT_BLOCK
    nb = lp // blk
    qi_tab = np.concatenate([np.full((i + 1,), i, np.int32) for i in range(nb)])
    ki_tab = np.concatenate([np.arange(i + 1, dtype=np.int32) for i in range(nb)])
    grid_spec = pltpu.PrefetchScalarGridSpec(
        num_scalar_prefetch=2,
        grid=(b, len(qi_tab)),
        in_specs=[
            pl.BlockSpec((1, blk, D_MODEL), lambda i, s, qt, kt: (i, qt[s], ZB_FQ)),
            pl.BlockSpec((1, blk, LANES), lambda i, s, qt, kt: (i, qt[s], 0)),
            pl.BlockSpec((1, blk, D_MODEL), lambda i, s, qt, kt: (i, kt[s], ZB_FK)),
            pl.BlockSpec((1, blk, LANES), lambda i, s, qt, kt: (i, kt[s], 0)),
            pl.BlockSpec((D_MODEL, blk), lambda i, s, qt, kt: (0, i * nb + kt[s])),
        ],
        out_specs=pl.BlockSpec((1, blk, D_MODEL), lambda i, s, qt, kt: (i, qt[s], 0)),
        scratch_shapes=[
            pltpu.VMEM((F_HEADS, blk, 2 * LANES), BF16),
            pltpu.VMEM((F_HEADS, F_DH, blk), F32),
            pltpu.VMEM((F_HEADS, 8, blk), F32),
            pltpu.VMEM((F_HEADS, 8, blk), F32),
            pltpu.VMEM((F_HEADS, 8, blk), F32),
            pltpu.VMEM((F_HEADS, blk, blk), F32),
            pltpu.VMEM((F_HEADS, blk, blk), BF16),
        ],
    )
    return pl.pallas_call(
        _fox_kernel,
        grid_spec=grid_spec,
        out_shape=jax.ShapeDtypeStruct((b, lp, D_MODEL), BF16),
        compiler_params=_cparams(("parallel", "arbitrary")),
        name="fox",
    )(jnp.asarray(qi_tab), jnp.asarray(ki_tab), z3, qa, z3, ka, vt16)


DEC_PAGES = 16


def _lane_bcast_cols(row):
    full = jnp.broadcast_to(row, (LANES, D_MODEL)).T
    return full.reshape(F_HEADS, F_DH, LANES)


def _per_head(x):
    return x.reshape(F_HEADS, 1, LANES)


def _dec_kernel(pt_ref, q_ref, kn_ref, vn_ref, lfn_ref, triu_ref, ones_ref, *rest):
    np_ = DEC_PAGES
    k_refs, v_refs, lf_refs = rest[:np_], rest[np_:2 * np_], rest[2 * np_:3 * np_]
    o_ref, qb_scr, acc_scr, m_scr, l_scr, c_scr = rest[3 * np_:]
    p = pl.program_id(1)

    @pl.when(p == 0)
    def _():
        qb_scr[...] = _lane_bcast_cols(q_ref[0])
        acc_scr[...] = jnp.zeros_like(acc_scr)
        m_scr[...] = jnp.full_like(m_scr, -jnp.inf)
        l_scr[...] = jnp.zeros_like(l_scr)
        c_scr[...] = jnp.zeros_like(c_scr)

    qb = qb_scr[...]
    carry = c_scr[...]
    us = []
    for i in range(np_):
        cum = _dot_exact_rhs01(_split3(lf_refs[i][0]), triu_ref[...]) + carry
        carry = jnp.broadcast_to(cum[:, LANES - 1:LANES], carry.shape)
        s = jnp.sum(qb * k_refs[i][0], axis=1)
        us.append(s - cum * LOG2E)
    c_scr[...] = carry
    m_prev = m_scr[...]
    m_cur = functools.reduce(jnp.maximum, [jnp.max(u, axis=1, keepdims=True) for u in us])
    m_next = jnp.maximum(m_prev, m_cur)
    alpha = jnp.exp2(m_prev - m_next)
    acc = _per_head(alpha) * acc_scr[...]
    l_new = alpha * l_scr[...]
    for i in range(np_):
        pr = jnp.exp2(us[i] - m_next)
        l_new = l_new + jnp.sum(pr, axis=1, keepdims=True)
        acc = acc + _per_head(pr) * v_refs[i][0]
    acc_scr[...] = acc
    l_scr[...] = l_new
    m_scr[...] = m_next

    @pl.when(p == pl.num_programs(1) - 1)
    def _():
        s_new = jnp.sum(qb * _lane_bcast_cols(kn_ref[0]), axis=1)
        u_new = s_new - (carry + lfn_ref[0]) * LOG2E
        m_fin = jnp.maximum(m_next, u_new)
        a_fin = jnp.exp2(m_next - m_fin)
        pn = jnp.exp2(u_new - m_fin)
        l_fin = a_fin * l_new + pn
        tot = (_per_head(a_fin) * acc + _per_head(pn * (1.0 / LANES)) * _lane_bcast_cols(vn_ref[0])) / _per_head(l_fin)
        out8 = _dot_nt_exact_lhs01(ones_ref[...], _split3(tot.reshape(D_MODEL, LANES)))
        o_ref[0] = out8[0:1, :]


def _dot_nt_exact_lhs01(m01, parts):
    acc = _dot_nt(m01, parts[0])
    for part in parts[1:]:
        acc = acc + _dot_nt(m01, part)
    return acc


def _fox_decode(page_table, q3, kn3, vn3, lfn3, cache_kt, cache_vt, cache_lft):
    db, npg = page_table.shape
    pg = cache_kt.shape[3]
    assert pg == LANES and npg % DEC_PAGES == 0
    row3 = lambda s, p, pt: (s, 0, 0)
    fix = lambda s, p, pt: (0, 0)

    def page(i, nd):
        return lambda s, p, pt: (pt[s, p * DEC_PAGES + i],) + (0,) * nd

    in_specs = [
        pl.BlockSpec((1, 1, D_MODEL), row3),
        pl.BlockSpec((1, 1, D_MODEL), row3),
        pl.BlockSpec((1, 1, D_MODEL), row3),
        pl.BlockSpec((1, F_HEADS, LANES), row3),
        pl.BlockSpec((LANES, LANES), fix),
        pl.BlockSpec((8, LANES), fix),
    ]
    in_specs += [pl.BlockSpec((1, F_HEADS, F_DH, LANES), page(i, 3)) for i in range(DEC_PAGES)]
    in_specs += [pl.BlockSpec((1, F_HEADS, F_DH, LANES), page(i, 3)) for i in range(DEC_PAGES)]
    in_specs += [pl.BlockSpec((1, F_HEADS, LANES), page(i, 2)) for i in range(DEC_PAGES)]
    grid_spec = pltpu.PrefetchScalarGridSpec(
        num_scalar_prefetch=1,
        grid=(db, npg // DEC_PAGES),
        in_specs=in_specs,
        out_specs=pl.BlockSpec((1, 1, D_MODEL), row3),
        scratch_shapes=[
            pltpu.VMEM((F_HEADS, F_DH, LANES), F32),
            pltpu.VMEM((F_HEADS, F_DH, LANES), F32),
            pltpu.VMEM((F_HEADS, LANES), F32),
            pltpu.VMEM((F_HEADS, LANES), F32),
            pltpu.VMEM((F_HEADS, LANES), F32),
        ],
    )
    triu = jnp.asarray(np.triu(np.ones((LANES, LANES), np.float32)), BF16)
    ones = jnp.ones((8, LANES), BF16)
    return pl.pallas_call(
        _dec_kernel,
        grid_spec=grid_spec,
        out_shape=jax.ShapeDtypeStruct((db, 1, D_MODEL), F32),
        compiler_params=_cparams(("parallel", "arbitrary")),
        name="dec",
    )(page_table, q3, kn3, vn3, lfn3, triu, ones,
      *([cache_kt] * DEC_PAGES), *([cache_vt] * DEC_PAGES), *([cache_lft] * DEC_PAGES))


TOKEN_SUBLANES = D_MODEL // LANES


def _store_token_tiles(ref, x):
    n = x.shape[0]
    for g in range(TOKEN_SUBLANES):
        ref[pl.ds(g, n, stride=TOKEN_SUBLANES), :] = x[:, g * LANES:(g + 1) * LANES]


def _load_token_tiles(ref, n):
    return jnp.concatenate([ref[pl.ds(g, n, stride=TOKEN_SUBLANES), :] for g in range(TOKEN_SUBLANES)], axis=1)


def _merge_kernel(x_ref, hn_ref, so_ref, hf_ref, ga_ref, gb_ref, wbm_ref, wbf_ref, wo_ref, ln2_ref,
                  wrh_ref, wrl_ref, br_ref, x1_ref, t_ref, lg_ref):
    hm = hn_ref[...] * so_ref[...]
    ya = _dot(hm, wbm_ref[...])
    yb = _dot(hf_ref[...], wbf_ref[...])
    u = ga_ref[...].astype(F32) * ya + gb_ref[...].astype(F32) * yb
    x1 = x_ref[...] + _dot(u.astype(BF16), wo_ref[...])
    x1_ref[...] = x1
    t = x1 * lax.rsqrt(jnp.mean(x1 * x1, axis=-1, keepdims=True) + EPS) * ln2_ref[...]
    _store_token_tiles(t_ref, t)
    th, tl = _split2(t)
    wrh = wrh_ref[...]
    lg_ref[...] = _dot(th, wrh) + _dot(tl, wrh) + _dot(th, wrl_ref[...]) + br_ref[...]


def _merge(x, hn, z, hf, wbm, wbf, wo, ln2, wrh, wrl, br, tm):
    t = x.shape[0]
    row = lambda i: (i, 0)
    fix = lambda i: (0, 0)
    full = pl.BlockSpec((D_MODEL, D_MODEL), fix)
    return pl.pallas_call(
        _merge_kernel,
        grid=(t // tm,),
        in_specs=[
            pl.BlockSpec((tm, D_MODEL), row),
            pl.BlockSpec((tm, D_MODEL), row),
            pl.BlockSpec((tm, D_MODEL), lambda i: (i, ZB_MO)),
            pl.BlockSpec((tm, D_MODEL), row),
            pl.BlockSpec((tm, D_MODEL), lambda i: (i, ZB_GA)),
            pl.BlockSpec((tm, D_MODEL), lambda i: (i, ZB_GB)),
            full, full, full,
            pl.BlockSpec((1, D_MODEL), fix),
            pl.BlockSpec((D_MODEL, LANES), fix),
            pl.BlockSpec((D_MODEL, LANES), fix),
            pl.BlockSpec((1, LANES), fix),
        ],
        out_specs=[
            pl.BlockSpec((tm, D_MODEL), row),
            pl.BlockSpec((tm * TOKEN_SUBLANES, LANES), row),
            pl.BlockSpec((tm, LANES), row),
        ],
        out_shape=[
            jax.ShapeDtypeStruct((t, D_MODEL), F32),
            jax.ShapeDtypeStruct((t * TOKEN_SUBLANES, LANES), F32),
            jax.ShapeDtypeStruct((t, LANES), F32),
        ],
        compiler_params=_cparams(("parallel",)),
        name="merge",
    )(x, hn, z, hf, z, z, wbm, wbf, wo, ln2, wrh, wrl, br)


def _route_kernel(n_valid, lg_ref, tril_ref, ids_ref, wts_ref, rank_ref, cnt_ref, carry_scr):
    i = pl.program_id(0)

    @pl.when(i == 0)
    def _():
        carry_scr[...] = jnp.zeros_like(carry_scr)

    tm = lg_ref.shape[0]
    lane = lax.broadcasted_iota(I32, (tm, LANES), 1)
    lanef = lane.astype(F32)
    lg = jnp.where(lane < N_EXPERTS, lg_ref[...], -jnp.inf)
    vals, idxs, hots = [], [], []
    for _ in range(TOP_K):
        mx = jnp.max(lg, axis=1, keepdims=True)
        idx = jnp.min(jnp.where(lg == mx, lanef, float(LANES)), axis=1, keepdims=True)
        hot = lanef == idx
        lg = jnp.where(hot, -jnp.inf, lg)
        vals.append(mx)
        idxs.append(idx)
        hots.append(hot)
    es = [jnp.exp(v - vals[0]) for v in vals]
    tot = functools.reduce(lambda a, b: a + b, es)
    sel = functools.reduce(lambda a, b: a | b, hots)
    real = i * tm + lax.broadcasted_iota(I32, (tm, LANES), 0) < n_valid
    a01 = jnp.where(sel & real, 1.0, 0.0)
    before = _dot(tril_ref[...], a01.astype(BF16)) + carry_scr[0:1, :]
    carry_scr[...] = jnp.broadcast_to(carry_scr[0:1, :] + jnp.sum(a01, axis=0, keepdims=True), carry_scr.shape)
    ids = jnp.zeros((tm, LANES), I32)
    wts = jnp.zeros((tm, LANES), F32)
    rank = jnp.zeros((tm, LANES), I32)
    for kk in range(TOP_K):
        r = jnp.sum(jnp.where(hots[kk], before, 0.0), axis=1, keepdims=True)
        ids = jnp.where(lane == kk, idxs[kk].astype(I32), ids)
        wts = jnp.where(lane == kk, es[kk] / tot, wts)
        rank = jnp.where(lane == kk, r.astype(I32), rank)
    ids_ref[...] = ids
    wts_ref[...] = wts
    rank_ref[...] = rank
    cnt_ref[...] = carry_scr[...].astype(I32)


def _route(logits, tril_strict, tm, n_valid):
    t = logits.shape[0]
    row = lambda i: (i, 0)
    return pl.pallas_call(
        functools.partial(_route_kernel, n_valid),
        grid=(t // tm,),
        in_specs=[pl.BlockSpec((tm, LANES), row), pl.BlockSpec((tm, tm), lambda i: (0, 0))],
        out_specs=[pl.BlockSpec((tm, LANES), row)] * 3 + [pl.BlockSpec((8, LANES), lambda i: (0, 0))],
        out_shape=[
            jax.ShapeDtypeStruct((t, LANES), I32),
            jax.ShapeDtypeStruct((t, LANES), F32),
            jax.ShapeDtypeStruct((t, LANES), I32),
            jax.ShapeDtypeStruct((8, LANES), I32),
        ],
        scratch_shapes=[pltpu.VMEM((8, LANES), F32)],
        compiler_params=_cparams(("arbitrary",)),
        name="route",
    )(logits, tril_strict)


def _tile_rows(i):
    return pl.ds(pl.multiple_of(i * TOKEN_SUBLANES, TOKEN_SUBLANES), TOKEN_SUBLANES)


def _dispatch_kernel(ntp, pad_lo, pad_hi, n_pad, pos_ref, tp_ref, ts_ref, xs_ref, zero_scr, sem, zsem):
    tc = COMB_TILE
    i = pl.program_id(0)

    def scatter_from(t_ref):
        def issue(r, carry):
            src = t_ref.at[_tile_rows(r), :]
            for kk in range(TOP_K):
                pltpu.make_async_copy(src, xs_ref.at[_tile_rows(pos_ref[0, kk, r]), :], sem).start(priority=kk % 2)
            return carry

        lax.fori_loop(0, tc, issue, 0, unroll=2)

    @pl.when(i < ntp)
    def _():
        scatter_from(tp_ref)

    @pl.when(i >= ntp)
    def _():
        scatter_from(ts_ref)

    @pl.when(i == 0)
    def _():
        zero_scr[...] = jnp.zeros_like(zero_scr)

        def per_expert(e, carry):
            def per_slot(s, c2):
                pltpu.make_async_copy(zero_scr, xs_ref.at[_tile_rows(s), :], zsem).start()
                return c2
            return lax.fori_loop(pad_lo[e], pad_hi[e], per_slot, carry)

        lax.fori_loop(0, pad_lo.shape[0], per_expert, 0)

        def drain(s, carry):
            pltpu.make_async_copy(zero_scr, xs_ref.at[pl.ds(0, TOKEN_SUBLANES), :], zsem).wait()
            return carry

        lax.fori_loop(0, n_pad[0], drain, 0)

    for kk in range(TOP_K):
        pltpu.make_async_copy(tp_ref, xs_ref.at[pl.ds(0, tc * TOKEN_SUBLANES), :], sem).wait()


def _dispatch(pad_lo, pad_hi, n_pad, pos3, t_p, t_s, r_pad):
    tc = COMB_TILE
    ntp = t_p.shape[0] // (tc * TOKEN_SUBLANES)
    assert t_s.shape[0] == tc * TOKEN_SUBLANES and pos3.shape[0] == ntp + 1
    grid_spec = pltpu.PrefetchScalarGridSpec(
        num_scalar_prefetch=3,
        grid=(ntp + 1,),
        in_specs=[
            pl.BlockSpec((1, TOP_K, tc), lambda i, lo, hi, n: (i, 0, 0), memory_space=pltpu.SMEM),
            pl.BlockSpec((tc * TOKEN_SUBLANES, LANES), lambda i, lo, hi, n: (jnp.minimum(i, ntp - 1), 0)),
            pl.BlockSpec((tc * TOKEN_SUBLANES, LANES), lambda i, lo, hi, n: (0, 0)),
        ],
        out_specs=pl.BlockSpec(memory_space=pl.ANY),
        scratch_shapes=[pltpu.VMEM((TOKEN_SUBLANES, LANES), F32), pltpu.SemaphoreType.DMA, pltpu.SemaphoreType.DMA],
    )
    return pl.pallas_call(
        functools.partial(_dispatch_kernel, ntp),
        grid_spec=grid_spec,
        out_shape=jax.ShapeDtypeStruct((r_pad * TOKEN_SUBLANES, LANES), F32),
        compiler_params=_cparams(("arbitrary",)),
        name="dispatch",
    )(pad_lo, pad_hi, n_pad, pos3, t_p, t_s)


def _experts_kernel(te_ref, nu_ref, xs_ref, w1_ref, b1_ref, w2_ref, b2_ref, ys_ref, w1b_scr, w2b_scr):
    j = pl.program_id(0)
    prev = te_ref[jnp.maximum(j - 1, 0)]

    @pl.when((j == 0) | (te_ref[j] != prev))
    def _():
        w1b_scr[...] = w1_ref[0].astype(BF16)
        w2b_scr[...] = w2_ref[0].astype(BF16)

    @pl.when(j < nu_ref[0])
    def _():
        x = _load_token_tiles(xs_ref, EXP_TILE)
        hcat = _dot(x.astype(BF16), w1b_scr[...]) + b1_ref[0]
        g = jnp.minimum(hcat[:, :D_FF], SWIGLU_LIMIT)
        u = jnp.clip(hcat[:, D_FF:], -SWIGLU_LIMIT, SWIGLU_LIMIT)
        a = g * jax.nn.sigmoid(SWIGLU_ALPHA * g) * (u + 1.0)
        _store_token_tiles(ys_ref, _dot(a.astype(BF16), w2b_scr[...]) + b2_ref[0])

    @pl.when(j >= nu_ref[0])
    def _():
        ys_ref[...] = jnp.zeros_like(ys_ref)


def _experts(tile_expert, n_used, xs, w1, b1, w2, b2):
    r = xs.shape[0] // TOKEN_SUBLANES
    tm = EXP_TILE
    grid_spec = pltpu.PrefetchScalarGridSpec(
        num_scalar_prefetch=2,
        grid=(r // tm,),
        in_specs=[
            pl.BlockSpec((tm * TOKEN_SUBLANES, LANES), lambda j, te, nu: (jnp.minimum(j, nu[0] - 1), 0)),
            pl.BlockSpec((1, D_MODEL, 2 * D_FF), lambda j, te, nu: (te[j], 0, 0)),
            pl.BlockSpec((1, 1, 2 * D_FF), lambda j, te, nu: (te[j], 0, 0)),
            pl.BlockSpec((1, D_FF, D_MODEL), lambda j, te, nu: (te[j], 0, 0)),
            pl.BlockSpec((1, 1, D_MODEL), lambda j, te, nu: (te[j], 0, 0)),
        ],
        out_specs=pl.BlockSpec((tm * TOKEN_SUBLANES, LANES), lambda j, te, nu: (j, 0)),
        scratch_shapes=[
            pltpu.VMEM((D_MODEL, 2 * D_FF), BF16),
            pltpu.VMEM((D_FF, D_MODEL), BF16),
        ],
    )
    return pl.pallas_call(
        _experts_kernel,
        grid_spec=grid_spec,
        out_shape=jax.ShapeDtypeStruct((r * TOKEN_SUBLANES, LANES), F32),
        compiler_params=_cparams(("arbitrary",)),
        name="experts",
    )(tile_expert, n_used, xs, w1, b1.reshape(N_EXPERTS, 1, 2 * D_FF), w2, b2.reshape(N_EXPERTS, 1, D_MODEL))


def _combine_kernel(ntp, tps, pos_ref, ys_ref, x1p_ref, x1s_ref, wts_ref, lnf_ref, yp_ref, ysm_ref, rows_scr, sem):
    tc = COMB_TILE
    i = pl.program_id(0)
    is_prompt = i < ntp
    wanted = jnp.logical_or(jnp.logical_not(is_prompt), lax.rem(i, tps) >= PAD_TILES)

    def finish(x1):
        w = wts_ref[...]
        acc = x1
        for kk in range(TOP_K):
            acc = acc + w[:, kk:kk + 1] * _load_token_tiles(rows_scr.at[kk], tc)
        return acc * lax.rsqrt(jnp.mean(acc * acc, axis=-1, keepdims=True) + EPS) * lnf_ref[...]

    @pl.when(wanted)
    def _():
        def issue(r, carry):
            for kk in range(TOP_K):
                pltpu.make_async_copy(ys_ref.at[_tile_rows(pos_ref[0, kk, r]), :],
                                      rows_scr.at[kk, _tile_rows(r), :], sem).start(priority=kk % 2)
            return carry

        lax.fori_loop(0, tc, issue, 0, unroll=2)
        for kk in range(TOP_K):
            pltpu.make_async_copy(ys_ref.at[pl.ds(0, tc * TOKEN_SUBLANES), :], rows_scr.at[kk], sem).wait()

        @pl.when(is_prompt)
        def _():
            yp_ref[0] = finish(x1p_ref[...])

        @pl.when(jnp.logical_not(is_prompt))
        def _():
            ysm_ref[...] = finish(x1s_ref[...])


def _combine(pos3, ys, x1p, x1s, wts, lnf, b, lp):
    tc = COMB_TILE
    ntp = x1p.shape[0] // tc
    tps = lp // tc
    seq_tiles = tps - PAD_TILES
    assert x1s.shape[0] == tc and pos3.shape[0] == ntp + 1

    def yp_map(i):
        blk = jnp.where(i >= ntp, seq_tiles - 1, jnp.maximum(lax.rem(i, tps) - PAD_TILES, 0))
        return (jnp.minimum(i // tps, b - 1), blk, 0)

    return pl.pallas_call(
        functools.partial(_combine_kernel, ntp, tps),
        grid=(ntp + 1,),
        in_specs=[
            pl.BlockSpec((1, TOP_K, tc), lambda i: (i, 0, 0), memory_space=pltpu.SMEM),
            pl.BlockSpec(memory_space=pl.ANY),
            pl.BlockSpec((tc, D_MODEL), lambda i: (jnp.minimum(i, ntp - 1), 0)),
            pl.BlockSpec((tc, D_MODEL), lambda i: (0, 0)),
            pl.BlockSpec((tc, LANES), lambda i: (i, 0)),
            pl.BlockSpec((1, D_MODEL), lambda i: (0, 0)),
        ],
        out_specs=[
            pl.BlockSpec((1, tc, D_MODEL), yp_map),
            pl.BlockSpec((tc, D_MODEL), lambda i: (0, 0)),
        ],
        out_shape=[
            jax.ShapeDtypeStruct((b, seq_tiles * tc, D_MODEL), F32),
            jax.ShapeDtypeStruct((tc, D_MODEL), F32),
        ],
        scratch_shapes=[pltpu.VMEM((TOP_K, tc * TOKEN_SUBLANES, LANES), F32), pltpu.SemaphoreType.DMA],
        compiler_params=_cparams(("arbitrary",)),
        name="combine",
    )(pos3, ys, x1p, x1s, wts, lnf)


def _consts():
    r64 = np.zeros((D_MODEL, LANES), np.float32)
    e64 = np.zeros((LANES, D_MODEL), np.float32)
    for h in range(F_HEADS):
        r64[h * F_DH:(h + 1) * F_DH, h] = 1.0 / F_DH
        e64[h, h * F_DH:(h + 1) * F_DH] = 1.0
    return jnp.asarray(r64, BF16), jnp.asarray(e64, BF16), jnp.asarray(e64, F32)


def _tril(n, strict=False):
    return jnp.asarray(np.tril(np.ones((n, n), np.float32), -1 if strict else 0), BF16)


def _round_up(a, b):
    return (a + b - 1) // b * b


def _moe_layer(t_p, t_s, x1_p, x1_s, lg_all, w1, b1, w2, b2, lnf, b, lp):
    tmoe = x1_p.shape[0] + x1_s.shape[0]
    t_route = _round_up(tmoe, ROW_TILE)
    lg_pad = jnp.concatenate([lg_all, jnp.zeros((t_route - tmoe, LANES), F32)], axis=0)
    ids, wts, rank, cnt = _route(lg_pad, _tril(ROW_TILE, strict=True), ROW_TILE, tmoe)
    counts = cnt[0, :N_EXPERTS]
    pcounts = (counts + EXP_TILE - 1) // EXP_TILE * EXP_TILE
    ends = jnp.cumsum(pcounts)
    starts = ends - pcounts
    pos = starts[ids[:tmoe, :TOP_K]] + rank[:tmoe, :TOP_K]
    pos3 = pos.reshape(tmoe // COMB_TILE, COMB_TILE, TOP_K).transpose(0, 2, 1)
    r_pad = _round_up(TOP_K * tmoe + N_EXPERTS * (EXP_TILE - 1), EXP_TILE)
    n_tiles = r_pad // EXP_TILE
    tile_start = jnp.arange(n_tiles, dtype=I32) * EXP_TILE
    tile_expert = jnp.minimum(jnp.sum((ends[None, :] <= tile_start[:, None]).astype(I32), axis=1), N_EXPERTS - 1)
    n_used = (ends[-1] // EXP_TILE).astype(I32)[None]
    pad_lo = jnp.concatenate([starts + counts, ends[-1:]]).astype(I32)
    pad_hi = jnp.concatenate([ends, jnp.full((1,), r_pad, ends.dtype)]).astype(I32)
    n_pad = jnp.sum(pad_hi - pad_lo).astype(I32)[None]
    xsrt = _dispatch(pad_lo, pad_hi, n_pad, pos3, t_p, t_s, r_pad)
    ysrt = _experts(tile_expert, n_used, xsrt, w1, b1, w2, b2)
    return _combine(pos3, ysrt, x1_p, x1_s, wts, lnf, b, lp)


def kernel(x_prompt, x_sample, cache_k, cache_v, cache_logf, state_C, state_n, state_m, page_table, meta_tokens, ln1, w_in, mlstm_b_i, mlstm_b_f, mlstm_head_norm, fox_b_f, fox_q_norm, fox_k_norm, w_branch_mlstm, w_branch_fox, w_out, ln2, w_router, b_router, w_exp_in, b_exp_in, w_exp_out, b_exp_out, ln_final):
    depth = w_in.shape[0]
    b, seq, _ = x_prompt.shape
    db, ds, _ = x_sample.shape
    assert ds == 1 and seq % ATT_BLOCK == 0 and db % MSTEP_SEQS == 0
    l_true = seq + N_META
    lp = seq + ATT_BLOCK
    tp = b * lp
    n_phys, pg = cache_k.shape[1], cache_k.shape[2]
    r64, e64, e64f = _consts()

    xp = jnp.concatenate([jnp.zeros((b, PAD_FRONT, D_MODEL), F32),
                          jnp.broadcast_to(meta_tokens[None].astype(F32), (b, N_META, D_MODEL)),
                          x_prompt], axis=1).reshape(tp, D_MODEL)
    xs = x_sample.reshape(db, D_MODEL)

    outs = {k: [] for k in ("kp", "vp", "lfp", "ks", "vs", "lfs", "cp", "np", "mp", "cs", "ns", "ms")}
    offs = np.cumsum((0,) + (M_HEADS * M_DK, M_HEADS * M_DK, M_HEADS * M_DV, M_HEADS * M_DV, M_HEADS, M_HEADS,
                             F_HEADS * F_DH, F_HEADS * F_DH, F_HEADS * F_DH, F_HEADS, 2 * D_MODEL))
    seg = lambda w, i: w[:, offs[i]:offs[i + 1]]
    for l in range(depth):
        w = w_in[l]
        gates = seg(w, 10)
        wmain = jnp.stack([jnp.concatenate([seg(w, 0), seg(w, 1)], axis=1), seg(w, 2), seg(w, 3), seg(w, 6),
                           seg(w, 7), seg(w, 8), gates[:, :D_MODEL], gates[:, D_MODEL:]]).astype(BF16)
        wsm = jnp.concatenate([seg(w, 4), seg(w, 5), seg(w, 9),
                               jnp.zeros((D_MODEL, LANES - G_END), F32)], axis=1)
        wsh = wsm.astype(BF16)
        wsl = (wsm - wsh.astype(F32)).astype(BF16)
        bias = jnp.concatenate([mlstm_b_i[l], mlstm_b_f[l], fox_b_f[l], jnp.zeros((LANES - G_END,), F32)])[None]
        qn = jnp.tile(fox_q_norm[l], F_HEADS)[None]
        kn = jnp.tile(fox_k_norm[l], F_HEADS)[None]
        hnw = mlstm_head_norm[l][None]
        ln1l = ln1[l][None]
        wbm = w_branch_mlstm[l].astype(BF16)
        wbf = w_branch_fox[l].astype(BF16)
        wo = w_out[l].astype(BF16)
        wr = jnp.concatenate([w_router[l], jnp.zeros((D_MODEL, LANES - N_EXPERTS), F32)], axis=1)
        wrh = wr.astype(BF16)
        wrl = (wr - wrh.astype(F32)).astype(BF16)
        br = jnp.concatenate([b_router[l], jnp.zeros((LANES - N_EXPERTS,), F32)])[None]

        zp, kpt32, vpt32, vpt16, gsp = _proj(xp, ln1l, wmain, wsh, wsl, bias, qn, kn, r64, e64, ROW_TILE)
        zp3 = zp.reshape(b, lp, N_ZB * D_MODEL)
        hnp, kap, qap, c_p, n_p, m_p = _mlstm_prompt(zp3, gsp.reshape(b, lp, LANES), hnw, _tril(M_CHUNK))
        hfp = _fox_prompt(zp3, qap, kap, vpt16)
        x1p, tpn, lgp = _merge(xp, hnp.reshape(tp, D_MODEL), zp, hfp.reshape(tp, D_MODEL), wbm, wbf, wo,
                               ln2[l][None], wrh, wrl, br, ROW_TILE)
        unpad = lambda a: jnp.transpose(a.reshape(F_HEADS, F_DH, b, lp)[:, :, :, PAD_FRONT:], (2, 3, 0, 1))
        outs["kp"].append(unpad(kpt32))
        outs["vp"].append(unpad(vpt32))
        outs["lfp"].append(gsp.reshape(b, lp, LANES)[:, PAD_FRONT:, G_FF:G_END])
        outs["cp"].append(c_p)
        outs["np"].append(n_p[:, :, 0, :])
        outs["mp"].append(m_p[:, :, 0, 0])

        zs, kst32, vst32, _, gss = _proj(xs, ln1l, wmain, wsh, wsl, bias, qn, kn, r64, e64, db)
        vs32 = vst32.T
        m_in = jnp.concatenate([state_m[l], jnp.zeros((db, LANES - M_HEADS), F32)], axis=1)
        hns, c_s, n_s, m_s = _mlstm_step(zs, gss, m_in, state_n[l].reshape(db, M_HEADS * M_DK), state_C[l], hnw)
        n_s = n_s.reshape(db, M_HEADS, M_DK)
        q3 = zs[:, ZB_FQ * D_MODEL:(ZB_FQ + 1) * D_MODEL].astype(F32).reshape(db, 1, D_MODEL)
        kn3 = zs[:, ZB_FK * D_MODEL:(ZB_FK + 1) * D_MODEL].astype(F32).reshape(db, 1, D_MODEL)
        hfs = _fox_decode(page_table, q3, kn3, vs32.reshape(db, 1, D_MODEL),
                          jnp.broadcast_to(gss[:, G_FF:G_END, None], (db, F_HEADS, LANES)),
                          jnp.transpose(cache_k[l], (0, 2, 3, 1)), jnp.transpose(cache_v[l], (0, 2, 3, 1)),
                          jnp.transpose(cache_logf[l], (0, 2, 1)))
        x1s, tsn, lgs = _merge(xs, hns, zs, hfs.reshape(db, D_MODEL).astype(BF16), wbm, wbf, wo,
                               ln2[l][None], wrh, wrl, br, db)
        outs["ks"].append(jnp.transpose(kst32.reshape(F_HEADS, F_DH, db), (2, 0, 1)).reshape(db, 1, F_HEADS, F_DH))
        outs["vs"].append(jnp.transpose(vst32.reshape(F_HEADS, F_DH, db), (2, 0, 1)).reshape(db, 1, F_HEADS, F_DH))
        outs["lfs"].append(gss[:, G_FF:G_END].reshape(db, 1, F_HEADS))
        outs["cs"].append(c_s)
        outs["ns"].append(n_s)
        outs["ms"].append(m_s[:, :M_HEADS])

        if l < depth - 1:
            raise NotImplementedError("deeper stacks need the un-normalised residual stream as well")
        lg_all = jnp.concatenate([lgp, lgs], axis=0)
        y_prompt, y_sample = _moe_layer(tpn, tsn, x1p, x1s, lg_all, w_exp_in[l], b_exp_in[l], w_exp_out[l],
                                        b_exp_out[l], ln_final[None], b, lp)

    st = lambda k: jnp.stack(outs[k])
    return (y_prompt, y_sample.reshape(db, 1, D_MODEL), st("kp"), st("vp"), st("lfp"), st("ks"), st("vs"), st("lfs"),
            st("cp"), st("np"), st("mp"), st("cs"), st("ns"), st("ms"))
```

```python
import functools

import numpy as np
import jax
import jax.numpy as jnp
from jax import lax
from jax.experimental import pallas as pl
from jax.experimental.pallas import tpu as pltpu

F32 = jnp.float32
BF16 = jnp.bfloat16
I32 = jnp.int32

D_MODEL = 1024
N_META = 16
M_HEADS, M_DK, M_DV = 4, 128, 256
F_HEADS, F_DH = 16, 64
N_EXPERTS, TOP_K, D_FF = 32, 4, 1024
GATE_CAP = 15.0
SWIGLU_LIMIT = 7.0
SWIGLU_ALPHA = 1.702
EPS = 1e-6
ATTN_SCALE = F_DH ** -0.5
LOG2E = 1.4426950408889634
KEY_HIDE = 1e30

LANES = 128
ATT_BLOCK = 256
ATT_SUB = 128
PAD_FRONT = ATT_BLOCK - N_META
M_CHUNK = 256
MLSTM_SEQS = 1
ROW_TILE = 512
EXP_TILE = 256
COMB_TILE = 128
PAD_TILES = ATT_BLOCK // COMB_TILE
VMEM_LIMIT = 56 * 1024 * 1024

G_IG = 0
G_LF = M_HEADS
G_FF = 2 * M_HEADS
G_END = G_FF + F_HEADS

WB_MQK, WB_MV, WB_MO, WB_FQ, WB_FK, WB_FV, WB_GA, WB_GB = range(8)
N_WB = 8
ZB_MQK, ZB_MV, ZB_MO, ZB_FQ, ZB_FK, ZB_GA, ZB_GB = range(7)
N_ZB = 7


def _dot(a, b):
    return jnp.dot(a, b, preferred_element_type=F32)


def _dot_nt(a, b):
    return lax.dot_general(a, b, (((1,), (1,)), ((), ())), preferred_element_type=F32)


def _dot_tn(a, b):
    return lax.dot_general(a, b, (((0,), (0,)), ((), ())), preferred_element_type=F32)


def _split2(x):
    hi = x.astype(BF16)
    lo = (x - hi.astype(F32)).astype(BF16)
    return hi, lo


def _split3(x):
    a = x.astype(BF16)
    r = x - a.astype(F32)
    b = r.astype(BF16)
    c = (r - b.astype(F32)).astype(BF16)
    return a, b, c


def _dot_exact_rhs01(parts, m01):
    acc = _dot(parts[0], m01)
    for p in parts[1:]:
        acc = acc + _dot(p, m01)
    return acc


def _log_sigmoid(x):
    return jnp.minimum(x, 0.0) - jnp.log1p(jnp.exp(-jnp.abs(x)))


def _cparams(sem):
    return pltpu.CompilerParams(dimension_semantics=sem, vmem_limit_bytes=VMEM_LIMIT)


def _head_norm(y, g, r_ref, e_ref):
    ms = _dot((y * y).astype(BF16), r_ref[...])
    rs = lax.rsqrt(ms + EPS)
    rsx = _dot_exact_rhs01(_split2(rs), e_ref[...])
    return y * rsx * g


def _proj_kernel(x_ref, ln_ref, w_ref, wsh_ref, wsl_ref, bias_ref, qn_ref, kn_ref, r_ref, e_ref,
                 z_ref, kt32_ref, vt32_ref, vt16_ref, gs_ref):
    x = x_ref[...]
    ms = jnp.mean(x * x, axis=-1, keepdims=True)
    h = x * lax.rsqrt(ms + EPS) * ln_ref[...]
    hh, hl = _split2(h)
    wsh = wsh_ref[...]
    g = _dot(hh, wsh) + _dot(hl, wsh) + _dot(hh, wsl_ref[...]) + bias_ref[...]
    lane = lax.broadcasted_iota(I32, g.shape, 1)
    cap = GATE_CAP * jnp.tanh(g / GATE_CAP)
    ls = _log_sigmoid(jnp.where(lane < G_FF, cap, g))
    gs_ref[...] = jnp.where(lane < G_LF, cap, jnp.where(lane < G_END, ls, 0.0))

    def y_of(wb):
        return _dot(hh, w_ref[wb])

    def put(zb, val):
        z_ref[:, zb * D_MODEL:(zb + 1) * D_MODEL] = val.astype(BF16)

    y = y_of(WB_MQK)
    col = lax.broadcasted_iota(I32, y.shape, 1)
    put(ZB_MQK, jnp.where(col < M_HEADS * M_DK, y * (M_DK ** -0.5), y))
    put(ZB_MV, y_of(WB_MV))
    put(ZB_MO, jax.nn.sigmoid(y_of(WB_MO)))
    put(ZB_FQ, _head_norm(y_of(WB_FQ), qn_ref[...], r_ref, e_ref) * (ATTN_SCALE * LOG2E))
    kn = _head_norm(y_of(WB_FK), kn_ref[...], r_ref, e_ref)
    kt32_ref[...] = kn.T
    put(ZB_FK, kn)
    vt = y_of(WB_FV).T
    vt32_ref[...] = vt
    vt16_ref[...] = vt.astype(BF16)
    put(ZB_GA, jax.nn.sigmoid(y_of(WB_GA)))
    put(ZB_GB, jax.nn.sigmoid(y_of(WB_GB)))


def _proj(x, ln, wmain, wsh, wsl, bias, qn, kn, r64, e64, tm):
    t = x.shape[0]
    row = lambda i: (i, 0)
    col = lambda i: (0, i)
    fix = lambda i: (0, 0)
    return pl.pallas_call(
        _proj_kernel,
        grid=(t // tm,),
        in_specs=[
            pl.BlockSpec((tm, D_MODEL), row),
            pl.BlockSpec((1, D_MODEL), fix),
            pl.BlockSpec((N_WB, D_MODEL, D_MODEL), lambda i: (0, 0, 0), pipeline_mode=pl.Buffered(1)),
            pl.BlockSpec((D_MODEL, LANES), fix),
            pl.BlockSpec((D_MODEL, LANES), fix),
            pl.BlockSpec((1, LANES), fix),
            pl.BlockSpec((1, D_MODEL), fix),
            pl.BlockSpec((1, D_MODEL), fix),
            pl.BlockSpec((D_MODEL, LANES), fix),
            pl.BlockSpec((LANES, D_MODEL), fix),
        ],
        out_specs=[
            pl.BlockSpec((tm, N_ZB * D_MODEL), row),
            pl.BlockSpec((D_MODEL, tm), col),
            pl.BlockSpec((D_MODEL, tm), col),
            pl.BlockSpec((D_MODEL, tm), col),
            pl.BlockSpec((tm, LANES), row),
        ],
        out_shape=[
            jax.ShapeDtypeStruct((t, N_ZB * D_MODEL), BF16),
            jax.ShapeDtypeStruct((D_MODEL, t), F32),
            jax.ShapeDtypeStruct((D_MODEL, t), F32),
            jax.ShapeDtypeStruct((D_MODEL, t), BF16),
            jax.ShapeDtypeStruct((t, LANES), F32),
        ],
        compiler_params=_cparams(("parallel",)),
        name="proj",
    )(x, ln, wmain, wsh, wsl, bias, qn, kn, r64, e64)


def _mlstm_kernel(qk_ref, v_ref, gs_ref, hnw_ref, tril_ref, place_ref,
                  hn_ref, ka_ref, qa_ref, c_out, n_out, m_out,
                  c_scr, n_scr, m_scr, crow_scr):
    c = pl.program_id(1)
    lc = M_CHUNK

    @pl.when(c == 0)
    def _():
        c_scr[...] = jnp.zeros_like(c_scr)
        n_scr[...] = jnp.zeros_like(n_scr)
        m_scr[...] = jnp.zeros_like(m_scr)
        crow_scr[...] = jnp.zeros_like(crow_scr)

    lane = lax.broadcasted_iota(I32, (lc, LANES), 1)
    pos = c * lc + lax.broadcasted_iota(I32, (lc, LANES), 0)
    valid = pos >= PAD_FRONT
    is_ig = lane < G_LF
    row_t = lax.broadcasted_iota(I32, (lc, lc), 0)
    col_s = lax.broadcasted_iota(I32, (lc, lc), 1)
    causal = col_s <= row_t

    def one_sequence(sq):
        g = gs_ref[sq]
        gsum = jnp.where(valid & jnp.logical_not(is_ig), g, 0.0)
        bcum = _dot_exact_rhs01_lhs(tril_ref[...], _split3(gsum))
        comb = jnp.where(is_ig, jnp.where(valid, g, -jnp.inf), bcum)
        comb_t = comb.T
        qk = qk_ref[sq]
        vv = v_ref[sq]
        for h in range(M_HEADS):
            q = qk[:, h * M_DK:(h + 1) * M_DK]
            k = qk[:, (M_HEADS + h) * M_DK:(M_HEADS + h + 1) * M_DK]
            v = vv[:, h * M_DV:(h + 1) * M_DV]
            m_prev = m_scr[sq, h, 0:1, 0:1]
            b_col = bcum[:, G_LF + h:G_LF + h + 1]
            ig_col = comb[:, G_IG + h:G_IG + h + 1]
            b_row = comb_t[G_LF + h:G_LF + h + 1, :]
            ig_row = comb_t[G_IG + h:G_IG + h + 1, :]

            log_d = jnp.where(causal, b_col - b_row + ig_row, -jnp.inf)
            inter = b_col + m_prev
            m_row = jnp.maximum(inter, jnp.max(log_d, axis=1, keepdims=True))
            s = _dot_nt(q, k) * jnp.exp(log_d - m_row)
            w_int = jnp.exp(inter - m_row)
            cmat = c_scr[sq, h]
            nvec = n_scr[sq, h, 0:1, :]
            num = _dot(s.astype(BF16), v) + w_int * _dot_nt(q, cmat.astype(BF16))
            den = jnp.sum(s, axis=1, keepdims=True) + w_int * jnp.sum(q.astype(F32) * nvec, axis=1, keepdims=True)
            hh = num / jnp.maximum(jnp.abs(den), jnp.exp(-m_row))
            hn = (hh * lax.rsqrt(jnp.mean(hh * hh, axis=1, keepdims=True) + EPS)
                  * hnw_ref[:, h * M_DV:(h + 1) * M_DV])
            hn_ref[sq, :, h * M_DV:(h + 1) * M_DV] = hn.astype(BF16)

            b_last = b_col[lc - 1:lc, :]
            m_new = jnp.maximum(b_last + m_prev, jnp.max(b_last - b_row + ig_row, axis=1, keepdims=True))
            decay = jnp.exp(b_last + m_prev - m_new)
            ws_col = jnp.exp(b_last - b_col + ig_col - m_new)
            kw = k.astype(F32) * ws_col
            c_scr[sq, h] = decay * cmat + _dot_tn(v, kw.astype(BF16))
            n_scr[sq, h] = jnp.broadcast_to(decay * nvec + jnp.sum(kw, axis=0, keepdims=True), n_scr.shape[2:])
            m_scr[sq, h] = jnp.broadcast_to(m_new, m_scr.shape[2:])

        cg = bcum + crow_scr[sq, 0:1, :]
        crow_scr[sq] = jnp.broadcast_to(cg[lc - 1:lc, :], crow_scr.shape[1:])
        p0, p1, p2 = _split3(cg * LOG2E)
        p0 = jnp.where(lane == 0, jnp.ones_like(p0), p0)
        hide = jnp.logical_not(valid) & (lane >= G_FF) & (lane < G_END)
        p0k = jnp.where(hide, jnp.full_like(p0, KEY_HIDE), p0)
        ka_ref[sq] = (_dot(p0k, place_ref[0]) + _dot(p1, place_ref[1]) + _dot(p2, place_ref[2])).astype(BF16)
        qa_ref[sq] = (_dot(p0, place_ref[3]) + _dot(p1, place_ref[4]) + _dot(p2, place_ref[5])).astype(BF16)

    for sq in range(MLSTM_SEQS):
        one_sequence(sq)

    @pl.when(c == pl.num_programs(1) - 1)
    def _():
        c_out[...] = c_scr[...]
        n_out[...] = n_scr[...]
        m_out[...] = m_scr[...]


def _dot_exact_rhs01_lhs(m01, parts):
    acc = _dot(m01, parts[0])
    for p in parts[1:]:
        acc = acc + _dot(m01, p)
    return acc


BIAS_LANES = 6


def _bias_placement():
    pm = np.zeros((6, LANES, LANES), np.float32)
    for h in range(F_HEADS):
        base = BIAS_LANES * h
        for t in range(3):
            pm[t, G_FF + h, base + t] = -1.0
            pm[0, 0, base + 3 + t] = 1.0
            pm[3 + t, G_FF + h, base + 3 + t] = 1.0
            pm[3, 0, base + t] = 1.0
    return jnp.asarray(pm, BF16)


def _mlstm_prompt(z3, gs3, hnw, tril):
    b, lp, _ = z3.shape
    lc = M_CHUNK
    nc = lp // lc
    sq = MLSTM_SEQS
    assert b % sq == 0
    return pl.pallas_call(
        _mlstm_kernel,
        grid=(b // sq, nc),
        in_specs=[
            pl.BlockSpec((sq, lc, D_MODEL), lambda i, c: (i, c, ZB_MQK)),
            pl.BlockSpec((sq, lc, D_MODEL), lambda i, c: (i, c, ZB_MV)),
            pl.BlockSpec((sq, lc, LANES), lambda i, c: (i, c, 0)),
            pl.BlockSpec((1, D_MODEL), lambda i, c: (0, 0)),
            pl.BlockSpec((lc, lc), lambda i, c: (0, 0)),
            pl.BlockSpec((6, LANES, LANES), lambda i, c: (0, 0, 0)),
        ],
        out_specs=[
            pl.BlockSpec((sq, lc, D_MODEL), lambda i, c: (i, c, 0)),
            pl.BlockSpec((sq, lc, LANES), lambda i, c: (i, c, 0)),
            pl.BlockSpec((sq, lc, LANES), lambda i, c: (i, c, 0)),
            pl.BlockSpec((sq, M_HEADS, M_DV, M_DK), lambda i, c: (i, 0, 0, 0)),
            pl.BlockSpec((sq, M_HEADS, 8, M_DK), lambda i, c: (i, 0, 0, 0)),
            pl.BlockSpec((sq, M_HEADS, 8, LANES), lambda i, c: (i, 0, 0, 0)),
        ],
        out_shape=[
            jax.ShapeDtypeStruct((b, lp, D_MODEL), BF16),
            jax.ShapeDtypeStruct((b, lp, LANES), BF16),
            jax.ShapeDtypeStruct((b, lp, LANES), BF16),
            jax.ShapeDtypeStruct((b, M_HEADS, M_DV, M_DK), F32),
            jax.ShapeDtypeStruct((b, M_HEADS, 8, M_DK), F32),
            jax.ShapeDtypeStruct((b, M_HEADS, 8, LANES), F32),
        ],
        scratch_shapes=[
            pltpu.VMEM((sq, M_HEADS, M_DV, M_DK), F32),
            pltpu.VMEM((sq, M_HEADS, 8, M_DK), F32),
            pltpu.VMEM((sq, M_HEADS, 8, LANES), F32),
            pltpu.VMEM((sq, 8, LANES), F32),
        ],
        compiler_params=_cparams(("parallel", "arbitrary")),
        name="mlstm",
    )(z3, z3, gs3, hnw, tril, _bias_placement())


MSTEP_SEQS = 8


def _mstep_kernel(qk_ref, v_ref, gs_ref, m_ref, n_ref, c_ref, hnw_ref,
                  hn_ref, c_out, n_out, m_out):
    sb = MSTEP_SEQS
    qk = qk_ref[...]
    vv = v_ref[...]
    g = gs_ref[...]
    row = lax.broadcasted_iota(I32, (sb, 1), 0)
    m_new_all = jnp.zeros((sb, LANES), F32)
    lane = lax.broadcasted_iota(I32, (sb, LANES), 1)
    for h in range(M_HEADS):
        q = qk[:, h * M_DK:(h + 1) * M_DK]
        k = qk[:, (M_HEADS + h) * M_DK:(M_HEADS + h + 1) * M_DK]
        v = vv[:, h * M_DV:(h + 1) * M_DV]
        ig = g[:, G_IG + h:G_IG + h + 1]
        lf = g[:, G_LF + h:G_LF + h + 1]
        m_prev = m_ref[:, h:h + 1]
        inter = lf + m_prev
        m_row = jnp.maximum(inter, ig)
        d = jnp.exp(ig - m_row)
        w_int = jnp.exp(inter - m_row)
        qf = q.astype(F32)
        kf = k.astype(F32)
        s = jnp.sum(qf * kf, axis=1, keepdims=True) * d
        nvec = n_ref[:, h * M_DK:(h + 1) * M_DK]
        den = s + w_int * jnp.sum(qf * nvec, axis=1, keepdims=True)
        kw = (kf * d).astype(BF16)
        hrows = []
        for i in range(sb):
            cmat = c_ref[i, h]
            sel = row == i
            qi = jnp.where(sel, q, jnp.zeros_like(q))
            qc = _dot_nt(qi, cmat.astype(BF16))
            hrows.append(jnp.where(sel, qc, 0.0))
            vi = jnp.where(sel, v, jnp.zeros_like(v))
            c_out[i, h] = w_int[i:i + 1, :] * cmat + _dot_tn(vi, kw)
        qc_all = functools.reduce(lambda a, b: a + b, hrows)
        num = s * v.astype(F32) + w_int * qc_all
        hh = num / jnp.maximum(jnp.abs(den), jnp.exp(-m_row))
        hn = hh * lax.rsqrt(jnp.mean(hh * hh, axis=1, keepdims=True) + EPS) * hnw_ref[:, h * M_DV:(h + 1) * M_DV]
        hn_ref[:, h * M_DV:(h + 1) * M_DV] = hn.astype(BF16)
        n_out[:, h * M_DK:(h + 1) * M_DK] = w_int * nvec + kf * d
        m_new_all = jnp.where(lane == h, m_row, m_new_all)
    m_out[...] = m_new_all


def _mlstm_step(zs, gss, m_in, n_in, c_in, hnw):
    db = zs.shape[0]
    sb = MSTEP_SEQS
    return pl.pallas_call(
        _mstep_kernel,
        grid=(db // sb,),
        in_specs=[
            pl.BlockSpec((sb, D_MODEL), lambda i: (i, ZB_MQK)),
            pl.BlockSpec((sb, D_MODEL), lambda i: (i, ZB_MV)),
            pl.BlockSpec((sb, LANES), lambda i: (i, 0)),
            pl.BlockSpec((sb, LANES), lambda i: (i, 0)),
            pl.BlockSpec((sb, M_HEADS * M_DK), lambda i: (i, 0)),
            pl.BlockSpec((sb, M_HEADS, M_DV, M_DK), lambda i: (i, 0, 0, 0)),
            pl.BlockSpec((1, D_MODEL), lambda i: (0, 0)),
        ],
        out_specs=[
            pl.BlockSpec((sb, D_MODEL), lambda i: (i, 0)),
            pl.BlockSpec((sb, M_HEADS, M_DV, M_DK), lambda i: (i, 0, 0, 0)),
            pl.BlockSpec((sb, M_HEADS * M_DK), lambda i: (i, 0)),
            pl.BlockSpec((sb, LANES), lambda i: (i, 0)),
        ],
        out_shape=[
            jax.ShapeDtypeStruct((db, D_MODEL), BF16),
            jax.ShapeDtypeStruct((db, M_HEADS, M_DV, M_DK), F32),
            jax.ShapeDtypeStruct((db, M_HEADS * M_DK), F32),
            jax.ShapeDtypeStruct((db, LANES), F32),
        ],
        compiler_params=_cparams(("parallel",)),
        name="mstep",
    )(zs, zs, gss, m_in, n_in, c_in, hnw)


def _fox_kernel(qi_tab, ki_tab, q_ref, qa_ref, k_ref, ka_ref, vt_ref, o_ref,
                qb_scr, acc_scr, m_scr, l_scr, a_scr, st_scr, pt_scr):
    step = pl.program_id(1)
    qi = qi_tab[step]
    ki = ki_tab[step]
    tq = tk = ATT_BLOCK

    @pl.when(ki == 0)
    def _():
        acc_scr[...] = jnp.zeros_like(acc_scr)
        m_scr[...] = jnp.full_like(m_scr, -jnp.inf)
        l_scr[...] = jnp.zeros_like(l_scr)
        lane = lax.broadcasted_iota(I32, (tq, LANES), 1)
        qa = qa_ref[0]
        for h in range(F_HEADS):
            q2 = q_ref[0, :, (h // 2) * LANES:(h // 2 + 1) * LANES]
            own = (lane >= (h % 2) * F_DH) & (lane < (h % 2 + 1) * F_DH)
            bias = (lane >= h * BIAS_LANES) & (lane < (h + 1) * BIAS_LANES)
            qb_scr[h] = jnp.concatenate([jnp.where(own, q2, jnp.zeros_like(q2)),
                                         jnp.where(bias, qa, jnp.zeros_like(qa))], axis=1)

    def sweep(masked):
        ka = ka_ref[0]
        if masked:
            key = ki * tk + lax.broadcasted_iota(I32, (tk, tq), 0)
            qry = qi * tq + lax.broadcasted_iota(I32, (tk, tq), 1)
            visible = key <= qry
        def score_stage(h):
            kb = jnp.concatenate([k_ref[0, :, (h // 2) * LANES:(h // 2 + 1) * LANES], ka], axis=1)
            st_scr[h] = _dot_nt(kb, qb_scr[h])

        def softmax_stage(h):
            for c0 in range(0, tq, LANES):
                cs = slice(c0, c0 + LANES)
                def scores(r0):
                    st = st_scr[h, r0:r0 + ATT_SUB, cs]
                    return jnp.where(visible[r0:r0 + ATT_SUB, cs], st, -jnp.inf) if masked else st

                m_prev = m_scr[h, 0:1, cs]
                m_next = m_prev
                for r0 in range(0, tk, ATT_SUB):
                    m_next = jnp.maximum(m_next, jnp.max(scores(r0), axis=0, keepdims=True))
                alpha = jnp.exp2(m_prev - m_next)
                l_new = alpha * l_scr[h, 0:1, cs]
                for r0 in range(0, tk, ATT_SUB):
                    pt = jnp.exp2(scores(r0) - m_next)
                    l_new = l_new + jnp.sum(pt, axis=0, keepdims=True)
                    pt_scr[h, r0:r0 + ATT_SUB, cs] = pt.astype(BF16)
                l_scr[h, :, cs] = jnp.broadcast_to(l_new, (8, LANES))
                m_scr[h, :, cs] = jnp.broadcast_to(m_next, (8, LANES))
                a_scr[h, :, cs] = jnp.broadcast_to(alpha, (8, LANES))
        def value_stage(h):
            acc_scr[h] = a_scr[h, 0:1, :] * acc_scr[h] + _dot(vt_ref[h * F_DH:(h + 1) * F_DH, :], pt_scr[h])

        for stage in (score_stage, softmax_stage, value_stage):
            for h in range(F_HEADS):
                stage(h)

    edge = ki == qi

    @pl.when(edge)
    def _():
        sweep(True)

    @pl.when(jnp.logical_not(edge))
    def _():
        sweep(False)

    @pl.when(ki == qi)
    def _():
        for p in range(F_HEADS // 2):
            o2 = jnp.concatenate([acc_scr[2 * p] * (1.0 / l_scr[2 * p, 0:1, :]),
                                  acc_scr[2 * p + 1] * (1.0 / l_scr[2 * p + 1, 0:1, :])], axis=0)
            o_ref[0, :, p * LANES:(p + 1) * LANES] = o2.T.astype(BF16)


def _fox_prompt(z3, qa, ka, vt16):
    b, lp, _ = z3.shape
    blk = ATT_BLOCK
    nb = lp // blk
    qi_tab = np.concatenate([np.full((i + 1,), i, np.int32) for i in range(nb)])
    ki_tab = np.concatenate([np.arange(i + 1, dtype=np.int32) for i in range(nb)])
    grid_spec = pltpu.PrefetchScalarGridSpec(
        num_scalar_prefetch=2,
        grid=(b, len(qi_tab)),
        in_specs=[
            pl.BlockSpec((1, blk, D_MODEL), lambda i, s, qt, kt: (i, qt[s], ZB_FQ)),
            pl.BlockSpec((1, blk, LANES), lambda i, s, qt, kt: (i, qt[s], 0)),
            pl.BlockSpec((1, blk, D_MODEL), lambda i, s, qt, kt: (i, kt[s], ZB_FK)),
            pl.BlockSpec((1, blk, LANES), lambda i, s, qt, kt: (i, kt[s], 0)),
            pl.BlockSpec((D_MODEL, blk), lambda i, s, qt, kt: (0, i * nb + kt[s])),
        ],
        out_specs=pl.BlockSpec((1, blk, D_MODEL), lambda i, s, qt, kt: (i, qt[s], 0)),
        scratch_shapes=[
            pltpu.VMEM((F_HEADS, blk, 2 * LANES), BF16),
            pltpu.VMEM((F_HEADS, F_DH, blk), F32),
            pltpu.VMEM((F_HEADS, 8, blk), F32),
            pltpu.VMEM((F_HEADS, 8, blk), F32),
            pltpu.VMEM((F_HEADS, 8, blk), F32),
            pltpu.VMEM((F_HEADS, blk, blk), F32),
            pltpu.VMEM((F_HEADS, blk, blk), BF16),
        ],
    )
    return pl.pallas_call(
        _fox_kernel,
        grid_spec=grid_spec,
        out_shape=jax.ShapeDtypeStruct((b, lp, D_MODEL), BF16),
        compiler_params=_cparams(("parallel", "arbitrary")),
        name="fox",
    )(jnp.asarray(qi_tab), jnp.asarray(ki_tab), z3, qa, z3, ka, vt16)


DEC_PAGES = 16


def _lane_bcast_cols(row):
    full = jnp.broadcast_to(row, (LANES, D_MODEL)).T
    return full.reshape(F_HEADS, F_DH, LANES)


def _per_head(x):
    return x.reshape(F_HEADS, 1, LANES)


def _dec_kernel(pt_ref, q_ref, kn_ref, vn_ref, lfn_ref, triu_ref, ones_ref, *rest):
    np_ = DEC_PAGES
    k_refs, v_refs, lf_refs = rest[:np_], rest[np_:2 * np_], rest[2 * np_:3 * np_]
    o_ref, qb_scr, acc_scr, m_scr, l_scr, c_scr = rest[3 * np_:]
    p = pl.program_id(1)

    @pl.when(p == 0)
    def _():
        qb_scr[...] = _lane_bcast_cols(q_ref[0])
        acc_scr[...] = jnp.zeros_like(acc_scr)
        m_scr[...] = jnp.full_like(m_scr, -jnp.inf)
        l_scr[...] = jnp.zeros_like(l_scr)
        c_scr[...] = jnp.zeros_like(c_scr)

    qb = qb_scr[...]
    carry = c_scr[...]
    us = []
    for i in range(np_):
        cum = _dot_exact_rhs01(_split3(lf_refs[i][0]), triu_ref[...]) + carry
        carry = jnp.broadcast_to(cum[:, LANES - 1:LANES], carry.shape)
        s = jnp.sum(qb * k_refs[i][0], axis=1)
        us.append(s - cum * LOG2E)
    c_scr[...] = carry
    m_prev = m_scr[...]
    m_cur = functools.reduce(jnp.maximum, [jnp.max(u, axis=1, keepdims=True) for u in us])
    m_next = jnp.maximum(m_prev, m_cur)
    alpha = jnp.exp2(m_prev - m_next)
    acc = _per_head(alpha) * acc_scr[...]
    l_new = alpha * l_scr[...]
    for i in range(np_):
        pr = jnp.exp2(us[i] - m_next)
        l_new = l_new + jnp.sum(pr, axis=1, keepdims=True)
        acc = acc + _per_head(pr) * v_refs[i][0]
    acc_scr[...] = acc
    l_scr[...] = l_new
    m_scr[...] = m_next

    @pl.when(p == pl.num_programs(1) - 1)
    def _():
        s_new = jnp.sum(qb * _lane_bcast_cols(kn_ref[0]), axis=1)
        u_new = s_new - (carry + lfn_ref[0]) * LOG2E
        m_fin = jnp.maximum(m_next, u_new)
        a_fin = jnp.exp2(m_next - m_fin)
        pn = jnp.exp2(u_new - m_fin)
        l_fin = a_fin * l_new + pn
        tot = (_per_head(a_fin) * acc + _per_head(pn * (1.0 / LANES)) * _lane_bcast_cols(vn_ref[0])) / _per_head(l_fin)
        out8 = _dot_nt_exact_lhs01(ones_ref[...], _split3(tot.reshape(D_MODEL, LANES)))
        o_ref[0] = out8[0:1, :]


def _dot_nt_exact_lhs01(m01, parts):
    acc = _dot_nt(m01, parts[0])
    for part in parts[1:]:
        acc = acc + _dot_nt(m01, part)
    return acc


def _fox_decode(page_table, q3, kn3, vn3, lfn3, cache_kt, cache_vt, cache_lft):
    db, npg = page_table.shape
    pg = cache_kt.shape[3]
    assert pg == LANES and npg % DEC_PAGES == 0
    row3 = lambda s, p, pt: (s, 0, 0)
    fix = lambda s, p, pt: (0, 0)

    def page(i, nd):
        return lambda s, p, pt: (pt[s, p * DEC_PAGES + i],) + (0,) * nd

    in_specs = [
        pl.BlockSpec((1, 1, D_MODEL), row3),
        pl.BlockSpec((1, 1, D_MODEL), row3),
        pl.BlockSpec((1, 1, D_MODEL), row3),
        pl.BlockSpec((1, F_HEADS, LANES), row3),
        pl.BlockSpec((LANES, LANES), fix),
        pl.BlockSpec((8, LANES), fix),
    ]
    in_specs += [pl.BlockSpec((1, F_HEADS, F_DH, LANES), page(i, 3)) for i in range(DEC_PAGES)]
    in_specs += [pl.BlockSpec((1, F_HEADS, F_DH, LANES), page(i, 3)) for i in range(DEC_PAGES)]
    in_specs += [pl.BlockSpec((1, F_HEADS, LANES), page(i, 2)) for i in range(DEC_PAGES)]
    grid_spec = pltpu.PrefetchScalarGridSpec(
        num_scalar_prefetch=1,
        grid=(db, npg // DEC_PAGES),
        in_specs=in_specs,
        out_specs=pl.BlockSpec((1, 1, D_MODEL), row3),
        scratch_shapes=[
            pltpu.VMEM((F_HEADS, F_DH, LANES), F32),
            pltpu.VMEM((F_HEADS, F_DH, LANES), F32),
            pltpu.VMEM((F_HEADS, LANES), F32),
            pltpu.VMEM((F_HEADS, LANES), F32),
            pltpu.VMEM((F_HEADS, LANES), F32),
        ],
    )
    triu = jnp.asarray(np.triu(np.ones((LANES, LANES), np.float32)), BF16)
    ones = jnp.ones((8, LANES), BF16)
    return pl.pallas_call(
        _dec_kernel,
        grid_spec=grid_spec,
        out_shape=jax.ShapeDtypeStruct((db, 1, D_MODEL), F32),
        compiler_params=_cparams(("parallel", "arbitrary")),
        name="dec",
    )(page_table, q3, kn3, vn3, lfn3, triu, ones,
      *([cache_kt] * DEC_PAGES), *([cache_vt] * DEC_PAGES), *([cache_lft] * DEC_PAGES))


TOKEN_SUBLANES = D_MODEL // LANES


def _store_token_tiles(ref, x):
    n = x.shape[0]
    for g in range(TOKEN_SUBLANES):
        ref[pl.ds(g, n, stride=TOKEN_SUBLANES), :] = x[:, g * LANES:(g + 1) * LANES]


def _load_token_tiles(ref, n):
    return jnp.concatenate([ref[pl.ds(g, n, stride=TOKEN_SUBLANES), :] for g in range(TOKEN_SUBLANES)], axis=1)


def _merge_kernel(x_ref, hn_ref, so_ref, hf_ref, ga_ref, gb_ref, wbm_ref, wbf_ref, wo_ref, ln2_ref,
                  wrh_ref, wrl_ref, br_ref, x1_ref, t_ref, lg_ref):
    hm = hn_ref[...] * so_ref[...]
    ya = _dot(hm, wbm_ref[...])
    yb = _dot(hf_ref[...], wbf_ref[...])
    u = ga_ref[...].astype(F32) * ya + gb_ref[...].astype(F32) * yb
    x1 = x_ref[...] + _dot(u.astype(BF16), wo_ref[...])
    x1_ref[...] = x1
    t = x1 * lax.rsqrt(jnp.mean(x1 * x1, axis=-1, keepdims=True) + EPS) * ln2_ref[...]
    _store_token_tiles(t_ref, t)
    th, tl = _split2(t)
    wrh = wrh_ref[...]
    lg_ref[...] = _dot(th, wrh) + _dot(tl, wrh) + _dot(th, wrl_ref[...]) + br_ref[...]


def _merge(x, hn, z, hf, wbm, wbf, wo, ln2, wrh, wrl, br, tm):
    t = x.shape[0]
    row = lambda i: (i, 0)
    fix = lambda i: (0, 0)
    full = pl.BlockSpec((D_MODEL, D_MODEL), fix)
    return pl.pallas_call(
        _merge_kernel,
        grid=(t // tm,),
        in_specs=[
            pl.BlockSpec((tm, D_MODEL), row),
            pl.BlockSpec((tm, D_MODEL), row),
            pl.BlockSpec((tm, D_MODEL), lambda i: (i, ZB_MO)),
            pl.BlockSpec((tm, D_MODEL), row),
            pl.BlockSpec((tm, D_MODEL), lambda i: (i, ZB_GA)),
            pl.BlockSpec((tm, D_MODEL), lambda i: (i, ZB_GB)),
            full, full, full,
            pl.BlockSpec((1, D_MODEL), fix),
            pl.BlockSpec((D_MODEL, LANES), fix),
            pl.BlockSpec((D_MODEL, LANES), fix),
            pl.BlockSpec((1, LANES), fix),
        ],
        out_specs=[
            pl.BlockSpec((tm, D_MODEL), row),
            pl.BlockSpec((tm * TOKEN_SUBLANES, LANES), row),
            pl.BlockSpec((tm, LANES), row),
        ],
        out_shape=[
            jax.ShapeDtypeStruct((t, D_MODEL), F32),
            jax.ShapeDtypeStruct((t * TOKEN_SUBLANES, LANES), F32),
            jax.ShapeDtypeStruct((t, LANES), F32),
        ],
        compiler_params=_cparams(("parallel",)),
        name="merge",
    )(x, hn, z, hf, z, z, wbm, wbf, wo, ln2, wrh, wrl, br)


def _route_kernel(n_valid, lg_ref, tril_ref, ids_ref, wts_ref, rank_ref, cnt_ref, carry_scr):
    i = pl.program_id(0)

    @pl.when(i == 0)
    def _():
        carry_scr[...] = jnp.zeros_like(carry_scr)

    tm = lg_ref.shape[0]
    lane = lax.broadcasted_iota(I32, (tm, LANES), 1)
    lanef = lane.astype(F32)
    lg = jnp.where(lane < N_EXPERTS, lg_ref[...], -jnp.inf)
    vals, idxs, hots = [], [], []
    for _ in range(TOP_K):
        mx = jnp.max(lg, axis=1, keepdims=True)
        idx = jnp.min(jnp.where(lg == mx, lanef, float(LANES)), axis=1, keepdims=True)
        hot = lanef == idx
        lg = jnp.where(hot, -jnp.inf, lg)
        vals.append(mx)
        idxs.append(idx)
        hots.append(hot)
    es = [jnp.exp(v - vals[0]) for v in vals]
    tot = functools.reduce(lambda a, b: a + b, es)
    sel = functools.reduce(lambda a, b: a | b, hots)
    real = i * tm + lax.broadcasted_iota(I32, (tm, LANES), 0) < n_valid
    a01 = jnp.where(sel & real, 1.0, 0.0)
    before = _dot(tril_ref[...], a01.astype(BF16)) + carry_scr[0:1, :]
    carry_scr[...] = jnp.broadcast_to(carry_scr[0:1, :] + jnp.sum(a01, axis=0, keepdims=True), carry_scr.shape)
    ids = jnp.zeros((tm, LANES), I32)
    wts = jnp.zeros((tm, LANES), F32)
    rank = jnp.zeros((tm, LANES), I32)
    for kk in range(TOP_K):
        r = jnp.sum(jnp.where(hots[kk], before, 0.0), axis=1, keepdims=True)
        ids = jnp.where(lane == kk, idxs[kk].astype(I32), ids)
        wts = jnp.where(lane == kk, es[kk] / tot, wts)
        rank = jnp.where(lane == kk, r.astype(I32), rank)
    ids_ref[...] = ids
    wts_ref[...] = wts
    rank_ref[...] = rank
    cnt_ref[...] = carry_scr[...].astype(I32)


def _route(logits, tril_strict, tm, n_valid):
    t = logits.shape[0]
    row = lambda i: (i, 0)
    return pl.pallas_call(
        functools.partial(_route_kernel, n_valid),
        grid=(t // tm,),
        in_specs=[pl.BlockSpec((tm, LANES), row), pl.BlockSpec((tm, tm), lambda i: (0, 0))],
        out_specs=[pl.BlockSpec((tm, LANES), row)] * 3 + [pl.BlockSpec((8, LANES), lambda i: (0, 0))],
        out_shape=[
            jax.ShapeDtypeStruct((t, LANES), I32),
            jax.ShapeDtypeStruct((t, LANES), F32),
            jax.ShapeDtypeStruct((t, LANES), I32),
            jax.ShapeDtypeStruct((8, LANES), I32),
        ],
        scratch_shapes=[pltpu.VMEM((8, LANES), F32)],
        compiler_params=_cparams(("arbitrary",)),
        name="route",
    )(logits, tril_strict)


def _tile_rows(i):
    return pl.ds(pl.multiple_of(i * TOKEN_SUBLANES, TOKEN_SUBLANES), TOKEN_SUBLANES)


def _dispatch_kernel(ntp, pad_lo, pad_hi, n_pad, pos_ref, tp_ref, ts_ref, xs_ref, zero_scr, sem, zsem):
    tc = COMB_TILE
    i = pl.program_id(0)

    def scatter_from(t_ref):
        def issue(r, carry):
            src = t_ref.at[_tile_rows(r), :]
            for kk in range(TOP_K):
                pltpu.make_async_copy(src, xs_ref.at[_tile_rows(pos_ref[0, kk, r]), :], sem).start(priority=kk % 2)
            return carry

        lax.fori_loop(0, tc, issue, 0, unroll=2)

    @pl.when(i < ntp)
    def _():
        scatter_from(tp_ref)

    @pl.when(i >= ntp)
    def _():
        scatter_from(ts_ref)

    @pl.when(i == 0)
    def _():
        zero_scr[...] = jnp.zeros_like(zero_scr)

        def per_expert(e, carry):
            def per_slot(s, c2):
                pltpu.make_async_copy(zero_scr, xs_ref.at[_tile_rows(s), :], zsem).start()
                return c2
            return lax.fori_loop(pad_lo[e], pad_hi[e], per_slot, carry)

        lax.fori_loop(0, pad_lo.shape[0], per_expert, 0)

        def drain(s, carry):
            pltpu.make_async_copy(zero_scr, xs_ref.at[pl.ds(0, TOKEN_SUBLANES), :], zsem).wait()
            return carry

        lax.fori_loop(0, n_pad[0], drain, 0)

    for kk in range(TOP_K):
        pltpu.make_async_copy(tp_ref, xs_ref.at[pl.ds(0, tc * TOKEN_SUBLANES), :], sem).wait()


def _dispatch(pad_lo, pad_hi, n_pad, pos3, t_p, t_s, r_pad):
    tc = COMB_TILE
    ntp = t_p.shape[0] // (tc * TOKEN_SUBLANES)
    assert t_s.shape[0] == tc * TOKEN_SUBLANES and pos3.shape[0] == ntp + 1
    grid_spec = pltpu.PrefetchScalarGridSpec(
        num_scalar_prefetch=3,
        grid=(ntp + 1,),
        in_specs=[
            pl.BlockSpec((1, TOP_K, tc), lambda i, lo, hi, n: (i, 0, 0), memory_space=pltpu.SMEM),
            pl.BlockSpec((tc * TOKEN_SUBLANES, LANES), lambda i, lo, hi, n: (jnp.minimum(i, ntp - 1), 0)),
            pl.BlockSpec((tc * TOKEN_SUBLANES, LANES), lambda i, lo, hi, n: (0, 0)),
        ],
        out_specs=pl.BlockSpec(memory_space=pl.ANY),
        scratch_shapes=[pltpu.VMEM((TOKEN_SUBLANES, LANES), F32), pltpu.SemaphoreType.DMA, pltpu.SemaphoreType.DMA],
    )
    return pl.pallas_call(
        functools.partial(_dispatch_kernel, ntp),
        grid_spec=grid_spec,
        out_shape=jax.ShapeDtypeStruct((r_pad * TOKEN_SUBLANES, LANES), F32),
        compiler_params=_cparams(("arbitrary",)),
        name="dispatch",
    )(pad_lo, pad_hi, n_pad, pos3, t_p, t_s)


def _experts_kernel(te_ref, nu_ref, xs_ref, w1_ref, b1_ref, w2_ref, b2_ref, ys_ref, w1b_scr, w2b_scr):
    j = pl.program_id(0)
    prev = te_ref[jnp.maximum(j - 1, 0)]

    @pl.when((j == 0) | (te_ref[j] != prev))
    def _():
        w1b_scr[...] = w1_ref[0].astype(BF16)
        w2b_scr[...] = w2_ref[0].astype(BF16)

    @pl.when(j < nu_ref[0])
    def _():
        x = _load_token_tiles(xs_ref, EXP_TILE)
        hcat = _dot(x.astype(BF16), w1b_scr[...]) + b1_ref[0]
        g = jnp.minimum(hcat[:, :D_FF], SWIGLU_LIMIT)
        u = jnp.clip(hcat[:, D_FF:], -SWIGLU_LIMIT, SWIGLU_LIMIT)
        a = g * jax.nn.sigmoid(SWIGLU_ALPHA * g) * (u + 1.0)
        _store_token_tiles(ys_ref, _dot(a.astype(BF16), w2b_scr[...]) + b2_ref[0])

    @pl.when(j >= nu_ref[0])
    def _():
        ys_ref[...] = jnp.zeros_like(ys_ref)


def _experts(tile_expert, n_used, xs, w1, b1, w2, b2):
    r = xs.shape[0] // TOKEN_SUBLANES
    tm = EXP_TILE
    grid_spec = pltpu.PrefetchScalarGridSpec(
        num_scalar_prefetch=2,
        grid=(r // tm,),
        in_specs=[
            pl.BlockSpec((tm * TOKEN_SUBLANES, LANES), lambda j, te, nu: (jnp.minimum(j, nu[0] - 1), 0)),
            pl.BlockSpec((1, D_MODEL, 2 * D_FF), lambda j, te, nu: (te[j], 0, 0)),
            pl.BlockSpec((1, 1, 2 * D_FF), lambda j, te, nu: (te[j], 0, 0)),
            pl.BlockSpec((1, D_FF, D_MODEL), lambda j, te, nu: (te[j], 0, 0)),
            pl.BlockSpec((1, 1, D_MODEL), lambda j, te, nu: (te[j], 0, 0)),
        ],
        out_specs=pl.BlockSpec((tm * TOKEN_SUBLANES, LANES), lambda j, te, nu: (j, 0)),
        scratch_shapes=[
            pltpu.VMEM((D_MODEL, 2 * D_FF), BF16),
            pltpu.VMEM((D_FF, D_MODEL), BF16),
        ],
    )
    return pl.pallas_call(
        _experts_kernel,
        grid_spec=grid_spec,
        out_shape=jax.ShapeDtypeStruct((r * TOKEN_SUBLANES, LANES), F32),
        compiler_params=_cparams(("arbitrary",)),
        name="experts",
    )(tile_expert, n_used, xs, w1, b1.reshape(N_EXPERTS, 1, 2 * D_FF), w2, b2.reshape(N_EXPERTS, 1, D_MODEL))


def _combine_kernel(ntp, tps, pos_ref, ys_ref, x1p_ref, x1s_ref, wts_ref, lnf_ref, yp_ref, ysm_ref, rows_scr, sem):
    tc = COMB_TILE
    i = pl.program_id(0)
    is_prompt = i < ntp
    wanted = jnp.logical_or(jnp.logical_not(is_prompt), lax.rem(i, tps) >= PAD_TILES)

    def finish(x1):
        w = wts_ref[...]
        acc = x1
        for kk in range(TOP_K):
            acc = acc + w[:, kk:kk + 1] * _load_token_tiles(rows_scr.at[kk], tc)
        return acc * lax.rsqrt(jnp.mean(acc * acc, axis=-1, keepdims=True) + EPS) * lnf_ref[...]

    @pl.when(wanted)
    def _():
        def issue(r, carry):
            for kk in range(TOP_K):
                pltpu.make_async_copy(ys_ref.at[_tile_rows(pos_ref[0, kk, r]), :],
                                      rows_scr.at[kk, _tile_rows(r), :], sem).start(priority=kk % 2)
            return carry

        lax.fori_loop(0, tc, issue, 0, unroll=2)
        for kk in range(TOP_K):
            pltpu.make_async_copy(ys_ref.at[pl.ds(0, tc * TOKEN_SUBLANES), :], rows_scr.at[kk], sem).wait()

        @pl.when(is_prompt)
        def _():
            yp_ref[0] = finish(x1p_ref[...])

        @pl.when(jnp.logical_not(is_prompt))
        def _():
            ysm_ref[...] = finish(x1s_ref[...])


def _combine(pos3, ys, x1p, x1s, wts, lnf, b, lp):
    tc = COMB_TILE
    ntp = x1p.shape[0] // tc
    tps = lp // tc
    seq_tiles = tps - PAD_TILES
    assert x1s.shape[0] == tc and pos3.shape[0] == ntp + 1

    def yp_map(i):
        blk = jnp.where(i >= ntp, seq_tiles - 1, jnp.maximum(lax.rem(i, tps) - PAD_TILES, 0))
        return (jnp.minimum(i // tps, b - 1), blk, 0)

    return pl.pallas_call(
        functools.partial(_combine_kernel, ntp, tps),
        grid=(ntp + 1,),
        in_specs=[
            pl.BlockSpec((1, TOP_K, tc), lambda i: (i, 0, 0), memory_space=pltpu.SMEM),
            pl.BlockSpec(memory_space=pl.ANY),
            pl.BlockSpec((tc, D_MODEL), lambda i: (jnp.minimum(i, ntp - 1), 0)),
            pl.BlockSpec((tc, D_MODEL), lambda i: (0, 0)),
            pl.BlockSpec((tc, LANES), lambda i: (i, 0)),
            pl.BlockSpec((1, D_MODEL), lambda i: (0, 0)),
        ],
        out_specs=[
            pl.BlockSpec((1, tc, D_MODEL), yp_map),
            pl.BlockSpec((tc, D_MODEL), lambda i: (0, 0)),
        ],
        out_shape=[
            jax.ShapeDtypeStruct((b, seq_tiles * tc, D_MODEL), F32),
            jax.ShapeDtypeStruct((tc, D_MODEL), F32),
        ],
        scratch_shapes=[pltpu.VMEM((TOP_K, tc * TOKEN_SUBLANES, LANES), F32), pltpu.SemaphoreType.DMA],
        compiler_params=_cparams(("arbitrary",)),
        name="combine",
    )(pos3, ys, x1p, x1s, wts, lnf)


def _consts():
    r64 = np.zeros((D_MODEL, LANES), np.float32)
    e64 = np.zeros((LANES, D_MODEL), np.float32)
    for h in range(F_HEADS):
        r64[h * F_DH:(h + 1) * F_DH, h] = 1.0 / F_DH
        e64[h, h * F_DH:(h + 1) * F_DH] = 1.0
    return jnp.asarray(r64, BF16), jnp.asarray(e64, BF16), jnp.asarray(e64, F32)


def _tril(n, strict=False):
    return jnp.asarray(np.tril(np.ones((n, n), np.float32), -1 if strict else 0), BF16)


def _round_up(a, b):
    return (a + b - 1) // b * b


def _moe_layer(t_p, t_s, x1_p, x1_s, lg_all, w1, b1, w2, b2, lnf, b, lp):
    tmoe = x1_p.shape[0] + x1_s.shape[0]
    t_route = _round_up(tmoe, ROW_TILE)
    lg_pad = jnp.concatenate([lg_all, jnp.zeros((t_route - tmoe, LANES), F32)], axis=0)
    ids, wts, rank, cnt = _route(lg_pad, _tril(ROW_TILE, strict=True), ROW_TILE, tmoe)
    counts = cnt[0, :N_EXPERTS]
    pcounts = (counts + EXP_TILE - 1) // EXP_TILE * EXP_TILE
    ends = jnp.cumsum(pcounts)
    starts = ends - pcounts
    pos = starts[ids[:tmoe, :TOP_K]] + rank[:tmoe, :TOP_K]
    pos3 = pos.reshape(tmoe // COMB_TILE, COMB_TILE, TOP_K).transpose(0, 2, 1)
    r_pad = _round_up(TOP_K * tmoe + N_EXPERTS * (EXP_TILE - 1), EXP_TILE)
    n_tiles = r_pad // EXP_TILE
    tile_start = jnp.arange(n_tiles, dtype=I32) * EXP_TILE
    tile_expert = jnp.minimum(jnp.sum((ends[None, :] <= tile_start[:, None]).astype(I32), axis=1), N_EXPERTS - 1)
    n_used = (ends[-1] // EXP_TILE).astype(I32)[None]
    pad_lo = jnp.concatenate([starts + counts, ends[-1:]]).astype(I32)
    pad_hi = jnp.concatenate([ends, jnp.full((1,), r_pad, ends.dtype)]).astype(I32)
    n_pad = jnp.sum(pad_hi - pad_lo).astype(I32)[None]
    xsrt = _dispatch(pad_lo, pad_hi, n_pad, pos3, t_p, t_s, r_pad)
    ysrt = _experts(tile_expert, n_used, xsrt, w1, b1, w2, b2)
    return _combine(pos3, ysrt, x1_p, x1_s, wts, lnf, b, lp)


def kernel(x_prompt, x_sample, cache_k, cache_v, cache_logf, state_C, state_n, state_m, page_table, meta_tokens, ln1, w_in, mlstm_b_i, mlstm_b_f, mlstm_head_norm, fox_b_f, fox_q_norm, fox_k_norm, w_branch_mlstm, w_branch_fox, w_out, ln2, w_router, b_router, w_exp_in, b_exp_in, w_exp_out, b_exp_out, ln_final):
    depth = w_in.shape[0]
    b, seq, _ = x_prompt.shape
    db, ds, _ = x_sample.shape
    assert ds == 1 and seq % ATT_BLOCK == 0 and db % MSTEP_SEQS == 0
    l_true = seq + N_META
    lp = seq + ATT_BLOCK
    tp = b * lp
    n_phys, pg = cache_k.shape[1], cache_k.shape[2]
    r64, e64, e64f = _consts()

    xp = jnp.concatenate([jnp.zeros((b, PAD_FRONT, D_MODEL), F32),
                          jnp.broadcast_to(meta_tokens[None].astype(F32), (b, N_META, D_MODEL)),
                          x_prompt], axis=1).reshape(tp, D_MODEL)
    xs = x_sample.reshape(db, D_MODEL)

    outs = {k: [] for k in ("kp", "vp", "lfp", "ks", "vs", "lfs", "cp", "np", "mp", "cs", "ns", "ms")}
    offs = np.cumsum((0,) + (M_HEADS * M_DK, M_HEADS * M_DK, M_HEADS * M_DV, M_HEADS * M_DV, M_HEADS, M_HEADS,
                             F_HEADS * F_DH, F_HEADS * F_DH, F_HEADS * F_DH, F_HEADS, 2 * D_MODEL))
    seg = lambda w, i: w[:, offs[i]:offs[i + 1]]
    for l in range(depth):
        w = w_in[l]
        gates = seg(w, 10)
        wmain = jnp.stack([jnp.concatenate([seg(w, 0), seg(w, 1)], axis=1), seg(w, 2), seg(w, 3), seg(w, 6),
                           seg(w, 7), seg(w, 8), gates[:, :D_MODEL], gates[:, D_MODEL:]]).astype(BF16)
        wsm = jnp.concatenate([seg(w, 4), seg(w, 5), seg(w, 9),
                               jnp.zeros((D_MODEL, LANES - G_END), F32)], axis=1)
        wsh = wsm.astype(BF16)
        wsl = (wsm - wsh.astype(F32)).astype(BF16)
        bias = jnp.concatenate([mlstm_b_i[l], mlstm_b_f[l], fox_b_f[l], jnp.zeros((LANES - G_END,), F32)])[None]
        qn = jnp.tile(fox_q_norm[l], F_HEADS)[None]
        kn = jnp.tile(fox_k_norm[l], F_HEADS)[None]
        hnw = mlstm_head_norm[l][None]
        ln1l = ln1[l][None]
        wbm = w_branch_mlstm[l].astype(BF16)
        wbf = w_branch_fox[l].astype(BF16)
        wo = w_out[l].astype(BF16)
        wr = jnp.concatenate([w_router[l], jnp.zeros((D_MODEL, LANES - N_EXPERTS), F32)], axis=1)
        wrh = wr.astype(BF16)
        wrl = (wr - wrh.astype(F32)).astype(BF16)
        br = jnp.concatenate([b_router[l], jnp.zeros((LANES - N_EXPERTS,), F32)])[None]

        zp, kpt32, vpt32, vpt16, gsp = _proj(xp, ln1l, wmain, wsh, wsl, bias, qn, kn, r64, e64, ROW_TILE)
        zp3 = zp.reshape(b, lp, N_ZB * D_MODEL)
        hnp, kap, qap, c_p, n_p, m_p = _mlstm_prompt(zp3, gsp.reshape(b, lp, LANES), hnw, _tril(M_CHUNK))
        hfp = _fox_prompt(zp3, qap, kap, vpt16)
        x1p, tpn, lgp = _merge(xp, hnp.reshape(tp, D_MODEL), zp, hfp.reshape(tp, D_MODEL), wbm, wbf, wo,
                               ln2[l][None], wrh, wrl, br, ROW_TILE)
        unpad = lambda a: jnp.transpose(a.reshape(F_HEADS, F_DH, b, lp)[:, :, :, PAD_FRONT:], (2, 3, 0, 1))
        outs["kp"].append(unpad(kpt32))
        outs["vp"].append(unpad(vpt32))
        outs["lfp"].append(gsp.reshape(b, lp, LANES)[:, PAD_FRONT:, G_FF:G_END])
        outs["cp"].append(c_p)
        outs["np"].append(n_p[:, :, 0, :])
        outs["mp"].append(m_p[:, :, 0, 0])

        zs, kst32, vst32, _, gss = _proj(xs, ln1l, wmain, wsh, wsl, bias, qn, kn, r64, e64, db)
        vs32 = vst32.T
        m_in = jnp.concatenate([state_m[l], jnp.zeros((db, LANES - M_HEADS), F32)], axis=1)
        hns, c_s, n_s, m_s = _mlstm_step(zs, gss, m_in, state_n[l].reshape(db, M_HEADS * M_DK), state_C[l], hnw)
        n_s = n_s.reshape(db, M_HEADS, M_DK)
        q3 = zs[:, ZB_FQ * D_MODEL:(ZB_FQ + 1) * D_MODEL].astype(F32).reshape(db, 1, D_MODEL)
        kn3 = zs[:, ZB_FK * D_MODEL:(ZB_FK + 1) * D_MODEL].astype(F32).reshape(db, 1, D_MODEL)
        hfs = _fox_decode(page_table, q3, kn3, vs32.reshape(db, 1, D_MODEL),
                          jnp.broadcast_to(gss[:, G_FF:G_END, None], (db, F_HEADS, LANES)),
                          jnp.transpose(cache_k[l], (0, 2, 3, 1)), jnp.transpose(cache_v[l], (0, 2, 3, 1)),
                          jnp.transpose(cache_logf[l], (0, 2, 1)))
        x1s, tsn, lgs = _merge(xs, hns, zs, hfs.reshape(db, D_MODEL).astype(BF16), wbm, wbf, wo,
                               ln2[l][None], wrh, wrl, br, db)
        outs["ks"].append(jnp.transpose(kst32.reshape(F_HEADS, F_DH, db), (2, 0, 1)).reshape(db, 1, F_HEADS, F_DH))
        outs["vs"].append(jnp.transpose(vst32.reshape(F_HEADS, F_DH, db), (2, 0, 1)).reshape(db, 1, F_HEADS, F_DH))
        outs["lfs"].append(gss[:, G_FF:G_END].reshape(db, 1, F_HEADS))
        outs["cs"].append(c_s)
        outs["ns"].append(n_s)
        outs["ms"].append(m_s[:, :M_HEADS])

        if l < depth - 1:
            raise NotImplementedError("deeper stacks need the un-normalised residual stream as well")
        lg_all = jnp.concatenate([lgp, lgs], axis=0)
        y_prompt, y_sample = _moe_layer(tpn, tsn, x1p, x1s, lg_all, w_exp_in[l], b_exp_in[l], w_exp_out[l],
                                        b_exp_out[l], ln_final[None], b, lp)

    st = lambda k: jnp.stack(outs[k])
    return (y_prompt, y_sample.reshape(db, 1, D_MODEL), st("kp"), st("vp"), st("lfp"), st("ks"), st("vs"), st("lfs"),
            st("cp"), st("np"), st("mp"), st("cs"), st("ns"), st("ms"))
```

```python
import functools

import numpy as np
import jax
import jax.numpy as jnp
from jax import lax
from jax.experimental import pallas as pl
from jax.experimental.pallas import tpu as pltpu

F32 = jnp.float32
BF16 = jnp.bfloat16
I32 = jnp.int32

D_MODEL = 1024
N_META = 16
M_HEADS, M_DK, M_DV = 4, 128, 256
F_HEADS, F_DH = 16, 64
N_EXPERTS, TOP_K, D_FF = 32, 4, 1024
GATE_CAP = 15.0
SWIGLU_LIMIT = 7.0
SWIGLU_ALPHA = 1.702
EPS = 1e-6
ATTN_SCALE = F_DH ** -0.5
LOG2E = 1.4426950408889634
KEY_HIDE = 1e30

LANES = 128
ATT_BLOCK = 256
ATT_SUB = 128
PAD_FRONT = ATT_BLOCK - N_META
M_CHUNK = 256
MLSTM_SEQS = 1
ROW_TILE = 512
EXP_TILE = 512
COMB_TILE = 128
PAD_TILES = ATT_BLOCK // COMB_TILE
VMEM_LIMIT = 56 * 1024 * 1024

G_IG = 0
G_LF = M_HEADS
G_FF = 2 * M_HEADS
G_END = G_FF + F_HEADS

WB_MQK, WB_MV, WB_MO, WB_FQ, WB_FK, WB_FV, WB_GA, WB_GB = range(8)
N_WB = 8
ZB_MQK, ZB_MV, ZB_MO, ZB_FQ, ZB_FK, ZB_GA, ZB_GB = range(7)
N_ZB = 7


def _dot(a, b):
    return jnp.dot(a, b, preferred_element_type=F32)


def _dot_nt(a, b):
    return lax.dot_general(a, b, (((1,), (1,)), ((), ())), preferred_element_type=F32)


def _dot_tn(a, b):
    return lax.dot_general(a, b, (((0,), (0,)), ((), ())), preferred_element_type=F32)


def _split2(x):
    hi = x.astype(BF16)
    lo = (x - hi.astype(F32)).astype(BF16)
    return hi, lo


def _split3(x):
    a = x.astype(BF16)
    r = x - a.astype(F32)
    b = r.astype(BF16)
    c = (r - b.astype(F32)).astype(BF16)
    return a, b, c


def _dot_exact_rhs01(parts, m01):
    acc = _dot(parts[0], m01)
    for p in parts[1:]:
        acc = acc + _dot(p, m01)
    return acc


def _log_sigmoid(x):
    return jnp.minimum(x, 0.0) - jnp.log1p(jnp.exp(-jnp.abs(x)))


def _cparams(sem):
    return pltpu.CompilerParams(dimension_semantics=sem, vmem_limit_bytes=VMEM_LIMIT)


def _head_norm(y, g, r_ref, e_ref):
    ms = _dot((y * y).astype(BF16), r_ref[...])
    rs = lax.rsqrt(ms + EPS)
    rsx = _dot_exact_rhs01(_split2(rs), e_ref[...])
    return y * rsx * g


def _proj_kernel(x_ref, ln_ref, w_ref, wsh_ref, wsl_ref, bias_ref, qn_ref, kn_ref, r_ref, e_ref,
                 z_ref, kt32_ref, vt32_ref, vt16_ref, gs_ref):
    x = x_ref[...]
    ms = jnp.mean(x * x, axis=-1, keepdims=True)
    h = x * lax.rsqrt(ms + EPS) * ln_ref[...]
    hh, hl = _split2(h)
    wsh = wsh_ref[...]
    g = _dot(hh, wsh) + _dot(hl, wsh) + _dot(hh, wsl_ref[...]) + bias_ref[...]
    lane = lax.broadcasted_iota(I32, g.shape, 1)
    cap = GATE_CAP * jnp.tanh(g / GATE_CAP)
    ls = _log_sigmoid(jnp.where(lane < G_FF, cap, g))
    gs_ref[...] = jnp.where(lane < G_LF, cap, jnp.where(lane < G_END, ls, 0.0))

    def y_of(wb):
        return _dot(hh, w_ref[wb])

    def put(zb, val):
        z_ref[:, zb * D_MODEL:(zb + 1) * D_MODEL] = val.astype(BF16)

    y = y_of(WB_MQK)
    col = lax.broadcasted_iota(I32, y.shape, 1)
    put(ZB_MQK, jnp.where(col < M_HEADS * M_DK, y * (M_DK ** -0.5), y))
    put(ZB_MV, y_of(WB_MV))
    put(ZB_MO, jax.nn.sigmoid(y_of(WB_MO)))
    put(ZB_FQ, _head_norm(y_of(WB_FQ), qn_ref[...], r_ref, e_ref) * (ATTN_SCALE * LOG2E))
    kn = _head_norm(y_of(WB_FK), kn_ref[...], r_ref, e_ref)
    kt32_ref[...] = kn.T
    put(ZB_FK, kn)
    vt = y_of(WB_FV).T
    vt32_ref[...] = vt
    vt16_ref[...] = vt.astype(BF16)
    put(ZB_GA, jax.nn.sigmoid(y_of(WB_GA)))
    put(ZB_GB, jax.nn.sigmoid(y_of(WB_GB)))


def _proj(x, ln, wmain, wsh, wsl, bias, qn, kn, r64, e64, tm):
    t = x.shape[0]
    row = lambda i: (i, 0)
    col = lambda i: (0, i)
    fix = lambda i: (0, 0)
    return pl.pallas_call(
        _proj_kernel,
        grid=(t // tm,),
        in_specs=[
            pl.BlockSpec((tm, D_MODEL), row),
            pl.BlockSpec((1, D_MODEL), fix),
            pl.BlockSpec((N_WB, D_MODEL, D_MODEL), lambda i: (0, 0, 0), pipeline_mode=pl.Buffered(1)),
            pl.BlockSpec((D_MODEL, LANES), fix),
            pl.BlockSpec((D_MODEL, LANES), fix),
            pl.BlockSpec((1, LANES), fix),
            pl.BlockSpec((1, D_MODEL), fix),
            pl.BlockSpec((1, D_MODEL), fix),
            pl.BlockSpec((D_MODEL, LANES), fix),
            pl.BlockSpec((LANES, D_MODEL), fix),
        ],
        out_specs=[
            pl.BlockSpec((tm, N_ZB * D_MODEL), row),
            pl.BlockSpec((D_MODEL, tm), col),
            pl.BlockSpec((D_MODEL, tm), col),
            pl.BlockSpec((D_MODEL, tm), col),
            pl.BlockSpec((tm, LANES), row),
        ],
        out_shape=[
            jax.ShapeDtypeStruct((t, N_ZB * D_MODEL), BF16),
            jax.ShapeDtypeStruct((D_MODEL, t), F32),
            jax.ShapeDtypeStruct((D_MODEL, t), F32),
            jax.ShapeDtypeStruct((D_MODEL, t), BF16),
            jax.ShapeDtypeStruct((t, LANES), F32),
        ],
        compiler_params=_cparams(("parallel",)),
        name="proj",
    )(x, ln, wmain, wsh, wsl, bias, qn, kn, r64, e64)


def _mlstm_kernel(qk_ref, v_ref, gs_ref, hnw_ref, tril_ref, place_ref,
                  hn_ref, ka_ref, qa_ref, c_out, n_out, m_out,
                  c_scr, n_scr, m_scr, crow_scr):
    c = pl.program_id(1)
    lc = M_CHUNK

    @pl.when(c == 0)
    def _():
        c_scr[...] = jnp.zeros_like(c_scr)
        n_scr[...] = jnp.zeros_like(n_scr)
        m_scr[...] = jnp.zeros_like(m_scr)
        crow_scr[...] = jnp.zeros_like(crow_scr)

    lane = lax.broadcasted_iota(I32, (lc, LANES), 1)
    pos = c * lc + lax.broadcasted_iota(I32, (lc, LANES), 0)
    valid = pos >= PAD_FRONT
    is_ig = lane < G_LF
    row_t = lax.broadcasted_iota(I32, (lc, lc), 0)
    col_s = lax.broadcasted_iota(I32, (lc, lc), 1)
    causal = col_s <= row_t

    def one_sequence(sq):
        g = gs_ref[sq]
        gsum = jnp.where(valid & jnp.logical_not(is_ig), g, 0.0)
        bcum = _dot_exact_rhs01_lhs(tril_ref[...], _split3(gsum))
        comb = jnp.where(is_ig, jnp.where(valid, g, -jnp.inf), bcum)
        comb_t = comb.T
        qk = qk_ref[sq]
        vv = v_ref[sq]
        for h in range(M_HEADS):
            q = qk[:, h * M_DK:(h + 1) * M_DK]
            k = qk[:, (M_HEADS + h) * M_DK:(M_HEADS + h + 1) * M_DK]
            v = vv[:, h * M_DV:(h + 1) * M_DV]
            m_prev = m_scr[sq, h, 0:1, 0:1]
            b_col = bcum[:, G_LF + h:G_LF + h + 1]
            ig_col = comb[:, G_IG + h:G_IG + h + 1]
            b_row = comb_t[G_LF + h:G_LF + h + 1, :]
            ig_row = comb_t[G_IG + h:G_IG + h + 1, :]

            log_d = jnp.where(causal, b_col - b_row + ig_row, -jnp.inf)
            inter = b_col + m_prev
            m_row = jnp.maximum(inter, jnp.max(log_d, axis=1, keepdims=True))
            s = _dot_nt(q, k) * jnp.exp(log_d - m_row)
            w_int = jnp.exp(inter - m_row)
            cmat = c_scr[sq, h]
            nvec = n_scr[sq, h, 0:1, :]
            num = _dot(s.astype(BF16), v) + w_int * _dot_nt(q, cmat.astype(BF16))
            den = jnp.sum(s, axis=1, keepdims=True) + w_int * jnp.sum(q.astype(F32) * nvec, axis=1, keepdims=True)
            hh = num / jnp.maximum(jnp.abs(den), jnp.exp(-m_row))
            hn = (hh * lax.rsqrt(jnp.mean(hh * hh, axis=1, keepdims=True) + EPS)
                  * hnw_ref[:, h * M_DV:(h + 1) * M_DV])
            hn_ref[sq, :, h * M_DV:(h + 1) * M_DV] = hn.astype(BF16)

            b_last = b_col[lc - 1:lc, :]
            m_new = jnp.maximum(b_last + m_prev, jnp.max(b_last - b_row + ig_row, axis=1, keepdims=True))
            decay = jnp.exp(b_last + m_prev - m_new)
            ws_col = jnp.exp(b_last - b_col + ig_col - m_new)
            kw = k.astype(F32) * ws_col
            c_scr[sq, h] = decay * cmat + _dot_tn(v, kw.astype(BF16))
            n_scr[sq, h] = jnp.broadcast_to(decay * nvec + jnp.sum(kw, axis=0, keepdims=True), n_scr.shape[2:])
            m_scr[sq, h] = jnp.broadcast_to(m_new, m_scr.shape[2:])

        cg = bcum + crow_scr[sq, 0:1, :]
        crow_scr[sq] = jnp.broadcast_to(cg[lc - 1:lc, :], crow_scr.shape[1:])
        p0, p1, p2 = _split3(cg * LOG2E)
        p0 = jnp.where(lane == 0, jnp.ones_like(p0), p0)
        hide = jnp.logical_not(valid) & (lane >= G_FF) & (lane < G_END)
        p0k = jnp.where(hide, jnp.full_like(p0, KEY_HIDE), p0)
        ka_ref[sq] = (_dot(p0k, place_ref[0]) + _dot(p1, place_ref[1]) + _dot(p2, place_ref[2])).astype(BF16)
        qa_ref[sq] = (_dot(p0, place_ref[3]) + _dot(p1, place_ref[4]) + _dot(p2, place_ref[5])).astype(BF16)

    for sq in range(MLSTM_SEQS):
        one_sequence(sq)

    @pl.when(c == pl.num_programs(1) - 1)
    def _():
        c_out[...] = c_scr[...]
        n_out[...] = n_scr[...]
        m_out[...] = m_scr[...]


def _dot_exact_rhs01_lhs(m01, parts):
    acc = _dot(m01, parts[0])
    for p in parts[1:]:
        acc = acc + _dot(m01, p)
    return acc


BIAS_LANES = 6


def _bias_placement():
    pm = np.zeros((6, LANES, LANES), np.float32)
    for h in range(F_HEADS):
        base = BIAS_LANES * h
        for t in range(3):
            pm[t, G_FF + h, base + t] = -1.0
            pm[0, 0, base + 3 + t] = 1.0
            pm[3 + t, G_FF + h, base + 3 + t] = 1.0
            pm[3, 0, base + t] = 1.0
    return jnp.asarray(pm, BF16)


def _mlstm_prompt(z3, gs3, hnw, tril):
    b, lp, _ = z3.shape
    lc = M_CHUNK
    nc = lp // lc
    sq = MLSTM_SEQS
    assert b % sq == 0
    return pl.pallas_call(
        _mlstm_kernel,
        grid=(b // sq, nc),
        in_specs=[
            pl.BlockSpec((sq, lc, D_MODEL), lambda i, c: (i, c, ZB_MQK)),
            pl.BlockSpec((sq, lc, D_MODEL), lambda i, c: (i, c, ZB_MV)),
            pl.BlockSpec((sq, lc, LANES), lambda i, c: (i, c, 0)),
            pl.BlockSpec((1, D_MODEL), lambda i, c: (0, 0)),
            pl.BlockSpec((lc, lc), lambda i, c: (0, 0)),
            pl.BlockSpec((6, LANES, LANES), lambda i, c: (0, 0, 0)),
        ],
        out_specs=[
            pl.BlockSpec((sq, lc, D_MODEL), lambda i, c: (i, c, 0)),
            pl.BlockSpec((sq, lc, LANES), lambda i, c: (i, c, 0)),
            pl.BlockSpec((sq, lc, LANES), lambda i, c: (i, c, 0)),
            pl.BlockSpec((sq, M_HEADS, M_DV, M_DK), lambda i, c: (i, 0, 0, 0)),
            pl.BlockSpec((sq, M_HEADS, 8, M_DK), lambda i, c: (i, 0, 0, 0)),
            pl.BlockSpec((sq, M_HEADS, 8, LANES), lambda i, c: (i, 0, 0, 0)),
        ],
        out_shape=[
            jax.ShapeDtypeStruct((b, lp, D_MODEL), BF16),
            jax.ShapeDtypeStruct((b, lp, LANES), BF16),
            jax.ShapeDtypeStruct((b, lp, LANES), BF16),
            jax.ShapeDtypeStruct((b, M_HEADS, M_DV, M_DK), F32),
            jax.ShapeDtypeStruct((b, M_HEADS, 8, M_DK), F32),
            jax.ShapeDtypeStruct((b, M_HEADS, 8, LANES), F32),
        ],
        scratch_shapes=[
            pltpu.VMEM((sq, M_HEADS, M_DV, M_DK), F32),
            pltpu.VMEM((sq, M_HEADS, 8, M_DK), F32),
            pltpu.VMEM((sq, M_HEADS, 8, LANES), F32),
            pltpu.VMEM((sq, 8, LANES), F32),
        ],
        compiler_params=_cparams(("parallel", "arbitrary")),
        name="mlstm",
    )(z3, z3, gs3, hnw, tril, _bias_placement())


MSTEP_SEQS = 8


def _mstep_kernel(qk_ref, v_ref, gs_ref, m_ref, n_ref, c_ref, hnw_ref,
                  hn_ref, c_out, n_out, m_out):
    sb = MSTEP_SEQS
    qk = qk_ref[...]
    vv = v_ref[...]
    g = gs_ref[...]
    row = lax.broadcasted_iota(I32, (sb, 1), 0)
    m_new_all = jnp.zeros((sb, LANES), F32)
    lane = lax.broadcasted_iota(I32, (sb, LANES), 1)
    for h in range(M_HEADS):
        q = qk[:, h * M_DK:(h + 1) * M_DK]
        k = qk[:, (M_HEADS + h) * M_DK:(M_HEADS + h + 1) * M_DK]
        v = vv[:, h * M_DV:(h + 1) * M_DV]
        ig = g[:, G_IG + h:G_IG + h + 1]
        lf = g[:, G_LF + h:G_LF + h + 1]
        m_prev = m_ref[:, h:h + 1]
        inter = lf + m_prev
        m_row = jnp.maximum(inter, ig)
        d = jnp.exp(ig - m_row)
        w_int = jnp.exp(inter - m_row)
        qf = q.astype(F32)
        kf = k.astype(F32)
        s = jnp.sum(qf * kf, axis=1, keepdims=True) * d
        nvec = n_ref[:, h * M_DK:(h + 1) * M_DK]
        den = s + w_int * jnp.sum(qf * nvec, axis=1, keepdims=True)
        kw = (kf * d).astype(BF16)
        hrows = []
        for i in range(sb):
            cmat = c_ref[i, h]
            sel = row == i
            qi = jnp.where(sel, q, jnp.zeros_like(q))
            qc = _dot_nt(qi, cmat.astype(BF16))
            hrows.append(jnp.where(sel, qc, 0.0))
            vi = jnp.where(sel, v, jnp.zeros_like(v))
            c_out[i, h] = w_int[i:i + 1, :] * cmat + _dot_tn(vi, kw)
        qc_all = functools.reduce(lambda a, b: a + b, hrows)
        num = s * v.astype(F32) + w_int * qc_all
        hh = num / jnp.maximum(jnp.abs(den), jnp.exp(-m_row))
        hn = hh * lax.rsqrt(jnp.mean(hh * hh, axis=1, keepdims=True) + EPS) * hnw_ref[:, h * M_DV:(h + 1) * M_DV]
        hn_ref[:, h * M_DV:(h + 1) * M_DV] = hn.astype(BF16)
        n_out[:, h * M_DK:(h + 1) * M_DK] = w_int * nvec + kf * d
        m_new_all = jnp.where(lane == h, m_row, m_new_all)
    m_out[...] = m_new_all


def _mlstm_step(zs, gss, m_in, n_in, c_in, hnw):
    db = zs.shape[0]
    sb = MSTEP_SEQS
    return pl.pallas_call(
        _mstep_kernel,
        grid=(db // sb,),
        in_specs=[
            pl.BlockSpec((sb, D_MODEL), lambda i: (i, ZB_MQK)),
            pl.BlockSpec((sb, D_MODEL), lambda i: (i, ZB_MV)),
            pl.BlockSpec((sb, LANES), lambda i: (i, 0)),
            pl.BlockSpec((sb, LANES), lambda i: (i, 0)),
            pl.BlockSpec((sb, M_HEADS * M_DK), lambda i: (i, 0)),
            pl.BlockSpec((sb, M_HEADS, M_DV, M_DK), lambda i: (i, 0, 0, 0)),
            pl.BlockSpec((1, D_MODEL), lambda i: (0, 0)),
        ],
        out_specs=[
            pl.BlockSpec((sb, D_MODEL), lambda i: (i, 0)),
            pl.BlockSpec((sb, M_HEADS, M_DV, M_DK), lambda i: (i, 0, 0, 0)),
            pl.BlockSpec((sb, M_HEADS * M_DK), lambda i: (i, 0)),
            pl.BlockSpec((sb, LANES), lambda i: (i, 0)),
        ],
        out_shape=[
            jax.ShapeDtypeStruct((db, D_MODEL), BF16),
            jax.ShapeDtypeStruct((db, M_HEADS, M_DV, M_DK), F32),
            jax.ShapeDtypeStruct((db, M_HEADS * M_DK), F32),
            jax.ShapeDtypeStruct((db, LANES), F32),
        ],
        compiler_params=_cparams(("parallel",)),
        name="mstep",
    )(zs, zs, gss, m_in, n_in, c_in, hnw)


def _fox_kernel(qi_tab, ki_tab, q_ref, qa_ref, k_ref, ka_ref, vt_ref, o_ref,
                qb_scr, acc_scr, m_scr, l_scr, a_scr, st_scr, pt_scr):
    step = pl.program_id(1)
    qi = qi_tab[step]
    ki = ki_tab[step]
    tq = tk = ATT_BLOCK

    @pl.when(ki == 0)
    def _():
        acc_scr[...] = jnp.zeros_like(acc_scr)
        m_scr[...] = jnp.full_like(m_scr, -jnp.inf)
        l_scr[...] = jnp.zeros_like(l_scr)
        lane = lax.broadcasted_iota(I32, (tq, LANES), 1)
        qa = qa_ref[0]
        for h in range(F_HEADS):
            q2 = q_ref[0, :, (h // 2) * LANES:(h // 2 + 1) * LANES]
            own = (lane >= (h % 2) * F_DH) & (lane < (h % 2 + 1) * F_DH)
            bias = (lane >= h * BIAS_LANES) & (lane < (h + 1) * BIAS_LANES)
            qb_scr[h] = jnp.concatenate([jnp.where(own, q2, jnp.zeros_like(q2)),
                                         jnp.where(bias, qa, jnp.zeros_like(qa))], axis=1)

    def sweep(masked):
        ka = ka_ref[0]
        if masked:
            key = ki * tk + lax.broadcasted_iota(I32, (tk, tq), 0)
            qry = qi * tq + lax.broadcasted_iota(I32, (tk, tq), 1)
            visible = key <= qry
        def score_stage(h):
            kb = jnp.concatenate([k_ref[0, :, (h // 2) * LANES:(h // 2 + 1) * LANES], ka], axis=1)
            st_scr[h] = _dot_nt(kb, qb_scr[h])

        def softmax_stage(h):
            for c0 in range(0, tq, LANES):
                cs = slice(c0, c0 + LANES)
                def scores(r0):
                    st = st_scr[h, r0:r0 + ATT_SUB, cs]
                    return jnp.where(visible[r0:r0 + ATT_SUB, cs], st, -jnp.inf) if masked else st

                m_prev = m_scr[h, 0:1, cs]
                m_next = m_prev
                for r0 in range(0, tk, ATT_SUB):
                    m_next = jnp.maximum(m_next, jnp.max(scores(r0), axis=0, keepdims=True))
                alpha = jnp.exp2(m_prev - m_next)
                l_new = alpha * l_scr[h, 0:1, cs]
                for r0 in range(0, tk, ATT_SUB):
                    pt = jnp.exp2(scores(r0) - m_next)
                    l_new = l_new + jnp.sum(pt, axis=0, keepdims=True)
                    pt_scr[h, r0:r0 + ATT_SUB, cs] = pt.astype(BF16)
                l_scr[h, :, cs] = jnp.broadcast_to(l_new, (8, LANES))
                m_scr[h, :, cs] = jnp.broadcast_to(m_next, (8, LANES))
                a_scr[h, :, cs] = jnp.broadcast_to(alpha, (8, LANES))
        def value_stage(h):
            acc_scr[h] = a_scr[h, 0:1, :] * acc_scr[h] + _dot(vt_ref[h * F_DH:(h + 1) * F_DH, :], pt_scr[h])

        for stage in (score_stage, softmax_stage, value_stage):
            for h in range(F_HEADS):
                stage(h)

    edge = ki == qi

    @pl.when(edge)
    def _():
        sweep(True)

    @pl.when(jnp.logical_not(edge))
    def _():
        sweep(False)

    @pl.when(ki == qi)
    def _():
        for p in range(F_HEADS // 2):
            o2 = jnp.concatenate([acc_scr[2 * p] * (1.0 / l_scr[2 * p, 0:1, :]),
                                  acc_scr[2 * p + 1] * (1.0 / l_scr[2 * p + 1, 0:1, :])], axis=0)
            o_ref[0, :, p * LANES:(p + 1) * LANES] = o2.T.astype(BF16)


def _fox_prompt(z3, qa, ka, vt16):
    b, lp, _ = z3.shape
    blk = ATT_BLOCK
    nb = lp // blk
    qi_tab = np.concatenate([np.full((i + 1,), i, np.int32) for i in range(nb)])
    ki_tab = np.concatenate([np.arange(i + 1, dtype=np.int32) for i in range(nb)])
    grid_spec = pltpu.PrefetchScalarGridSpec(
        num_scalar_prefetch=2,
        grid=(b, len(qi_tab)),
        in_specs=[
            pl.BlockSpec((1, blk, D_MODEL), lambda i, s, qt, kt: (i, qt[s], ZB_FQ)),
            pl.BlockSpec((1, blk, LANES), lambda i, s, qt, kt: (i, qt[s], 0)),
            pl.BlockSpec((1, blk, D_MODEL), lambda i, s, qt, kt: (i, kt[s], ZB_FK)),
            pl.BlockSpec((1, blk, LANES), lambda i, s, qt, kt: (i, kt[s], 0)),
            pl.BlockSpec((D_MODEL, blk), lambda i, s, qt, kt: (0, i * nb + kt[s])),
        ],
        out_specs=pl.BlockSpec((1, blk, D_MODEL), lambda i, s, qt, kt: (i, qt[s], 0)),
        scratch_shapes=[
            pltpu.VMEM((F_HEADS, blk, 2 * LANES), BF16),
            pltpu.VMEM((F_HEADS, F_DH, blk), F32),
            pltpu.VMEM((F_HEADS, 8, blk), F32),
            pltpu.VMEM((F_HEADS, 8, blk), F32),
            pltpu.VMEM((F_HEADS, 8, blk), F32),
            pltpu.VMEM((F_HEADS, blk, blk), F32),
            pltpu.VMEM((F_HEADS, blk, blk), BF16),
        ],
    )
    return pl.pallas_call(
        _fox_kernel,
        grid_spec=grid_spec,
        out_shape=jax.ShapeDtypeStruct((b, lp, D_MODEL), BF16),
        compiler_params=_cparams(("parallel", "arbitrary")),
        name="fox",
    )(jnp.asarray(qi_tab), jnp.asarray(ki_tab), z3, qa, z3, ka, vt16)


DEC_PAGES = 16


def _lane_bcast_cols(row):
    full = jnp.broadcast_to(row, (LANES, D_MODEL)).T
    return full.reshape(F_HEADS, F_DH, LANES)


def _per_head(x):
    return x.reshape(F_HEADS, 1, LANES)


def _dec_kernel(pt_ref, q_ref, kn_ref, vn_ref, lfn_ref, triu_ref, ones_ref, *rest):
    np_ = DEC_PAGES
    k_refs, v_refs, lf_refs = rest[:np_], rest[np_:2 * np_], rest[2 * np_:3 * np_]
    o_ref, qb_scr, acc_scr, m_scr, l_scr, c_scr = rest[3 * np_:]
    p = pl.program_id(1)

    @pl.when(p == 0)
    def _():
        qb_scr[...] = _lane_bcast_cols(q_ref[0])
        acc_scr[...] = jnp.zeros_like(acc_scr)
        m_scr[...] = jnp.full_like(m_scr, -jnp.inf)
        l_scr[...] = jnp.zeros_like(l_scr)
        c_scr[...] = jnp.zeros_like(c_scr)

    qb = qb_scr[...]
    carry = c_scr[...]
    us = []
    for i in range(np_):
        cum = _dot_exact_rhs01(_split3(lf_refs[i][0]), triu_ref[...]) + carry
        carry = jnp.broadcast_to(cum[:, LANES - 1:LANES], carry.shape)
        s = jnp.sum(qb * k_refs[i][0], axis=1)
        us.append(s - cum * LOG2E)
    c_scr[...] = carry
    m_prev = m_scr[...]
    m_cur = functools.reduce(jnp.maximum, [jnp.max(u, axis=1, keepdims=True) for u in us])
    m_next = jnp.maximum(m_prev, m_cur)
    alpha = jnp.exp2(m_prev - m_next)
    acc = _per_head(alpha) * acc_scr[...]
    l_new = alpha * l_scr[...]
    for i in range(np_):
        pr = jnp.exp2(us[i] - m_next)
        l_new = l_new + jnp.sum(pr, axis=1, keepdims=True)
        acc = acc + _per_head(pr) * v_refs[i][0]
    acc_scr[...] = acc
    l_scr[...] = l_new
    m_scr[...] = m_next

    @pl.when(p == pl.num_programs(1) - 1)
    def _():
        s_new = jnp.sum(qb * _lane_bcast_cols(kn_ref[0]), axis=1)
        u_new = s_new - (carry + lfn_ref[0]) * LOG2E
        m_fin = jnp.maximum(m_next, u_new)
        a_fin = jnp.exp2(m_next - m_fin)
        pn = jnp.exp2(u_new - m_fin)
        l_fin = a_fin * l_new + pn
        tot = (_per_head(a_fin) * acc + _per_head(pn * (1.0 / LANES)) * _lane_bcast_cols(vn_ref[0])) / _per_head(l_fin)
        out8 = _dot_nt_exact_lhs01(ones_ref[...], _split3(tot.reshape(D_MODEL, LANES)))
        o_ref[0] = out8[0:1, :]


def _dot_nt_exact_lhs01(m01, parts):
    acc = _dot_nt(m01, parts[0])
    for part in parts[1:]:
        acc = acc + _dot_nt(m01, part)
    return acc


def _fox_decode(page_table, q3, kn3, vn3, lfn3, cache_kt, cache_vt, cache_lft):
    db, npg = page_table.shape
    pg = cache_kt.shape[3]
    assert pg == LANES and npg % DEC_PAGES == 0
    row3 = lambda s, p, pt: (s, 0, 0)
    fix = lambda s, p, pt: (0, 0)

    def page(i, nd):
        return lambda s, p, pt: (pt[s, p * DEC_PAGES + i],) + (0,) * nd

    in_specs = [
        pl.BlockSpec((1, 1, D_MODEL), row3),
        pl.BlockSpec((1, 1, D_MODEL), row3),
        pl.BlockSpec((1, 1, D_MODEL), row3),
        pl.BlockSpec((1, F_HEADS, LANES), row3),
        pl.BlockSpec((LANES, LANES), fix),
        pl.BlockSpec((8, LANES), fix),
    ]
    in_specs += [pl.BlockSpec((1, F_HEADS, F_DH, LANES), page(i, 3)) for i in range(DEC_PAGES)]
    in_specs += [pl.BlockSpec((1, F_HEADS, F_DH, LANES), page(i, 3)) for i in range(DEC_PAGES)]
    in_specs += [pl.BlockSpec((1, F_HEADS, LANES), page(i, 2)) for i in range(DEC_PAGES)]
    grid_spec = pltpu.PrefetchScalarGridSpec(
        num_scalar_prefetch=1,
        grid=(db, npg // DEC_PAGES),
        in_specs=in_specs,
        out_specs=pl.BlockSpec((1, 1, D_MODEL), row3),
        scratch_shapes=[
            pltpu.VMEM((F_HEADS, F_DH, LANES), F32),
            pltpu.VMEM((F_HEADS, F_DH, LANES), F32),
            pltpu.VMEM((F_HEADS, LANES), F32),
            pltpu.VMEM((F_HEADS, LANES), F32),
            pltpu.VMEM((F_HEADS, LANES), F32),
        ],
    )
    triu = jnp.asarray(np.triu(np.ones((LANES, LANES), np.float32)), BF16)
    ones = jnp.ones((8, LANES), BF16)
    return pl.pallas_call(
        _dec_kernel,
        grid_spec=grid_spec,
        out_shape=jax.ShapeDtypeStruct((db, 1, D_MODEL), F32),
        compiler_params=_cparams(("parallel", "arbitrary")),
        name="dec",
    )(page_table, q3, kn3, vn3, lfn3, triu, ones,
      *([cache_kt] * DEC_PAGES), *([cache_vt] * DEC_PAGES), *([cache_lft] * DEC_PAGES))


TOKEN_SUBLANES = D_MODEL // LANES


def _store_token_tiles(ref, x):
    n = x.shape[0]
    for g in range(TOKEN_SUBLANES):
        ref[pl.ds(g, n, stride=TOKEN_SUBLANES), :] = x[:, g * LANES:(g + 1) * LANES]


def _load_token_tiles(ref, n):
    return jnp.concatenate([ref[pl.ds(g, n, stride=TOKEN_SUBLANES), :] for g in range(TOKEN_SUBLANES)], axis=1)


def _merge_kernel(x_ref, hn_ref, so_ref, hf_ref, ga_ref, gb_ref, wbm_ref, wbf_ref, wo_ref, ln2_ref,
                  wrh_ref, wrl_ref, br_ref, x1_ref, t_ref, lg_ref):
    hm = hn_ref[...] * so_ref[...]
    ya = _dot(hm, wbm_ref[...])
    yb = _dot(hf_ref[...], wbf_ref[...])
    u = ga_ref[...].astype(F32) * ya + gb_ref[...].astype(F32) * yb
    x1 = x_ref[...] + _dot(u.astype(BF16), wo_ref[...])
    x1_ref[...] = x1
    t = x1 * lax.rsqrt(jnp.mean(x1 * x1, axis=-1, keepdims=True) + EPS) * ln2_ref[...]
    _store_token_tiles(t_ref, t)
    th, tl = _split2(t)
    wrh = wrh_ref[...]
    lg_ref[...] = _dot(th, wrh) + _dot(tl, wrh) + _dot(th, wrl_ref[...]) + br_ref[...]


def _merge(x, hn, z, hf, wbm, wbf, wo, ln2, wrh, wrl, br, tm):
    t = x.shape[0]
    row = lambda i: (i, 0)
    fix = lambda i: (0, 0)
    full = pl.BlockSpec((D_MODEL, D_MODEL), fix)
    return pl.pallas_call(
        _merge_kernel,
        grid=(t // tm,),
        in_specs=[
            pl.BlockSpec((tm, D_MODEL), row),
            pl.BlockSpec((tm, D_MODEL), row),
            pl.BlockSpec((tm, D_MODEL), lambda i: (i, ZB_MO)),
            pl.BlockSpec((tm, D_MODEL), row),
            pl.BlockSpec((tm, D_MODEL), lambda i: (i, ZB_GA)),
            pl.BlockSpec((tm, D_MODEL), lambda i: (i, ZB_GB)),
            full, full, full,
            pl.BlockSpec((1, D_MODEL), fix),
            pl.BlockSpec((D_MODEL, LANES), fix),
            pl.BlockSpec((D_MODEL, LANES), fix),
            pl.BlockSpec((1, LANES), fix),
        ],
        out_specs=[
            pl.BlockSpec((tm, D_MODEL), row),
            pl.BlockSpec((tm * TOKEN_SUBLANES, LANES), row),
            pl.BlockSpec((tm, LANES), row),
        ],
        out_shape=[
            jax.ShapeDtypeStruct((t, D_MODEL), F32),
            jax.ShapeDtypeStruct((t * TOKEN_SUBLANES, LANES), F32),
            jax.ShapeDtypeStruct((t, LANES), F32),
        ],
        compiler_params=_cparams(("parallel",)),
        name="merge",
    )(x, hn, z, hf, z, z, wbm, wbf, wo, ln2, wrh, wrl, br)


def _route_kernel(n_valid, lg_ref, tril_ref, ids_ref, wts_ref, rank_ref, cnt_ref, carry_scr):
    i = pl.program_id(0)

    @pl.when(i == 0)
    def _():
        carry_scr[...] = jnp.zeros_like(carry_scr)

    tm = lg_ref.shape[0]
    lane = lax.broadcasted_iota(I32, (tm, LANES), 1)
    lanef = lane.astype(F32)
    lg = jnp.where(lane < N_EXPERTS, lg_ref[...], -jnp.inf)
    vals, idxs, hots = [], [], []
    for _ in range(TOP_K):
        mx = jnp.max(lg, axis=1, keepdims=True)
        idx = jnp.min(jnp.where(lg == mx, lanef, float(LANES)), axis=1, keepdims=True)
        hot = lanef == idx
        lg = jnp.where(hot, -jnp.inf, lg)
        vals.append(mx)
        idxs.append(idx)
        hots.append(hot)
    es = [jnp.exp(v - vals[0]) for v in vals]
    tot = functools.reduce(lambda a, b: a + b, es)
    sel = functools.reduce(lambda a, b: a | b, hots)
    real = i * tm + lax.broadcasted_iota(I32, (tm, LANES), 0) < n_valid
    a01 = jnp.where(sel & real, 1.0, 0.0)
    before = _dot(tril_ref[...], a01.astype(BF16)) + carry_scr[0:1, :]
    carry_scr[...] = jnp.broadcast_to(carry_scr[0:1, :] + jnp.sum(a01, axis=0, keepdims=True), carry_scr.shape)
    ids = jnp.zeros((tm, LANES), I32)
    wts = jnp.zeros((tm, LANES), F32)
    rank = jnp.zeros((tm, LANES), I32)
    for kk in range(TOP_K):
        r = jnp.sum(jnp.where(hots[kk], before, 0.0), axis=1, keepdims=True)
        ids = jnp.where(lane == kk, idxs[kk].astype(I32), ids)
        wts = jnp.where(lane == kk, es[kk] / tot, wts)
        rank = jnp.where(lane == kk, r.astype(I32), rank)
    ids_ref[...] = ids
    wts_ref[...] = wts
    rank_ref[...] = rank
    cnt_ref[...] = carry_scr[...].astype(I32)


def _route(logits, tril_strict, tm, n_valid):
    t = logits.shape[0]
    row = lambda i: (i, 0)
    return pl.pallas_call(
        functools.partial(_route_kernel, n_valid),
        grid=(t // tm,),
        in_specs=[pl.BlockSpec((tm, LANES), row), pl.BlockSpec((tm, tm), lambda i: (0, 0))],
        out_specs=[pl.BlockSpec((tm, LANES), row)] * 3 + [pl.BlockSpec((8, LANES), lambda i: (0, 0))],
        out_shape=[
            jax.ShapeDtypeStruct((t, LANES), I32),
            jax.ShapeDtypeStruct((t, LANES), F32),
            jax.ShapeDtypeStruct((t, LANES), I32),
            jax.ShapeDtypeStruct((8, LANES), I32),
        ],
        scratch_shapes=[pltpu.VMEM((8, LANES), F32)],
        compiler_params=_cparams(("arbitrary",)),
        name="route",
    )(logits, tril_strict)


def _tile_rows(i):
    return pl.ds(pl.multiple_of(i * TOKEN_SUBLANES, TOKEN_SUBLANES), TOKEN_SUBLANES)


def _dispatch_kernel(ntp, pad_lo, pad_hi, n_pad, pos_ref, tp_ref, ts_ref, xs_ref, zero_scr, sem, zsem):
    tc = COMB_TILE
    i = pl.program_id(0)

    def scatter_from(t_ref):
        def issue(r, carry):
            src = t_ref.at[_tile_rows(r), :]
            for kk in range(TOP_K):
                pltpu.make_async_copy(src, xs_ref.at[_tile_rows(pos_ref[0, kk, r]), :], sem).start(priority=kk % 2)
            return carry

        lax.fori_loop(0, tc, issue, 0, unroll=2)

    @pl.when(i < ntp)
    def _():
        scatter_from(tp_ref)

    @pl.when(i >= ntp)
    def _():
        scatter_from(ts_ref)

    @pl.when(i == 0)
    def _():
        zero_scr[...] = jnp.zeros_like(zero_scr)

        def per_expert(e, carry):
            def per_slot(s, c2):
                pltpu.make_async_copy(zero_scr, xs_ref.at[_tile_rows(s), :], zsem).start()
                return c2
            return lax.fori_loop(pad_lo[e], pad_hi[e], per_slot, carry)

        lax.fori_loop(0, pad_lo.shape[0], per_expert, 0)

        def drain(s, carry):
            pltpu.make_async_copy(zero_scr, xs_ref.at[pl.ds(0, TOKEN_SUBLANES), :], zsem).wait()
            return carry

        lax.fori_loop(0, n_pad[0], drain, 0)

    for kk in range(TOP_K):
        pltpu.make_async_copy(tp_ref, xs_ref.at[pl.ds(0, tc * TOKEN_SUBLANES), :], sem).wait()


def _dispatch(pad_lo, pad_hi, n_pad, pos3, t_p, t_s, r_pad):
    tc = COMB_TILE
    ntp = t_p.shape[0] // (tc * TOKEN_SUBLANES)
    assert t_s.shape[0] == tc * TOKEN_SUBLANES and pos3.shape[0] == ntp + 1
    grid_spec = pltpu.PrefetchScalarGridSpec(
        num_scalar_prefetch=3,
        grid=(ntp + 1,),
        in_specs=[
            pl.BlockSpec((1, TOP_K, tc), lambda i, lo, hi, n: (i, 0, 0), memory_space=pltpu.SMEM),
            pl.BlockSpec((tc * TOKEN_SUBLANES, LANES), lambda i, lo, hi, n: (jnp.minimum(i, ntp - 1), 0)),
            pl.BlockSpec((tc * TOKEN_SUBLANES, LANES), lambda i, lo, hi, n: (0, 0)),
        ],
        out_specs=pl.BlockSpec(memory_space=pl.ANY),
        scratch_shapes=[pltpu.VMEM((TOKEN_SUBLANES, LANES), F32), pltpu.SemaphoreType.DMA, pltpu.SemaphoreType.DMA],
    )
    return pl.pallas_call(
        functools.partial(_dispatch_kernel, ntp),
        grid_spec=grid_spec,
        out_shape=jax.ShapeDtypeStruct((r_pad * TOKEN_SUBLANES, LANES), F32),
        compiler_params=_cparams(("arbitrary",)),
        name="dispatch",
    )(pad_lo, pad_hi, n_pad, pos3, t_p, t_s)


def _experts_kernel(te_ref, nu_ref, xs_ref, w1_ref, b1_ref, w2_ref, b2_ref, ys_ref, w1b_scr, w2b_scr):
    j = pl.program_id(0)
    prev = te_ref[jnp.maximum(j - 1, 0)]

    @pl.when((j == 0) | (te_ref[j] != prev))
    def _():
        w1b_scr[...] = w1_ref[0].astype(BF16)
        w2b_scr[...] = w2_ref[0].astype(BF16)

    @pl.when(j < nu_ref[0])
    def _():
        x = _load_token_tiles(xs_ref, EXP_TILE)
        hcat = _dot(x.astype(BF16), w1b_scr[...]) + b1_ref[0]
        g = jnp.minimum(hcat[:, :D_FF], SWIGLU_LIMIT)
        u = jnp.clip(hcat[:, D_FF:], -SWIGLU_LIMIT, SWIGLU_LIMIT)
        a = g * jax.nn.sigmoid(SWIGLU_ALPHA * g) * (u + 1.0)
        _store_token_tiles(ys_ref, _dot(a.astype(BF16), w2b_scr[...]) + b2_ref[0])

    @pl.when(j >= nu_ref[0])
    def _():
        ys_ref[...] = jnp.zeros_like(ys_ref)


def _experts(tile_expert, n_used, xs, w1, b1, w2, b2):
    r = xs.shape[0] // TOKEN_SUBLANES
    tm = EXP_TILE
    grid_spec = pltpu.PrefetchScalarGridSpec(
        num_scalar_prefetch=2,
        grid=(r // tm,),
        in_specs=[
            pl.BlockSpec((tm * TOKEN_SUBLANES, LANES), lambda j, te, nu: (jnp.minimum(j, nu[0] - 1), 0)),
            pl.BlockSpec((1, D_MODEL, 2 * D_FF), lambda j, te, nu: (te[j], 0, 0)),
            pl.BlockSpec((1, 1, 2 * D_FF), lambda j, te, nu: (te[j], 0, 0)),
            pl.BlockSpec((1, D_FF, D_MODEL), lambda j, te, nu: (te[j], 0, 0)),
            pl.BlockSpec((1, 1, D_MODEL), lambda j, te, nu: (te[j], 0, 0)),
        ],
        out_specs=pl.BlockSpec((tm * TOKEN_SUBLANES, LANES), lambda j, te, nu: (j, 0)),
        scratch_shapes=[
            pltpu.VMEM((D_MODEL, 2 * D_FF), BF16),
            pltpu.VMEM((D_FF, D_MODEL), BF16),
        ],
    )
    return pl.pallas_call(
        _experts_kernel,
        grid_spec=grid_spec,
        out_shape=jax.ShapeDtypeStruct((r * TOKEN_SUBLANES, LANES), F32),
        compiler_params=_cparams(("arbitrary",)),
        name="experts",
    )(tile_expert, n_used, xs, w1, b1.reshape(N_EXPERTS, 1, 2 * D_FF), w2, b2.reshape(N_EXPERTS, 1, D_MODEL))


def _combine_kernel(ntp, tps, pos_ref, ys_ref, x1p_ref, x1s_ref, wts_ref, lnf_ref, yp_ref, ysm_ref, rows_scr, sem):
    tc = COMB_TILE
    i = pl.program_id(0)
    is_prompt = i < ntp
    wanted = jnp.logical_or(jnp.logical_not(is_prompt), lax.rem(i, tps) >= PAD_TILES)

    def finish(x1):
        w = wts_ref[...]
        acc = x1
        for kk in range(TOP_K):
            acc = acc + w[:, kk:kk + 1] * _load_token_tiles(rows_scr.at[kk], tc)
        return acc * lax.rsqrt(jnp.mean(acc * acc, axis=-1, keepdims=True) + EPS) * lnf_ref[...]

    @pl.when(wanted)
    def _():
        def issue(r, carry):
            for kk in range(TOP_K):
                pltpu.make_async_copy(ys_ref.at[_tile_rows(pos_ref[0, kk, r]), :],
                                      rows_scr.at[kk, _tile_rows(r), :], sem).start(priority=kk % 2)
            return carry

        lax.fori_loop(0, tc, issue, 0, unroll=2)
        for kk in range(TOP_K):
            pltpu.make_async_copy(ys_ref.at[pl.ds(0, tc * TOKEN_SUBLANES), :], rows_scr.at[kk], sem).wait()

        @pl.when(is_prompt)
        def _():
            yp_ref[0] = finish(x1p_ref[...])

        @pl.when(jnp.logical_not(is_prompt))
        def _():
            ysm_ref[...] = finish(x1s_ref[...])


def _combine(pos3, ys, x1p, x1s, wts, lnf, b, lp):
    tc = COMB_TILE
    ntp = x1p.shape[0] // tc
    tps = lp // tc
    seq_tiles = tps - PAD_TILES
    assert x1s.shape[0] == tc and pos3.shape[0] == ntp + 1

    def yp_map(i):
        blk = jnp.where(i >= ntp, seq_tiles - 1, jnp.maximum(lax.rem(i, tps) - PAD_TILES, 0))
        return (jnp.minimum(i // tps, b - 1), blk, 0)

    return pl.pallas_call(
        functools.partial(_combine_kernel, ntp, tps),
        grid=(ntp + 1,),
        in_specs=[
            pl.BlockSpec((1, TOP_K, tc), lambda i: (i, 0, 0), memory_space=pltpu.SMEM),
            pl.BlockSpec(memory_space=pl.ANY),
            pl.BlockSpec((tc, D_MODEL), lambda i: (jnp.minimum(i, ntp - 1), 0)),
            pl.BlockSpec((tc, D_MODEL), lambda i: (0, 0)),
            pl.BlockSpec((tc, LANES), lambda i: (i, 0)),
            pl.BlockSpec((1, D_MODEL), lambda i: (0, 0)),
        ],
        out_specs=[
            pl.BlockSpec((1, tc, D_MODEL), yp_map),
            pl.BlockSpec((tc, D_MODEL), lambda i: (0, 0)),
        ],
        out_shape=[
            jax.ShapeDtypeStruct((b, seq_tiles * tc, D_MODEL), F32),
            jax.ShapeDtypeStruct((tc, D_MODEL), F32),
        ],
        scratch_shapes=[pltpu.VMEM((TOP_K, tc * TOKEN_SUBLANES, LANES), F32), pltpu.SemaphoreType.DMA],
        compiler_params=_cparams(("arbitrary",)),
        name="combine",
    )(pos3, ys, x1p, x1s, wts, lnf)


def _consts():
    r64 = np.zeros((D_MODEL, LANES), np.float32)
    e64 = np.zeros((LANES, D_MODEL), np.float32)
    for h in range(F_HEADS):
        r64[h * F_DH:(h + 1) * F_DH, h] = 1.0 / F_DH
        e64[h, h * F_DH:(h + 1) * F_DH] = 1.0
    return jnp.asarray(r64, BF16), jnp.asarray(e64, BF16), jnp.asarray(e64, F32)


def _tril(n, strict=False):
    return jnp.asarray(np.tril(np.ones((n, n), np.float32), -1 if strict else 0), BF16)


def _round_up(a, b):
    return (a + b - 1) // b * b


def _moe_layer(t_p, t_s, x1_p, x1_s, lg_all, w1, b1, w2, b2, lnf, b, lp):
    tmoe = x1_p.shape[0] + x1_s.shape[0]
    t_route = _round_up(tmoe, ROW_TILE)
    lg_pad = jnp.concatenate([lg_all, jnp.zeros((t_route - tmoe, LANES), F32)], axis=0)
    ids, wts, rank, cnt = _route(lg_pad, _tril(ROW_TILE, strict=True), ROW_TILE, tmoe)
    counts = cnt[0, :N_EXPERTS]
    pcounts = (counts + EXP_TILE - 1) // EXP_TILE * EXP_TILE
    ends = jnp.cumsum(pcounts)
    starts = ends - pcounts
    pos = starts[ids[:tmoe, :TOP_K]] + rank[:tmoe, :TOP_K]
    pos3 = pos.reshape(tmoe // COMB_TILE, COMB_TILE, TOP_K).transpose(0, 2, 1)
    r_pad = _round_up(TOP_K * tmoe + N_EXPERTS * (EXP_TILE - 1), EXP_TILE)
    n_tiles = r_pad // EXP_TILE
    tile_start = jnp.arange(n_tiles, dtype=I32) * EXP_TILE
    tile_expert = jnp.minimum(jnp.sum((ends[None, :] <= tile_start[:, None]).astype(I32), axis=1), N_EXPERTS - 1)
    n_used = (ends[-1] // EXP_TILE).astype(I32)[None]
    pad_lo = jnp.concatenate([starts + counts, ends[-1:]]).astype(I32)
    pad_hi = jnp.concatenate([ends, jnp.full((1,), r_pad, ends.dtype)]).astype(I32)
    n_pad = jnp.sum(pad_hi - pad_lo).astype(I32)[None]
    xsrt = _dispatch(pad_lo, pad_hi, n_pad, pos3, t_p, t_s, r_pad)
    ysrt = _experts(tile_expert, n_used, xsrt, w1, b1, w2, b2)
    return _combine(pos3, ysrt, x1_p, x1_s, wts, lnf, b, lp)


def kernel(x_prompt, x_sample, cache_k, cache_v, cache_logf, state_C, state_n, state_m, page_table, meta_tokens, ln1, w_in, mlstm_b_i, mlstm_b_f, mlstm_head_norm, fox_b_f, fox_q_norm, fox_k_norm, w_branch_mlstm, w_branch_fox, w_out, ln2, w_router, b_router, w_exp_in, b_exp_in, w_exp_out, b_exp_out, ln_final):
    depth = w_in.shape[0]
    b, seq, _ = x_prompt.shape
    db, ds, _ = x_sample.shape
    assert ds == 1 and seq % ATT_BLOCK == 0 and db % MSTEP_SEQS == 0
    l_true = seq + N_META
    lp = seq + ATT_BLOCK
    tp = b * lp
    n_phys, pg = cache_k.shape[1], cache_k.shape[2]
    r64, e64, e64f = _consts()

    xp = jnp.concatenate([jnp.zeros((b, PAD_FRONT, D_MODEL), F32),
                          jnp.broadcast_to(meta_tokens[None].astype(F32), (b, N_META, D_MODEL)),
                          x_prompt], axis=1).reshape(tp, D_MODEL)
    xs = x_sample.reshape(db, D_MODEL)

    outs = {k: [] for k in ("kp", "vp", "lfp", "ks", "vs", "lfs", "cp", "np", "mp", "cs", "ns", "ms")}
    offs = np.cumsum((0,) + (M_HEADS * M_DK, M_HEADS * M_DK, M_HEADS * M_DV, M_HEADS * M_DV, M_HEADS, M_HEADS,
                             F_HEADS * F_DH, F_HEADS * F_DH, F_HEADS * F_DH, F_HEADS, 2 * D_MODEL))
    seg = lambda w, i: w[:, offs[i]:offs[i + 1]]
    for l in range(depth):
        w = w_in[l]
        gates = seg(w, 10)
        wmain = jnp.stack([jnp.concatenate([seg(w, 0), seg(w, 1)], axis=1), seg(w, 2), seg(w, 3), seg(w, 6),
                           seg(w, 7), seg(w, 8), gates[:, :D_MODEL], gates[:, D_MODEL:]]).astype(BF16)
        wsm = jnp.concatenate([seg(w, 4), seg(w, 5), seg(w, 9),
                               jnp.zeros((D_MODEL, LANES - G_END), F32)], axis=1)
        wsh = wsm.astype(BF16)
        wsl = (wsm - wsh.astype(F32)).astype(BF16)
        bias = jnp.concatenate([mlstm_b_i[l], mlstm_b_f[l], fox_b_f[l], jnp.zeros((LANES - G_END,), F32)])[None]
        qn = jnp.tile(fox_q_norm[l], F_HEADS)[None]
        kn = jnp.tile(fox_k_norm[l], F_HEADS)[None]
        hnw = mlstm_head_norm[l][None]
        ln1l = ln1[l][None]
        wbm = w_branch_mlstm[l].astype(BF16)
        wbf = w_branch_fox[l].astype(BF16)
        wo = w_out[l].astype(BF16)
        wr = jnp.concatenate([w_router[l], jnp.zeros((D_MODEL, LANES - N_EXPERTS), F32)], axis=1)
        wrh = wr.astype(BF16)
        wrl = (wr - wrh.astype(F32)).astype(BF16)
        br = jnp.concatenate([b_router[l], jnp.zeros((LANES - N_EXPERTS,), F32)])[None]

        zp, kpt32, vpt32, vpt16, gsp = _proj(xp, ln1l, wmain, wsh, wsl, bias, qn, kn, r64, e64, ROW_TILE)
        zp3 = zp.reshape(b, lp, N_ZB * D_MODEL)
        hnp, kap, qap, c_p, n_p, m_p = _mlstm_prompt(zp3, gsp.reshape(b, lp, LANES), hnw, _tril(M_CHUNK))
        hfp = _fox_prompt(zp3, qap, kap, vpt16)
        x1p, tpn, lgp = _merge(xp, hnp.reshape(tp, D_MODEL), zp, hfp.reshape(tp, D_MODEL), wbm, wbf, wo,
                               ln2[l][None], wrh, wrl, br, ROW_TILE)
        unpad = lambda a: jnp.transpose(a.reshape(F_HEADS, F_DH, b, lp)[:, :, :, PAD_FRONT:], (2, 3, 0, 1))
        outs["kp"].append(unpad(kpt32))
        outs["vp"].append(unpad(vpt32))
        outs["lfp"].append(gsp.reshape(b, lp, LANES)[:, PAD_FRONT:, G_FF:G_END])
        outs["cp"].append(c_p)
        outs["np"].append(n_p[:, :, 0, :])
        outs["mp"].append(m_p[:, :, 0, 0])

        zs, kst32, vst32, _, gss = _proj(xs, ln1l, wmain, wsh, wsl, bias, qn, kn, r64, e64, db)
        vs32 = vst32.T
        m_in = jnp.concatenate([state_m[l], jnp.zeros((db, LANES - M_HEADS), F32)], axis=1)
        hns, c_s, n_s, m_s = _mlstm_step(zs, gss, m_in, state_n[l].reshape(db, M_HEADS * M_DK), state_C[l], hnw)
        n_s = n_s.reshape(db, M_HEADS, M_DK)
        q3 = zs[:, ZB_FQ * D_MODEL:(ZB_FQ + 1) * D_MODEL].astype(F32).reshape(db, 1, D_MODEL)
        kn3 = zs[:, ZB_FK * D_MODEL:(ZB_FK + 1) * D_MODEL].astype(F32).reshape(db, 1, D_MODEL)
        hfs = _fox_decode(page_table, q3, kn3, vs32.reshape(db, 1, D_MODEL),
                          jnp.broadcast_to(gss[:, G_FF:G_END, None], (db, F_HEADS, LANES)),
                          jnp.transpose(cache_k[l], (0, 2, 3, 1)), jnp.transpose(cache_v[l], (0, 2, 3, 1)),
                          jnp.transpose(cache_logf[l], (0, 2, 1)))
        x1s, tsn, lgs = _merge(xs, hns, zs, hfs.reshape(db, D_MODEL).astype(BF16), wbm, wbf, wo,
                               ln2[l][None], wrh, wrl, br, db)
        outs["ks"].append(jnp.transpose(kst32.reshape(F_HEADS, F_DH, db), (2, 0, 1)).reshape(db, 1, F_HEADS, F_DH))
        outs["vs"].append(jnp.transpose(vst32.reshape(F_HEADS, F_DH, db), (2, 0, 1)).reshape(db, 1, F_HEADS, F_DH))
        outs["lfs"].append(gss[:, G_FF:G_END].reshape(db, 1, F_HEADS))
        outs["cs"].append(c_s)
        outs["ns"].append(n_s)
        outs["ms"].append(m_s[:, :M_HEADS])

        if l < depth - 1:
            raise NotImplementedError("deeper stacks need the un-normalised residual stream as well")
        lg_all = jnp.concatenate([lgp, lgs], axis=0)
        y_prompt, y_sample = _moe_layer(tpn, tsn, x1p, x1s, lg_all, w_exp_in[l], b_exp_in[l], w_exp_out[l],
                                        b_exp_out[l], ln_final[None], b, lp)

    st = lambda k: jnp.stack(outs[k])
    return (y_prompt, y_sample.reshape(db, 1, D_MODEL), st("kp"), st("vp"), st("lfp"), st("ks"), st("vs"), st("lfs"),
            st("cp"), st("np"), st("mp"), st("cs"), st("ns"), st("ms"))
```
